```python
import math
import jax, jax.numpy as jnp
from jax import lax
import numpy as np

D_MODEL = 1024
BATCH = 4
SEQ = 4096
DEPTH = 4
DEC_BATCH = 32
DEC_SEQ = 8
PAST_LEN = 8192
PAGE_SIZE = 128

N_A = DEPTH // 2
N_B = DEPTH - N_A
A_HEADS = 8
A_DK = 128
A_DV = 128
A_QK = A_HEADS * A_DK
A_VD = A_HEADS * A_DV
QKV_W = 2 * A_QK + A_VD
A_IN = QKV_W + A_VD + 2 * A_HEADS
A_CONV = 4
A_CHUNK = 64
B_HEADS = 8
B_DH = 64
B_VD = 2 * B_DH
B_QD = 2 * B_HEADS * B_DH
B_KVD = B_QD + B_HEADS * B_VD
ROPE_THETA = 10000.0
Q_BLOCK = 128
D_FF = 2816
F_CONV = 3
EPS = 1e-6

kernel_name = 'yoco_gdn_diffattn_convffn_step'


def rmsnorm(x, g):
    xf = x.astype(jnp.float32)
    y = xf * lax.rsqrt(jnp.mean(xf * xf, axis=-1, keepdims=True) + EPS)
    return (y * g.astype(jnp.float32)).astype(x.dtype)


def l2norm(x):
    xf = x.astype(jnp.float32)
    return xf * lax.rsqrt(jnp.sum(xf * xf, axis=-1, keepdims=True) + EPS)


def causal_dwconv(x, state, w):
    K = w.shape[0]
    L = x.shape[1]
    xx = jnp.concatenate([state.astype(x.dtype), x], axis=1)
    y = xx[:, 0:L] * w[0]
    for i in range(1, K):
        y = y + xx[:, i:i + L] * w[i]
    return y, xx[:, -(K - 1):]


def rope(x, pos):
    half = x.shape[-1] // 2
    inv = 1.0 / (ROPE_THETA ** (jnp.arange(half, dtype=jnp.float32) / half))
    ang = pos.astype(jnp.float32)[:, None] * inv[None, :]
    cos = jnp.cos(ang)[None, :, None, :]
    sin = jnp.sin(ang)[None, :, None, :]
    xf = x.astype(jnp.float32)
    x1, x2 = xf[..., :half], xf[..., half:]
    return jnp.concatenate([x1 * cos - x2 * sin, x2 * cos + x1 * sin], axis=-1).astype(x.dtype)


def gated_delta_chunked(q, k, v, g, beta, S0):
    B, L, H, DK = q.shape
    DV = v.shape[-1]
    C = math.gcd(L, A_CHUNK)
    N = L // C

    def to_chunks(t):
        t = t.reshape((B, N, C, H) + t.shape[3:])
        return jnp.moveaxis(t, (1, 3), (0, 2))

    qc, kc, vc, gc, bc = (to_chunks(t) for t in (q, k, v, g, beta))
    gcum = jnp.cumsum(gc, axis=-1)
    tril = jnp.tril(jnp.ones((C, C), dtype=bool))
    strict = jnp.tril(jnp.ones((C, C), dtype=bool), -1)
    eye = jnp.eye(C, dtype=jnp.float32)

    def step(S, inp):
        qi, ki, vi, gi, bi = inp
        diff = gi[..., :, None] - gi[..., None, :]
        decay = jnp.exp(jnp.where(tril, diff, -jnp.inf))
        kb = ki * bi[..., None]
        lmat = jnp.where(strict, jnp.einsum('bhid,bhjd->bhij', kb, ki) * decay, 0.0)
        rhs = jnp.concatenate([vi * bi[..., None], kb * jnp.exp(gi)[..., None]], axis=-1)
        sol = lax.linalg.triangular_solve(lmat + eye, rhs, left_side=True, lower=True,
                                          unit_diagonal=True)
        u, w = sol[..., :DV], sol[..., DV:]
        v_new = u - jnp.einsum('bhck,bhkv->bhcv', w, S)
        attn = jnp.einsum('bhid,bhjd->bhij', qi, ki) * decay
        o = (jnp.einsum('bhck,bhkv->bhcv', qi * jnp.exp(gi)[..., None], S)
             + jnp.einsum('bhij,bhjv->bhiv', attn, v_new))
        g_last = gi[..., -1]
        S = (S * jnp.exp(g_last)[..., None, None]
             + jnp.einsum('bhck,bhcv->bhkv', ki * jnp.exp(g_last[..., None] - gi)[..., None], v_new))
        return S, o

    S, o = lax.scan(step, S0, (qc, kc, vc, gcum, bc))
    o = jnp.moveaxis(o, (0, 2), (1, 3)).reshape(B, L, H, DV)
    return o, S


def gdn_mixer(h, delta0, conv0, w_in, conv_w, A_log, dt_bias, o_gain, w_out):
    B, L, _ = h.shape
    proj = h @ w_in
    qkv = proj[..., :QKV_W]
    z = proj[..., QKV_W:QKV_W + A_VD]
    b_raw = proj[..., QKV_W + A_VD:QKV_W + A_VD + A_HEADS]
    a_raw = proj[..., QKV_W + A_VD + A_HEADS:]
    qkv, conv_new = causal_dwconv(qkv, conv0, conv_w)
    qkv = jax.nn.silu(qkv)
    q = l2norm(qkv[..., :A_QK].reshape(B, L, A_HEADS, A_DK)) * (A_DK ** -0.5)
    k = l2norm(qkv[..., A_QK:2 * A_QK].reshape(B, L, A_HEADS, A_DK))
    v = qkv[..., 2 * A_QK:].reshape(B, L, A_HEADS, A_DV).astype(jnp.float32)
    beta = jax.nn.sigmoid(b_raw.astype(jnp.float32))
    g = -jnp.exp(A_log.astype(jnp.float32)) * jax.nn.softplus(
        a_raw.astype(jnp.float32) + dt_bias.astype(jnp.float32))
    o, S = gated_delta_chunked(q, k, v, g, beta, delta0.astype(jnp.float32))
    o = rmsnorm(o, o_gain) * jax.nn.silu(z.reshape(B, L, A_HEADS, A_DV).astype(jnp.float32))
    y = o.reshape(B, L, A_VD).astype(h.dtype) @ w_out
    return y, S.astype(h.dtype), conv_new


def conv_ffn(h, conv0, w_up, conv_w, w_down):
    up = h @ w_up
    gate, val = up[..., :D_FF], up[..., D_FF:]
    gate, conv_new = causal_dwconv(gate, conv0, conv_w)
    return (jax.nn.silu(gate) * val) @ w_down, conv_new


def shared_kv(h, pos, kv_norm, w_kv):
    B, L, _ = h.shape
    kv = rmsnorm(h, kv_norm) @ w_kv
    k = rope(kv[..., :B_QD].reshape(B, L, 2 * B_HEADS, B_DH), pos)
    v = kv[..., B_QD:].reshape(B, L, B_HEADS, B_VD)
    return k, v


def diff_attention(q, k, v, q_pos, k_pos, lam):
    B, Lq = q.shape[:2]
    QB = math.gcd(Lq, Q_BLOCK)
    nb = Lq // QB
    qb = q.reshape(B, nb, QB, 2 * B_HEADS, B_DH).swapaxes(0, 1)
    pb = q_pos.reshape(nb, QB)
    scale = B_DH ** -0.5

    def block(args):
        qi, pi = args
        s = jnp.einsum('bqhd,bkhd->bhqk', qi, k).astype(jnp.float32) * scale
        s = jnp.where(k_pos[None, None, None, :] <= pi[None, None, :, None], s, -jnp.inf)
        p = jax.nn.softmax(s, axis=-1).reshape(B, B_HEADS, 2, QB, s.shape[-1])
        a = p[:, :, 0] - lam * p[:, :, 1]
        return jnp.einsum('bhqk,bkhe->bqhe', a.astype(v.dtype), v)

    o = lax.map(block, (qb, pb))
    return o.swapaxes(0, 1).reshape(B, Lq, B_HEADS, B_VD)


def diff_layer(h, k_all, v_all, q_pos, k_pos, w_q, lam_vecs, subln, w_out, lam_init):
    B, L, _ = h.shape
    q = rope((h @ w_q).reshape(B, L, 2 * B_HEADS, B_DH), q_pos)
    lv = lam_vecs.astype(jnp.float32)
    lam = jnp.exp(jnp.sum(lv[0] * lv[1])) - jnp.exp(jnp.sum(lv[2] * lv[3])) + lam_init
    o = diff_attention(q, k_all, v_all, q_pos, k_pos, lam)
    o = rmsnorm(o, subln) * (1.0 - lam_init)
    return o.reshape(B, L, B_HEADS * B_VD) @ w_out


def _trunk(x, pos, delta0, dconv0, fconv0, past_k, past_v, p):
    h = x
    deltas, dconvs, fconvs = [], [], []
    k_new = v_new = k_all = v_all = k_pos = None
    for l in range(DEPTH):
        if l < N_A:
            y, S, c = gdn_mixer(rmsnorm(h, p['a_norm'][l]), delta0[l], dconv0[l], p['a_w_in'][l],
                                p['a_conv_w'][l], p['a_A_log'][l], p['a_dt_bias'][l],
                                p['a_o_gain'][l], p['a_w_out'][l])
            h = h + y
            deltas.append(S)
            dconvs.append(c)
        else:
            if l == N_A:
                k_new, v_new = shared_kv(h, pos, p['kv_norm'], p['w_kv'])
                if past_k is None:
                    k_all, v_all, k_pos = k_new, v_new, pos
                else:
                    k_all = jnp.concatenate([past_k, k_new.astype(past_k.dtype)], axis=1)
                    v_all = jnp.concatenate([past_v, v_new.astype(past_v.dtype)], axis=1)
                    k_pos = jnp.concatenate([jnp.arange(past_k.shape[1], dtype=jnp.int32), pos])
            j = l - N_A
            lam_init = 0.8 - 0.6 * math.exp(-0.3 * l)
            h = h + diff_layer(rmsnorm(h, p['b_norm'][j]), k_all, v_all, pos, k_pos, p['b_w_q'][j],
                               p['b_lambda'][j], p['b_subln'][j], p['b_w_out'][j], lam_init)
        y, c = conv_ffn(rmsnorm(h, p['f_norm'][l]), fconv0[l], p['f_w_up'][l], p['f_conv_w'][l],
                        p['f_w_down'][l])
        h = h + y
        fconvs.append(c)
    return (rmsnorm(h, p['final_norm']), jnp.stack(deltas), jnp.stack(dconvs), jnp.stack(fconvs),
            k_new, v_new)


def setup_inputs(seed: int = 0) -> dict:
    key = jax.random.key(seed)
    ks = iter(jax.random.split(key, 40))
    f32 = jnp.float32

    def nrm(shape, scale):
        return jax.random.normal(next(ks), shape, f32) * scale

    def gain(shape):
        return 1.0 + nrm(shape, 0.02)

    n_pages = PAST_LEN // PAGE_SIZE
    n_used = DEC_BATCH * n_pages
    n_pool = n_used + max(n_used // 4, 1)
    page_table = jax.random.permutation(next(ks), n_pool)[:n_used].reshape(
        DEC_BATCH, n_pages).astype(jnp.int32)
    A = jax.random.uniform(next(ks), (N_A, A_HEADS), f32, 1.0, 16.0)
    dt = jnp.exp(jax.random.uniform(next(ks), (N_A, A_HEADS), f32, math.log(1e-3), math.log(1e-1)))
    return {
        'x_prompt': nrm((BATCH, SEQ, D_MODEL), 1.0),
        'x_sample': nrm((DEC_BATCH, DEC_SEQ, D_MODEL), 1.0),
        'state_delta': nrm((N_A, DEC_BATCH, A_HEADS, A_DK, A_DV), 0.1),
        'state_dconv': nrm((N_A, DEC_BATCH, A_CONV - 1, QKV_W), 1.0),
        'state_fconv': nrm((DEPTH, DEC_BATCH, F_CONV - 1, D_FF), 1.0),
        'cache_k': nrm((n_pool, PAGE_SIZE, 2 * B_HEADS, B_DH), 1.0),
        'cache_v': nrm((n_pool, PAGE_SIZE, B_HEADS, B_VD), 1.0),
        'page_table': page_table,
        'a_norm': gain((N_A, D_MODEL)),
        'a_w_in': nrm((N_A, D_MODEL, A_IN), D_MODEL ** -0.5),
        'a_conv_w': nrm((N_A, A_CONV, QKV_W), A_CONV ** -0.5),
        'a_A_log': jnp.log(A),
        'a_dt_bias': dt + jnp.log(-jnp.expm1(-dt)),
        'a_o_gain': gain((N_A, A_DV)),
        'a_w_out': nrm((N_A, A_VD, D_MODEL), A_VD ** -0.5),
        'kv_norm': gain((D_MODEL,)),
        'w_kv': nrm((D_MODEL, B_KVD), D_MODEL ** -0.5),
        'b_norm': gain((N_B, D_MODEL)),
        'b_w_q': nrm((N_B, D_MODEL, B_QD), D_MODEL ** -0.5),
        'b_lambda': nrm((N_B, 4, B_DH), 0.1),
        'b_subln': gain((N_B, B_VD)),
        'b_w_out': nrm((N_B, B_HEADS * B_VD, D_MODEL), (B_HEADS * B_VD) ** -0.5),
        'f_norm': gain((DEPTH, D_MODEL)),
        'f_w_up': nrm((DEPTH, D_MODEL, 2 * D_FF), D_MODEL ** -0.5),
        'f_conv_w': nrm((DEPTH, F_CONV, D_FF), F_CONV ** -0.5),
        'f_w_down': nrm((DEPTH, D_FF, D_MODEL), D_FF ** -0.5),
        'final_norm': gain((D_MODEL,)),
    }


def reference(x_prompt, x_sample, state_delta, state_dconv, state_fconv, cache_k, cache_v, page_table,
              a_norm, a_w_in, a_conv_w, a_A_log, a_dt_bias, a_o_gain, a_w_out,
              kv_norm, w_kv, b_norm, b_w_q, b_lambda, b_subln, b_w_out,
              f_norm, f_w_up, f_conv_w, f_w_down, final_norm):
    params = {'a_norm': a_norm, 'a_w_in': a_w_in, 'a_conv_w': a_conv_w, 'a_A_log': a_A_log,
              'a_dt_bias': a_dt_bias, 'a_o_gain': a_o_gain, 'a_w_out': a_w_out,
              'kv_norm': kv_norm, 'w_kv': w_kv, 'b_norm': b_norm, 'b_w_q': b_w_q,
              'b_lambda': b_lambda, 'b_subln': b_subln, 'b_w_out': b_w_out,
              'f_norm': f_norm, 'f_w_up': f_w_up, 'f_conv_w': f_conv_w, 'f_w_down': f_w_down,
              'final_norm': final_norm}
    bp, lp = x_prompt.shape[:2]
    dt = x_prompt.dtype
    pos_p = jnp.arange(lp, dtype=jnp.int32)
    y_prompt, p_delta, p_dconv, p_fconv, p_k, p_v = _trunk(
        x_prompt, pos_p,
        jnp.zeros((N_A, bp, A_HEADS, A_DK, A_DV), dt),
        jnp.zeros((N_A, bp, A_CONV - 1, QKV_W), dt),
        jnp.zeros((DEPTH, bp, F_CONV - 1, D_FF), dt),
        None, None, params)
    bs, ls = x_sample.shape[:2]
    past_len = page_table.shape[1] * cache_k.shape[1]
    past_k = cache_k[page_table].reshape(bs, past_len, 2 * B_HEADS, B_DH)
    past_v = cache_v[page_table].reshape(bs, past_len, B_HEADS, B_VD)
    pos_s = past_len + jnp.arange(ls, dtype=jnp.int32)
    y_sample, s_delta, s_dconv, s_fconv, s_k, s_v = _trunk(
        x_sample, pos_s, state_delta, state_dconv, state_fconv, past_k, past_v, params)
    return (y_prompt, y_sample, p_delta, p_dconv, p_fconv, p_k, p_v,
            s_delta, s_dconv, s_fconv, s_k, s_v)
```

```python
import functools
import math

import jax
import jax.numpy as jnp
from jax import lax
from jax.experimental import pallas as pl
from jax.experimental.pallas import tpu as pltpu

F32 = jnp.float32
BF16 = jnp.bfloat16
EPS = 1e-6
ROPE_THETA = 10000.0
LANES = 128
SUBLANES = 8
CHUNK = 64
VMEM_LIMIT = 48 * 1024 * 1024
HI = lax.Precision.HIGHEST


def _cparams(*sem):
    return pltpu.CompilerParams(dimension_semantics=sem, vmem_limit_bytes=VMEM_LIMIT)


def _dot(a, b):
    return jnp.dot(a, b, preferred_element_type=F32)


def _dot_nt(a, b, precision=None):
    return lax.dot_general(a, b, (((1,), (1,)), ((), ())), precision=precision,
                           preferred_element_type=F32)


def _dot_tn(a, b, precision=None):
    return lax.dot_general(a, b, (((0,), (0,)), ((), ())), precision=precision,
                           preferred_element_type=F32)


def _dot_hi(a, b):
    return jnp.dot(a, b, precision=HI, preferred_element_type=F32)


def _rms(x, g):
    r = lax.rsqrt(jnp.mean(x * x, axis=-1, keepdims=True) + EPS)
    return x * r * g


def _silu(x):
    return x * jax.nn.sigmoid(x)


def _shift_rows(x, prev, k):
    ax = x.ndim - 2
    xr = pltpu.roll(x, k, ax)
    pr = pltpu.roll(prev, k, ax)
    shape = [1] * x.ndim
    shape[ax] = SUBLANES
    row = lax.broadcasted_iota(jnp.int32, tuple(shape), ax)
    if x.ndim == 2:
        head = jnp.where(row < k, pr, xr[:SUBLANES])
        if x.shape[0] == SUBLANES:
            return head
        return jnp.concatenate([head, xr[SUBLANES:]], axis=0)
    return jnp.where(row < k, pr, xr)


def _proj_kernel(*refs, rope, scale, n_out):
    if rope:
        x_ref, g_ref, w_ref, wr_ref, cos_ref, sin_ref = refs[:6]
        rest = refs[6:]
    else:
        x_ref, g_ref, w_ref = refs[:3]
        rest = refs[3:]
    outs, xn_ref = rest[:n_out], rest[n_out]

    @pl.when(pl.program_id(1) == 0)
    def _():
        xn_ref[...] = _rms(x_ref[...], g_ref[...]).astype(BF16)

    xn = xn_ref[...]
    y = _dot(xn, w_ref[...])
    if rope:
        yr = _dot(xn, wr_ref[...])
        reps = y.shape[1] // LANES
        cos = jnp.concatenate([cos_ref[...]] * reps, axis=1)
        sin = jnp.concatenate([sin_ref[...]] * reps, axis=1)
        y = y * cos + yr * sin
    if scale != 1.0:
        y = y * scale
    for o in outs:
        o[...] = y.astype(o.dtype)


def _norm_proj(x, g, w, *, w_rot=None, cos=None, sin=None, scale=1.0, out_dtypes=(F32,), tm, tn=512):
    m, d = x.shape
    n = w.shape[1]
    tn = min(tn, n)
    rope = w_rot is not None
    in_specs = [pl.BlockSpec((tm, d), lambda i, j: (i, 0)),
                pl.BlockSpec((1, d), lambda i, j: (0, 0)),
                pl.BlockSpec((d, tn), lambda i, j: (0, j))]
    args = [x, g.reshape(1, d), w]
    if rope:
        pt = cos.shape[0] // tm
        in_specs += [pl.BlockSpec((d, tn), lambda i, j: (0, j)),
                     pl.BlockSpec((tm, LANES), lambda i, j: (i % pt, 0)),
                     pl.BlockSpec((tm, LANES), lambda i, j: (i % pt, 0))]
        args += [w_rot, cos, sin]
    res = pl.pallas_call(
        functools.partial(_proj_kernel, rope=rope, scale=scale, n_out=len(out_dtypes)),
        grid=(m // tm, n // tn),
        in_specs=in_specs,
        out_specs=[pl.BlockSpec((tm, tn), lambda i, j: (i, j)) for _ in out_dtypes],
        out_shape=[jax.ShapeDtypeStruct((m, n), dt) for dt in out_dtypes],
        scratch_shapes=[pltpu.VMEM((tm, d), BF16)],
        compiler_params=_cparams("parallel", "arbitrary"),
        name="norm_proj_rope" if rope else "norm_proj",
    )(*args)
    return res


def _mm_res_kernel(x_ref, w_ref, r_ref, o_ref):
    o_ref[...] = r_ref[...] + _dot(x_ref[...], w_ref[...])


def _mm_res(x, w, res, *, tm, tn=512):
    m, k = x.shape
    n = w.shape[1]
    tn = min(tn, n)
    return pl.pallas_call(
        _mm_res_kernel,
        grid=(m // tm, n // tn),
        in_specs=[pl.BlockSpec((tm, k), lambda i, j: (i, 0)),
                  pl.BlockSpec((k, tn), lambda i, j: (0, j)),
                  pl.BlockSpec((tm, tn), lambda i, j: (i, j))],
        out_specs=pl.BlockSpec((tm, tn), lambda i, j: (i, j)),
        out_shape=jax.ShapeDtypeStruct((m, n), F32),
        compiler_params=_cparams("parallel", "arbitrary"),
        name="mm_residual",
    )(x, w, res)


def _ffn_kernel(*refs, per_seq, tiles_per_seq, nf, final):
    x_ref, g_ref, wg_ref, wv_ref, cw_ref, wd_ref = refs[:6]
    pos = 6
    st_ref = fg_ref = carry_ref = None
    if per_seq:
        st_ref = refs[pos]
        pos += 1
    if final:
        fg_ref = refs[pos]
        pos += 1
    o_ref, tail_ref, xn_ref, acc_ref = refs[pos:pos + 4]
    if not per_seq:
        carry_ref = refs[pos + 4]
    i = pl.program_id(0)
    j = pl.program_id(1)

    @pl.when(j == 0)
    def _():
        xn_ref[...] = _rms(x_ref[...], g_ref[...]).astype(BF16)
        acc_ref[...] = jnp.zeros_like(acc_ref)

    xn = xn_ref[...]
    gate = _dot(xn, wg_ref[...])
    val = _dot(xn, wv_ref[...])
    tm, tf = gate.shape
    cw = cw_ref[...]
    if per_seq:
        g3 = gate.reshape(tm // SUBLANES, SUBLANES, tf)
        prev = st_ref[...]
        tail_ref[...] = g3
        w0, w1, w2 = cw[0:1][None], cw[1:2][None], cw[2:3][None]
    else:
        g3 = gate

        @pl.when((i % tiles_per_seq) == 0)
        def _():
            carry_ref[j] = jnp.zeros((SUBLANES, tf), F32)

        prev = carry_ref[j]
        tail = gate[tm - SUBLANES:]
        tail_ref[0] = tail
        carry_ref[j] = tail
        w0, w1, w2 = cw[0:1], cw[1:2], cw[2:3]
    conv = w2 * g3 + w1 * _shift_rows(g3, prev, 1) + w0 * _shift_rows(g3, prev, 2)
    act = _silu(conv).reshape(tm, tf) * val
    acc_ref[...] += _dot(act.astype(BF16), wd_ref[...])

    @pl.when(j == nf - 1)
    def _():
        y = x_ref[...] + acc_ref[...]
        if final:
            y = _rms(y, fg_ref[...])
        o_ref[...] = y


def _conv_ffn(x, g, wg, wv, cw, wd, *, state=None, final_g=None, seq_len, tm, tf=256):
    m, d = x.shape
    f = wg.shape[1]
    per_seq = state is not None
    final = final_g is not None
    nf = f // tf
    groups = tm // SUBLANES if per_seq else 1
    in_specs = [pl.BlockSpec((tm, d), lambda i, j: (i, 0)),
                pl.BlockSpec((1, d), lambda i, j: (0, 0)),
                pl.BlockSpec((d, tf), lambda i, j: (0, j)),
                pl.BlockSpec((d, tf), lambda i, j: (0, j)),
                pl.BlockSpec((SUBLANES, tf), lambda i, j: (0, j)),
                pl.BlockSpec((tf, d), lambda i, j: (j, 0))]
    args = [x, g.reshape(1, d), wg, wv, cw, wd]
    if per_seq:
        in_specs.append(pl.BlockSpec((groups, SUBLANES, tf), lambda i, j: (i, 0, j)))
        args.append(state)
    if final:
        in_specs.append(pl.BlockSpec((1, d), lambda i, j: (0, 0)))
        args.append(final_g.reshape(1, d))
    scratch = [pltpu.VMEM((tm, d), BF16), pltpu.VMEM((tm, d), F32)]
    if not per_seq:
        scratch.append(pltpu.VMEM((nf, SUBLANES, tf), F32))
    return pl.pallas_call(
        functools.partial(_ffn_kernel, per_seq=per_seq, tiles_per_seq=max(seq_len // tm, 1), nf=nf, final=final),
        grid=(m // tm, nf),
        in_specs=in_specs,
        out_specs=[pl.BlockSpec((tm, d), lambda i, j: (i, 0)),
                   pl.BlockSpec((groups, SUBLANES, tf), lambda i, j: (i, 0, j))],
        out_shape=[jax.ShapeDtypeStruct((m, d), F32),
                   jax.ShapeDtypeStruct((m // tm * groups, SUBLANES, f), F32)],
        scratch_shapes=scratch,
        compiler_params=_cparams("arbitrary", "arbitrary"),
        name="conv_ffn",
    )(*args)


def _gdn_in_kernel(*refs, per_seq, tiles_per_seq, nq, nqkv, nz, qscale):
    x_ref, g_ref, w_ref, cw_ref = refs[:4]
    pos = 4
    st_ref = carry_ref = None
    if per_seq:
        st_ref = refs[pos]
        pos += 1
    o_ref, ba_ref, tail_ref, xn_ref = refs[pos:pos + 4]
    if not per_seq:
        carry_ref = refs[pos + 4]
    i = pl.program_id(0)
    j = pl.program_id(1)

    @pl.when(j == 0)
    def _():
        xn_ref[...] = _rms(x_ref[...], g_ref[...]).astype(BF16)

    pre = _dot(xn_ref[...], w_ref[...])
    tm, tn = pre.shape

    def conv_silu():
        cw = cw_ref[...]
        if per_seq:
            p3 = pre.reshape(tm // SUBLANES, SUBLANES, tn)
            prev = st_ref[...]
            tail_ref[...] = p3
            taps = [cw[t:t + 1][None] for t in range(4)]
        else:
            p3 = pre
            jc = jnp.minimum(j, nqkv - 1)

            @pl.when((i % tiles_per_seq) == 0)
            def _():
                carry_ref[jc] = jnp.zeros((SUBLANES, tn), F32)

            prev = carry_ref[jc]
            tail = pre[tm - SUBLANES:]
            tail_ref[0] = tail
            carry_ref[jc] = tail
            taps = [cw[t:t + 1] for t in range(4)]
        conv = taps[3] * p3
        for t in range(3):
            conv = conv + taps[t] * _shift_rows(p3, prev, 3 - t)
        return _silu(conv).reshape(tm, tn)

    def l2n(y, s):
        parts = []
        for a in range(tn // LANES):
            ya = y[:, a * LANES:(a + 1) * LANES]
            r = lax.rsqrt(jnp.sum(ya * ya, axis=-1, keepdims=True) + EPS)
            parts.append(ya * (r * s) if s != 1.0 else ya * r)
        return jnp.concatenate(parts, axis=1)

    @pl.when(j < nq)
    def _():
        o_ref[...] = l2n(conv_silu(), qscale)

    @pl.when((j >= nq) & (j < 2 * nq))
    def _():
        o_ref[...] = l2n(conv_silu(), 1.0)

    @pl.when((j >= 2 * nq) & (j < nqkv))
    def _():
        o_ref[...] = conv_silu()

    @pl.when((j >= nqkv) & (j < nqkv + nz))
    def _():
        o_ref[...] = pre

    @pl.when(j == nqkv + nz)
    def _():
        ba_ref[...] = pre[:, :2 * LANES]


def _gdn_in(x, g, w_all, cw, *, state=None, seq_len, a_qk, a_qkv, a_vd, qscale, tm, tn=512):
    m, d = x.shape
    per_seq = state is not None
    nq, nqkv, nz = a_qk // tn, a_qkv // tn, a_vd // tn
    nb = nqkv + nz + 1
    groups = tm // SUBLANES if per_seq else 1
    in_specs = [pl.BlockSpec((tm, d), lambda i, j: (i, 0)),
                pl.BlockSpec((1, d), lambda i, j: (0, 0)),
                pl.BlockSpec((d, tn), lambda i, j: (0, j)),
                pl.BlockSpec((SUBLANES, tn), lambda i, j: (0, jnp.minimum(j, nqkv - 1)))]
    args = [x, g.reshape(1, d), w_all, cw]
    if per_seq:
        in_specs.append(pl.BlockSpec((groups, SUBLANES, tn), lambda i, j: (i, 0, jnp.minimum(j, nqkv - 1))))
        args.append(state)
    scratch = [pltpu.VMEM((tm, d), BF16)]
    if not per_seq:
        scratch.append(pltpu.VMEM((nqkv, SUBLANES, tn), F32))
    return pl.pallas_call(
        functools.partial(_gdn_in_kernel, per_seq=per_seq, tiles_per_seq=max(seq_len // tm, 1),
                          nq=nq, nqkv=nqkv, nz=nz, qscale=qscale),
        grid=(m // tm, nb),
        in_specs=in_specs,
        out_specs=[pl.BlockSpec((tm, tn), lambda i, j: (i, jnp.minimum(j, nqkv + nz - 1))),
                   pl.BlockSpec((tm, 2 * LANES), lambda i, j: (i, 0)),
                   pl.BlockSpec((groups, SUBLANES, tn), lambda i, j: (i, 0, jnp.minimum(j, nqkv - 1)))],
        out_shape=[jax.ShapeDtypeStruct((m, a_qkv + a_vd), F32),
                   jax.ShapeDtypeStruct((m, 2 * LANES), F32),
                   jax.ShapeDtypeStruct((m // tm * groups, SUBLANES, a_qkv), F32)],
        scratch_shapes=scratch,
        compiler_params=_cparams("arbitrary", "arbitrary"),
        name="gdn_in",
    )(*args)


def _unit_lower_inverse(lm, row, col, eye):
    def same(s):
        return (row >> s) == (col >> s)

    nd = jnp.where(same(3), -lm, 0.0)
    nd2 = _dot_hi(nd, nd)
    nd4 = _dot_hi(nd2, nd2)
    x = _dot_hi(_dot_hi(eye + nd, eye + nd2), eye + nd4)
    for s in (3, 4, 5):
        off = jnp.where(same(s + 1) & jnp.logical_not(same(s)), lm, 0.0)
        x = x - _dot_hi(_dot_hi(x, off), x)
    return x


def _gdn_chunk_kernel(*refs, heads, dk, dv, has_s0):
    q_ref, k_ref, v_ref, z_ref, ba_ref, na_ref, dt_ref, gain_ref = refs[:8]
    pos = 8
    s0_ref = None
    if has_s0:
        s0_ref = refs[pos]
        pos += 1
    o_ref, sout_ref, s_ref = refs[pos:pos + 3]
    n = pl.program_id(1)

    @pl.when(n == 0)
    def _():
        if has_s0:
            s_ref[...] = s0_ref[0]
        else:
            s_ref[...] = jnp.zeros_like(s_ref)

    cr = q_ref.shape[0]

    def pad(a):
        if cr == CHUNK:
            return a
        return jnp.concatenate([a, jnp.zeros((CHUNK - cr, a.shape[1]), a.dtype)], axis=0)

    ba = ba_ref[...]
    beta = pad(jax.nn.sigmoid(ba[:, :LANES]))
    g = pad(na_ref[...] * jax.nn.softplus(ba[:, LANES:] + dt_ref[...]))
    q, k, v = pad(q_ref[...]), pad(k_ref[...]), pad(v_ref[...])
    z = z_ref[...]

    row = lax.broadcasted_iota(jnp.int32, (CHUNK, CHUNK), 0)
    col = lax.broadcasted_iota(jnp.int32, (CHUNK, CHUNK), 1)
    lower = row >= col
    strict = row > col
    eye = (row == col).astype(F32)
    gc = _dot_hi(lower.astype(F32), g)
    gct = gc.T
    glast = gc[CHUNK - 1:CHUNK]
    eg = jnp.exp(gc)
    egl = jnp.exp(glast - gc)
    eglast = jnp.exp(glast)
    gain = gain_ref[...]

    for h in range(heads):
        qh = q[:, h * dk:(h + 1) * dk]
        kh = k[:, h * dk:(h + 1) * dk]
        vh = v[:, h * dv:(h + 1) * dv]
        bcol = beta[:, h:h + 1]
        decay = jnp.where(lower, jnp.exp(gc[:, h:h + 1] - gct[h:h + 1, :]), 0.0)
        kb = kh * bcol
        lm = jnp.where(strict, _dot_nt(kb, kh, HI) * decay, 0.0)
        t = _unit_lower_inverse(lm, row, col, eye)
        rhs = jnp.concatenate([vh * bcol, kb * eg[:, h:h + 1]], axis=1)
        sol = _dot_hi(t, rhs)
        u, w = sol[:, :dv], sol[:, dv:]
        sh = s_ref[h]
        v_new = u - _dot_hi(w, sh)
        attn = _dot_nt(qh, kh, HI) * decay
        o = _dot_hi(qh * eg[:, h:h + 1], sh) + _dot_hi(attn, v_new)
        s_ref[h] = sh * eglast[:, h:h + 1] + _dot_tn(kh * egl[:, h:h + 1], v_new, HI)
        o = o[:cr]
        zh = z[:, h * dv:(h + 1) * dv]
        on = o * lax.rsqrt(jnp.mean(o * o, axis=-1, keepdims=True) + EPS) * gain * _silu(zh)
        o_ref[:, h * dv:(h + 1) * dv] = on.astype(o_ref.dtype)

    @pl.when(n == pl.num_programs(1) - 1)
    def _():
        sout_ref[0] = s_ref[...]


def _gdn_chunk(qkvz, ba, neg_a, dt_bias, gain, *, s0, batch, seq_len, heads, dk, dv):
    m = qkvz.shape[0]
    cr = min(CHUNK, seq_len)
    nc = seq_len // cr
    hd = heads * dk
    has_s0 = s0 is not None

    def blk(c):
        return pl.BlockSpec((cr, hd), lambda b, n: (b * nc + n, c))

    in_specs = [blk(0), blk(1), blk(2), blk(3),
                pl.BlockSpec((cr, 2 * LANES), lambda b, n: (b * nc + n, 0)),
                pl.BlockSpec((1, LANES), lambda b, n: (0, 0)),
                pl.BlockSpec((1, LANES), lambda b, n: (0, 0)),
                pl.BlockSpec((1, dv), lambda b, n: (0, 0))]
    args = [qkvz, qkvz, qkvz, qkvz, ba, neg_a, dt_bias, gain.reshape(1, dv)]
    if has_s0:
        in_specs.append(pl.BlockSpec((1, heads, dk, dv), lambda b, n: (b, 0, 0, 0)))
        args.append(s0)
    return pl.pallas_call(
        functools.partial(_gdn_chunk_kernel, heads=heads, dk=dk, dv=dv, has_s0=has_s0),
        grid=(batch, nc),
        in_specs=in_specs,
        out_specs=[pl.BlockSpec((cr, hd), lambda b, n: (b * nc + n, 0)),
                   pl.BlockSpec((1, heads, dk, dv), lambda b, n: (b, 0, 0, 0))],
        out_shape=[jax.ShapeDtypeStruct((m, hd), BF16),
                   jax.ShapeDtypeStruct((batch, heads, dk, dv), F32)],
        scratch_shapes=[pltpu.VMEM((heads, dk, dv), F32)],
        compiler_params=_cparams("parallel", "arbitrary"),
        name="gdn_chunk",
    )(*args)


def _lambda(lv_ref, lam_init):
    lv = lv_ref[...]
    a = jnp.sum(lv[0:1] * lv[1:2], axis=-1, keepdims=True)
    b = jnp.sum(lv[2:3] * lv[3:4], axis=-1, keepdims=True)
    return jnp.exp(a) - jnp.exp(b) + lam_init


def _flash_kernel(q_ref, k_ref, v_ref, lv_ref, sub_ref, o_ref, qs_ref, m_ref, l_ref, acc_ref, *, dh, lam_init):
    qi = pl.program_id(2)
    ki = pl.program_id(3)
    tq = q_ref.shape[0]
    tk = k_ref.shape[0]

    @pl.when(ki == 0)
    def _():
        q = q_ref[...]
        lane = lax.broadcasted_iota(jnp.int32, q.shape, 1)
        zero = jnp.zeros_like(q)
        qs_ref[:tq] = jnp.where(lane < dh, q, zero)
        qs_ref[tq:] = jnp.where(lane >= dh, q, zero)
        m_ref[...] = jnp.full_like(m_ref, -jnp.inf)
        l_ref[...] = jnp.zeros_like(l_ref)
        acc_ref[...] = jnp.zeros_like(acc_ref)

    def step(masked):
        s = _dot_nt(qs_ref[...], k_ref[...])
        if masked:
            r = lax.broadcasted_iota(jnp.int32, s.shape, 0)
            c = lax.broadcasted_iota(jnp.int32, s.shape, 1)
            r = jnp.where(r >= tq, r - tq, r)
            s = jnp.where(c <= r, s, -jnp.inf)
        m_old = m_ref[...]
        m_new = jnp.maximum(m_old, jnp.max(s, axis=-1, keepdims=True))
        alpha = jnp.exp(m_old - m_new)
        p = jnp.exp(s - m_new)
        l_ref[...] = alpha * l_ref[...] + jnp.sum(p, axis=-1, keepdims=True)
        acc_ref[...] = alpha * acc_ref[...] + _dot(p.astype(BF16), v_ref[...])
        m_ref[...] = m_new

    @pl.when(ki < qi)
    def _():
        step(False)

    @pl.when(ki == qi)
    def _():
        step(True)

    @pl.when(ki == pl.num_programs(3) - 1)
    def _():
        lam = _lambda(lv_ref, lam_init)
        o1 = acc_ref[:tq] / l_ref[:tq]
        o2 = acc_ref[tq:] / l_ref[tq:]
        o = o1 - lam * o2
        o_ref[...] = (_rms(o, sub_ref[...]) * (1.0 - lam_init)).astype(o_ref.dtype)


def _flash_prompt(q, k, v, lam_vecs, subln, *, batch, seq_len, dh, vd, lam_init, tq=512):
    m, width = q.shape
    pairs = width // (2 * dh)
    tq = min(tq, seq_len)
    nq = seq_len // tq
    return pl.pallas_call(
        functools.partial(_flash_kernel, dh=dh, lam_init=lam_init),
        grid=(batch, pairs, nq, nq),
        in_specs=[pl.BlockSpec((tq, 2 * dh), lambda b, h, i, j: (b * nq + i, h)),
                  pl.BlockSpec((tq, 2 * dh), lambda b, h, i, j: (b * nq + jnp.minimum(i, j), h)),
                  pl.BlockSpec((tq, vd), lambda b, h, i, j: (b * nq + jnp.minimum(i, j), h)),
                  pl.BlockSpec(lam_vecs.shape, lambda b, h, i, j: (0, 0)),
                  pl.BlockSpec((1, vd), lambda b, h, i, j: (0, 0))],
        out_specs=pl.BlockSpec((tq, vd), lambda b, h, i, j: (b * nq + i, h)),
        out_shape=jax.ShapeDtypeStruct((m, pairs * vd), BF16),
        scratch_shapes=[pltpu.VMEM((2 * tq, 2 * dh), BF16),
                        pltpu.VMEM((2 * tq, 1), F32),
                        pltpu.VMEM((2 * tq, 1), F32),
                        pltpu.VMEM((2 * tq, vd), F32)],
        compiler_params=_cparams("parallel", "parallel", "parallel", "arbitrary"),
        name="diff_flash",
    )(q, k, v, lam_vecs, subln.reshape(1, vd))


def _paged_kernel(pt_ref, q_ref, kn_ref, vn_ref, kc_ref, vc_ref, lv_ref, sub_ref, o_ref,
                  qbd_ref, m_ref, l_ref, acc_ref, *, nh, dh, vd, lam_init):
    del pt_ref
    p = pl.program_id(1)
    t = q_ref.shape[1]
    width = q_ref.shape[2]
    rows = nh * t
    page = kc_ref.shape[2]
    nvh = width // vd

    def accum(kt16, v16, mask):
        s = _dot(qbd_ref[...], kt16)
        if mask is not None:
            s = jnp.where(mask, s, -jnp.inf)
        m_old = m_ref[...]
        m_new = jnp.maximum(m_old, jnp.max(s, axis=-1, keepdims=True))
        alpha = jnp.exp(m_old - m_new)
        pr = jnp.exp(s - m_new)
        l_ref[...] = alpha * l_ref[...] + jnp.sum(pr, axis=-1, keepdims=True)
        acc_ref[...] = alpha * acc_ref[...] + _dot(pr.astype(BF16), v16)
        m_ref[...] = m_new

    @pl.when(p == 0)
    def _():
        q = q_ref[0].astype(F32)
        q3 = jnp.broadcast_to(q[None], (nh, t, width))
        hd = lax.broadcasted_iota(jnp.int32, (nh, t, width), 0)
        ln = lax.broadcasted_iota(jnp.int32, (nh, t, width), 2)
        qbd = jnp.where((ln >= hd * dh) & (ln < (hd + 1) * dh), q3, 0.0)
        qbd_ref[...] = qbd.reshape(rows, width).astype(BF16)
        m_ref[...] = jnp.full_like(m_ref, -jnp.inf)
        l_ref[...] = jnp.zeros_like(l_ref)
        acc_ref[...] = jnp.zeros_like(acc_ref)
        zpad = jnp.zeros((page - t, width), F32)
        kt16 = jnp.concatenate([kn_ref[0], zpad], axis=0).T.astype(BF16)
        v16 = jnp.concatenate([vn_ref[0], zpad], axis=0).astype(BF16)
        r = lax.broadcasted_iota(jnp.int32, (rows, page), 0)
        c = lax.broadcasted_iota(jnp.int32, (rows, page), 1)
        accum(kt16, v16, c <= (r & (t - 1)))

    @pl.when(p > 0)
    def _():
        v = jnp.concatenate([vc_ref[0, pl.ds(h, page, stride=nvh), :] for h in range(nvh)], axis=1)
        accum(kc_ref[0].astype(BF16), v.astype(BF16), None)

    @pl.when(p == pl.num_programs(1) - 1)
    def _():
        lam = _lambda(lv_ref, lam_init)
        r = lax.broadcasted_iota(jnp.int32, (rows, 1), 0)
        odd = ((r // t) & 1) == 1
        wgt = jnp.where(odd, -lam, 1.0) / l_ref[...]
        a3 = (acc_ref[...] * wgt).reshape(nh, t, width)
        hd = lax.broadcasted_iota(jnp.int32, (nh, t, width), 0)
        ln = lax.broadcasted_iota(jnp.int32, (nh, t, width), 2)
        pair = hd >> 1
        o = jnp.sum(jnp.where((ln >= pair * vd) & (ln < (pair + 1) * vd), a3, 0.0), axis=0)
        sub = sub_ref[...]
        for h in range(width // vd):
            oh = o[:, h * vd:(h + 1) * vd]
            o_ref[0, :, h * vd:(h + 1) * vd] = (_rms(oh, sub) * (1.0 - lam_init)).astype(o_ref.dtype)


def _paged_attention(q, k_new, v_new, cache_k, cache_v, page_table, lam_vecs, subln, *, dh, vd, lam_init):
    b, t, width = q.shape
    npg = page_table.shape[1]
    page = cache_k.shape[2]
    nh = width // dh
    nvh = width // vd
    assert t & (t - 1) == 0 and t <= page and nh == 2 * nvh and cache_v.shape[1] == page * nvh

    def pidx(bb, p, pt):
        return (pt[bb * npg + jnp.maximum(p - 1, 0)], 0, 0)

    grid_spec = pltpu.PrefetchScalarGridSpec(
        num_scalar_prefetch=1,
        grid=(b, npg + 1),
        in_specs=[pl.BlockSpec((1, t, width), lambda bb, p, pt: (bb, 0, 0)),
                  pl.BlockSpec((1, t, width), lambda bb, p, pt: (bb, 0, 0)),
                  pl.BlockSpec((1, t, width), lambda bb, p, pt: (bb, 0, 0)),
                  pl.BlockSpec((1, width, page), pidx),
                  pl.BlockSpec((1, page * nvh, vd), pidx),
                  pl.BlockSpec(lam_vecs.shape, lambda bb, p, pt: (0, 0)),
                  pl.BlockSpec((1, vd), lambda bb, p, pt: (0, 0))],
        out_specs=pl.BlockSpec((1, t, width), lambda bb, p, pt: (bb, 0, 0)),
        scratch_shapes=[pltpu.VMEM((nh * t, width), BF16),
                        pltpu.VMEM((nh * t, 1), F32),
                        pltpu.VMEM((nh * t, 1), F32),
                        pltpu.VMEM((nh * t, width), F32)])
    return pl.pallas_call(
        functools.partial(_paged_kernel, nh=nh, dh=dh, vd=vd, lam_init=lam_init),
        grid_spec=grid_spec,
        out_shape=jax.ShapeDtypeStruct((b, t, width), BF16),
        compiler_params=_cparams("parallel", "arbitrary"),
        name="diff_paged",
    )(page_table.reshape(-1), q, k_new, v_new, cache_k, cache_v, lam_vecs, subln.reshape(1, vd))


def _rot_weight(w, dh):
    k, n = w.shape
    w4 = w.reshape(k, n // dh, 2, dh // 2)
    return jnp.stack([-w4[:, :, 1], w4[:, :, 0]], axis=2).reshape(k, n)


def _rope_tables(pos, dh):
    half = dh // 2
    inv = 1.0 / (ROPE_THETA ** (jnp.arange(half, dtype=F32) / half))
    ang = pos.astype(F32)[:, None] * inv[None, :]
    reps = LANES // half
    return jnp.tile(jnp.cos(ang), (1, reps)), jnp.tile(jnp.sin(ang), (1, reps))


def _pad_rows(a, rows, front):
    pad = [(0, 0)] * a.ndim
    pad[-2] = (rows - a.shape[-2], 0) if front else (0, rows - a.shape[-2])
    return jnp.pad(a, pad)


def _prep_weights(p):
    n_a, d, a_in = p["a_w_in"].shape
    heads = p["a_A_log"].shape[1]
    dv = p["a_o_gain"].shape[1]
    a_vd = heads * dv
    a_qkv = a_in - a_vd - 2 * heads
    dh = p["b_lambda"].shape[-1]
    w = {}
    w_in = p["a_w_in"]
    tn = 512
    zpad = jnp.zeros((n_a, d, LANES - heads), F32)
    w["a_w_all"] = jnp.concatenate(
        [w_in[:, :, :a_qkv + a_vd], w_in[:, :, a_qkv + a_vd:a_qkv + a_vd + heads], zpad,
         w_in[:, :, a_qkv + a_vd + heads:], zpad, jnp.zeros((n_a, d, tn - 2 * LANES), F32)], axis=2).astype(BF16)
    w["a_cw"] = _pad_rows(p["a_conv_w"], SUBLANES, front=False)
    hp = jnp.zeros((n_a, LANES - heads), F32)
    w["a_neg_a"] = jnp.concatenate([-jnp.exp(p["a_A_log"].astype(F32)), hp], axis=1)[:, None, :]
    w["a_dt"] = jnp.concatenate([p["a_dt_bias"].astype(F32), hp], axis=1)[:, None, :]
    w["a_w_out"] = p["a_w_out"].astype(BF16)
    kq = p["w_kv"].shape[1] - (p["b_w_out"].shape[1])
    w["w_k"] = p["w_kv"][:, :kq].astype(BF16)
    w["w_k_rot"] = _rot_weight(p["w_kv"][:, :kq], dh).astype(BF16)
    w["w_v"] = p["w_kv"][:, kq:].astype(BF16)
    w["b_w_q"] = p["b_w_q"].astype(BF16)
    w["b_w_q_rot"] = jnp.stack([_rot_weight(p["b_w_q"][j], dh) for j in range(p["b_w_q"].shape[0])]).astype(BF16)
    w["b_w_out"] = p["b_w_out"].astype(BF16)
    f = p["f_w_down"].shape[1]
    w["f_wg"] = p["f_w_up"][:, :, :f].astype(BF16)
    w["f_wv"] = p["f_w_up"][:, :, f:].astype(BF16)
    w["f_cw"] = _pad_rows(p["f_conv_w"], SUBLANES, front=False)
    w["f_wd"] = p["f_w_down"].astype(BF16)
    return w


def _trunk(x, pos, p, w, *, delta0, dconv0, fconv0, cache_k, cache_v, page_table):
    b, l, d = x.shape
    m = b * l
    sample = page_table is not None
    depth = p["f_norm"].shape[0]
    n_a = p["a_norm"].shape[0]
    heads = p["a_A_log"].shape[1]
    dv = p["a_o_gain"].shape[1]
    dk = (p["a_w_in"].shape[2] - 2 * heads - 2 * heads * dv) // (2 * heads)
    a_vd = heads * dv
    a_qk = heads * dk
    a_qkv = 2 * a_qk + a_vd
    dh = p["b_lambda"].shape[-1]
    vd = p["b_subln"].shape[-1]
    f = p["f_w_down"].shape[1]
    tm = m if sample else min(512, l)
    assert m % tm == 0 and (sample or l % tm == 0)

    h = x.reshape(m, d)
    cos, sin = _rope_tables(pos, dh)
    if sample:
        cos, sin = jnp.tile(cos, (b, 1)), jnp.tile(sin, (b, 1))

    def tails_to_state(tails, rows):
        if sample:
            return tails[:, SUBLANES - rows:, :]
        per = l // tm
        return tails.reshape(b, per, SUBLANES, -1)[:, per - 1, SUBLANES - rows:, :]

    deltas, dconvs, fconvs = [], [], []
    k_new = v_new = k16 = v16 = None
    for layer in range(depth):
        if layer < n_a:
            st = _pad_rows(dconv0[layer], SUBLANES, front=True) if sample else None
            qkvz, ba, tails = _gdn_in(h, p["a_norm"][layer], w["a_w_all"][layer], w["a_cw"][layer], state=st,
                                      seq_len=l, a_qk=a_qk, a_qkv=a_qkv, a_vd=a_vd, qscale=dk ** -0.5, tm=tm)
            dconvs.append(tails_to_state(tails, p["a_conv_w"].shape[1] - 1))
            o, s_fin = _gdn_chunk(qkvz, ba, w["a_neg_a"][layer], w["a_dt"][layer], p["a_o_gain"][layer],
                                  s0=delta0[layer] if sample else None, batch=b, seq_len=l,
                                  heads=heads, dk=dk, dv=dv)
            deltas.append(s_fin)
            h = _mm_res(o, w["a_w_out"][layer], h, tm=tm)
        else:
            if layer == n_a:
                k_new, k16 = _norm_proj(h, p["kv_norm"], w["w_k"], w_rot=w["w_k_rot"], cos=cos, sin=sin,
                                        out_dtypes=(F32, BF16), tm=tm)
                v_new, v16 = _norm_proj(h, p["kv_norm"], w["w_v"], out_dtypes=(F32, BF16), tm=tm)
            j = layer - n_a
            lam_init = 0.8 - 0.6 * math.exp(-0.3 * layer)
            (q16,) = _norm_proj(h, p["b_norm"][j], w["b_w_q"][j], w_rot=w["b_w_q_rot"][j], cos=cos, sin=sin,
                                scale=dh ** -0.5, out_dtypes=(BF16,), tm=tm)
            if sample:
                width = q16.shape[1]
                o = _paged_attention(q16.reshape(b, l, width), k_new.reshape(b, l, width), v_new.reshape(b, l, width),
                                     cache_k, cache_v, page_table, p["b_lambda"][j], p["b_subln"][j],
                                     dh=dh, vd=vd, lam_init=lam_init).reshape(m, width)
            else:
                o = _flash_prompt(q16, k16, v16, p["b_lambda"][j], p["b_subln"][j], batch=b, seq_len=l,
                                  dh=dh, vd=vd, lam_init=lam_init)
            h = _mm_res(o, w["b_w_out"][j], h, tm=tm)
        st = _pad_rows(fconv0[layer], SUBLANES, front=True) if sample else None
        h, tails = _conv_ffn(h, p["f_norm"][layer], w["f_wg"][layer], w["f_wv"][layer], w["f_cw"][layer],
                             w["f_wd"][layer], state=st, final_g=p["final_norm"] if layer == depth - 1 else None,
                             seq_len=l, tm=tm)
        fconvs.append(tails_to_state(tails, p["f_conv_w"].shape[1] - 1))
    nkh = k_new.shape[1] // dh
    return (h.reshape(b, l, d), jnp.stack(deltas), jnp.stack(dconvs), jnp.stack(fconvs),
            k_new.reshape(b, l, nkh, dh), v_new.reshape(b, l, v_new.shape[1] // vd, vd))


def kernel(x_prompt, x_sample, state_delta, state_dconv, state_fconv, cache_k, cache_v, page_table, a_norm, a_w_in, a_conv_w, a_A_log, a_dt_bias, a_o_gain, a_w_out, kv_norm, w_kv, b_norm, b_w_q, b_lambda, b_subln, b_w_out, f_norm, f_w_up, f_conv_w, f_w_down, final_norm):
    p = dict(a_norm=a_norm, a_w_in=a_w_in, a_conv_w=a_conv_w, a_A_log=a_A_log, a_dt_bias=a_dt_bias,
             a_o_gain=a_o_gain, a_w_out=a_w_out, kv_norm=kv_norm, w_kv=w_kv, b_norm=b_norm, b_w_q=b_w_q,
             b_lambda=b_lambda, b_subln=b_subln, b_w_out=b_w_out, f_norm=f_norm, f_w_up=f_w_up,
             f_conv_w=f_conv_w, f_w_down=f_w_down, final_norm=final_norm)
    w = _prep_weights(p)
    lp = x_prompt.shape[1]
    prompt = _trunk(x_prompt, jnp.arange(lp, dtype=jnp.int32), p, w, delta0=None, dconv0=None, fconv0=None,
                    cache_k=None, cache_v=None, page_table=None)
    ls = x_sample.shape[1]
    past_len = page_table.shape[1] * cache_k.shape[1]
    pool, page = cache_k.shape[:2]
    sample = _trunk(x_sample, past_len + jnp.arange(ls, dtype=jnp.int32), p, w, delta0=state_delta,
                    dconv0=state_dconv, fconv0=state_fconv,
                    cache_k=cache_k.transpose(0, 2, 3, 1).reshape(pool, -1, page),
                    cache_v=cache_v.reshape(pool, page * cache_v.shape[2], cache_v.shape[3]),
                    page_table=page_table)
    return (prompt[0], sample[0]) + prompt[1:] + sample[1:]
```

```python
import functools
import math

import jax
import jax.numpy as jnp
from jax import lax
from jax.experimental import pallas as pl
from jax.experimental.pallas import tpu as pltpu

F32 = jnp.float32
BF16 = jnp.bfloat16
EPS = 1e-6
ROPE_THETA = 10000.0
LANES = 128
SUBLANES = 8
CHUNK = 64
VMEM_LIMIT = 48 * 1024 * 1024
HI = lax.Precision.HIGHEST


def _cparams(*sem):
    return pltpu.CompilerParams(dimension_semantics=sem, vmem_limit_bytes=VMEM_LIMIT)


def _dot(a, b):
    return jnp.dot(a, b, preferred_element_type=F32)


def _dot_nt(a, b, precision=None):
    return lax.dot_general(a, b, (((1,), (1,)), ((), ())), precision=precision,
                           preferred_element_type=F32)


def _dot_tn(a, b, precision=None):
    return lax.dot_general(a, b, (((0,), (0,)), ((), ())), precision=precision,
                           preferred_element_type=F32)


def _dot_hi(a, b):
    return jnp.dot(a, b, precision=HI, preferred_element_type=F32)


def _rms(x, g):
    r = lax.rsqrt(jnp.mean(x * x, axis=-1, keepdims=True) + EPS)
    return x * r * g


def _silu(x):
    return x * jax.nn.sigmoid(x)


def _shift_rows(x, prev, k):
    ax = x.ndim - 2
    xr = pltpu.roll(x, k, ax)
    pr = pltpu.roll(prev, k, ax)
    shape = [1] * x.ndim
    shape[ax] = SUBLANES
    row = lax.broadcasted_iota(jnp.int32, tuple(shape), ax)
    if x.ndim == 2:
        head = jnp.where(row < k, pr, xr[:SUBLANES])
        if x.shape[0] == SUBLANES:
            return head
        return jnp.concatenate([head, xr[SUBLANES:]], axis=0)
    return jnp.where(row < k, pr, xr)


def _proj_kernel(*refs, rope, scale, n_out):
    if rope:
        x_ref, g_ref, w_ref, wr_ref, cos_ref, sin_ref = refs[:6]
        rest = refs[6:]
    else:
        x_ref, g_ref, w_ref = refs[:3]
        rest = refs[3:]
    outs, xn_ref = rest[:n_out], rest[n_out]

    @pl.when(pl.program_id(1) == 0)
    def _():
        xn_ref[...] = _rms(x_ref[...], g_ref[...]).astype(BF16)

    xn = xn_ref[...]
    y = _dot(xn, w_ref[...])
    if rope:
        yr = _dot(xn, wr_ref[...])
        reps = y.shape[1] // LANES
        cos = jnp.concatenate([cos_ref[...]] * reps, axis=1)
        sin = jnp.concatenate([sin_ref[...]] * reps, axis=1)
        y = y * cos + yr * sin
    if scale != 1.0:
        y = y * scale
    for o in outs:
        o[...] = y.astype(o.dtype)


def _norm_proj(x, g, w, *, w_rot=None, cos=None, sin=None, scale=1.0, out_dtypes=(F32,), tm, tn=512):
    m, d = x.shape
    n = w.shape[1]
    tn = min(tn, n)
    rope = w_rot is not None
    in_specs = [pl.BlockSpec((tm, d), lambda i, j: (i, 0)),
                pl.BlockSpec((1, d), lambda i, j: (0, 0)),
                pl.BlockSpec((d, tn), lambda i, j: (0, j))]
    args = [x, g.reshape(1, d), w]
    if rope:
        pt = cos.shape[0] // tm
        in_specs += [pl.BlockSpec((d, tn), lambda i, j: (0, j)),
                     pl.BlockSpec((tm, LANES), lambda i, j: (i % pt, 0)),
                     pl.BlockSpec((tm, LANES), lambda i, j: (i % pt, 0))]
        args += [w_rot, cos, sin]
    res = pl.pallas_call(
        functools.partial(_proj_kernel, rope=rope, scale=scale, n_out=len(out_dtypes)),
        grid=(m // tm, n // tn),
        in_specs=in_specs,
        out_specs=[pl.BlockSpec((tm, tn), lambda i, j: (i, j)) for _ in out_dtypes],
        out_shape=[jax.ShapeDtypeStruct((m, n), dt) for dt in out_dtypes],
        scratch_shapes=[pltpu.VMEM((tm, d), BF16)],
        compiler_params=_cparams("parallel", "arbitrary"),
        name="norm_proj_rope" if rope else "norm_proj",
    )(*args)
    return res


def _proj_t_kernel(*refs, rope, scale):
    if rope:
        x_ref, g_ref, wt_ref, wrt_ref, cos_ref, sin_ref, o_ref, xn_ref = refs
    else:
        x_ref, g_ref, wt_ref, o_ref, xn_ref = refs

    @pl.when(pl.program_id(1) == 0)
    def _():
        xn_ref[...] = _rms(x_ref[...], g_ref[...]).astype(BF16)

    xn = xn_ref[...]
    y = _dot_nt(wt_ref[...], xn)
    if rope:
        yr = _dot_nt(wrt_ref[...], xn)
        reps = y.shape[0] // LANES
        cos = jnp.concatenate([cos_ref[...]] * reps, axis=0)
        sin = jnp.concatenate([sin_ref[...]] * reps, axis=0)
        y = y * cos + yr * sin
    if scale != 1.0:
        y = y * scale
    o_ref[...] = y.astype(o_ref.dtype)


def _norm_proj_t(x, g, wt, *, wt_rot=None, cos_t=None, sin_t=None, scale=1.0, tm, tn=512):
    m, d = x.shape
    n = wt.shape[0]
    tn = min(tn, n)
    rope = wt_rot is not None
    in_specs = [pl.BlockSpec((tm, d), lambda i, j: (i, 0)),
                pl.BlockSpec((1, d), lambda i, j: (0, 0)),
                pl.BlockSpec((tn, d), lambda i, j: (j, 0))]
    args = [x, g.reshape(1, d), wt]
    if rope:
        pt = cos_t.shape[1] // tm
        in_specs += [pl.BlockSpec((tn, d), lambda i, j: (j, 0)),
                     pl.BlockSpec((LANES, tm), lambda i, j: (0, i % pt)),
                     pl.BlockSpec((LANES, tm), lambda i, j: (0, i % pt))]
        args += [wt_rot, cos_t, sin_t]
    return pl.pallas_call(
        functools.partial(_proj_t_kernel, rope=rope, scale=scale),
        grid=(m // tm, n // tn),
        in_specs=in_specs,
        out_specs=pl.BlockSpec((tn, tm), lambda i, j: (j, i)),
        out_shape=jax.ShapeDtypeStruct((n, m), BF16),
        scratch_shapes=[pltpu.VMEM((tm, d), BF16)],
        compiler_params=_cparams("parallel", "arbitrary"),
        name="norm_proj_t_rope" if rope else "norm_proj_t",
    )(*args)


def _mm_res_kernel(x_ref, w_ref, r_ref, o_ref):
    o_ref[...] = r_ref[...] + _dot(x_ref[...], w_ref[...])


def _mm_res(x, w, res, *, tm, tn=512):
    m, k = x.shape
    n = w.shape[1]
    tn = min(tn, n)
    return pl.pallas_call(
        _mm_res_kernel,
        grid=(m // tm, n // tn),
        in_specs=[pl.BlockSpec((tm, k), lambda i, j: (i, 0)),
                  pl.BlockSpec((k, tn), lambda i, j: (0, j)),
                  pl.BlockSpec((tm, tn), lambda i, j: (i, j))],
        out_specs=pl.BlockSpec((tm, tn), lambda i, j: (i, j)),
        out_shape=jax.ShapeDtypeStruct((m, n), F32),
        compiler_params=_cparams("parallel", "arbitrary"),
        name="mm_residual",
    )(x, w, res)


def _ffn_kernel(*refs, per_seq, tiles_per_seq, nf, final):
    x_ref, g_ref, wg_ref, wv_ref, cw_ref, wd_ref = refs[:6]
    pos = 6
    st_ref = fg_ref = carry_ref = None
    if per_seq:
        st_ref = refs[pos]
        pos += 1
    if final:
        fg_ref = refs[pos]
        pos += 1
    o_ref, tail_ref, xn_ref, acc_ref = refs[pos:pos + 4]
    if not per_seq:
        carry_ref = refs[pos + 4]
    i = pl.program_id(0)
    j = pl.program_id(1)

    @pl.when(j == 0)
    def _():
        xn_ref[...] = _rms(x_ref[...], g_ref[...]).astype(BF16)
        acc_ref[...] = jnp.zeros_like(acc_ref)

    xn = xn_ref[...]
    gate = _dot(xn, wg_ref[...])
    val = _dot(xn, wv_ref[...])
    tm, tf = gate.shape
    cw = cw_ref[...]
    if per_seq:
        g3 = gate.reshape(tm // SUBLANES, SUBLANES, tf)
        prev = st_ref[...]
        tail_ref[...] = g3
        w0, w1, w2 = cw[0:1][None], cw[1:2][None], cw[2:3][None]
    else:
        g3 = gate

        @pl.when((i % tiles_per_seq) == 0)
        def _():
            carry_ref[j] = jnp.zeros((SUBLANES, tf), F32)

        prev = carry_ref[j]
        tail = gate[tm - SUBLANES:]
        tail_ref[0] = tail
        carry_ref[j] = tail
        w0, w1, w2 = cw[0:1], cw[1:2], cw[2:3]
    conv = w2 * g3 + w1 * _shift_rows(g3, prev, 1) + w0 * _shift_rows(g3, prev, 2)
    act = _silu(conv).reshape(tm, tf) * val
    acc_ref[...] += _dot(act.astype(BF16), wd_ref[...])

    @pl.when(j == nf - 1)
    def _():
        y = x_ref[...] + acc_ref[...]
        if final:
            y = _rms(y, fg_ref[...])
        o_ref[...] = y


def _conv_ffn(x, g, wg, wv, cw, wd, *, state=None, final_g=None, seq_len, tm, tf=256):
    m, d = x.shape
    f = wg.shape[1]
    per_seq = state is not None
    final = final_g is not None
    nf = f // tf
    groups = tm // SUBLANES if per_seq else 1
    in_specs = [pl.BlockSpec((tm, d), lambda i, j: (i, 0)),
                pl.BlockSpec((1, d), lambda i, j: (0, 0)),
                pl.BlockSpec((d, tf), lambda i, j: (0, j)),
                pl.BlockSpec((d, tf), lambda i, j: (0, j)),
                pl.BlockSpec((SUBLANES, tf), lambda i, j: (0, j)),
                pl.BlockSpec((tf, d), lambda i, j: (j, 0))]
    args = [x, g.reshape(1, d), wg, wv, cw, wd]
    if per_seq:
        in_specs.append(pl.BlockSpec((groups, SUBLANES, tf), lambda i, j: (i, 0, j)))
        args.append(state)
    if final:
        in_specs.append(pl.BlockSpec((1, d), lambda i, j: (0, 0)))
        args.append(final_g.reshape(1, d))
    scratch = [pltpu.VMEM((tm, d), BF16), pltpu.VMEM((tm, d), F32)]
    if not per_seq:
        scratch.append(pltpu.VMEM((nf, SUBLANES, tf), F32))
    return pl.pallas_call(
        functools.partial(_ffn_kernel, per_seq=per_seq, tiles_per_seq=max(seq_len // tm, 1), nf=nf, final=final),
        grid=(m // tm, nf),
        in_specs=in_specs,
        out_specs=[pl.BlockSpec((tm, d), lambda i, j: (i, 0)),
                   pl.BlockSpec((groups, SUBLANES, tf), lambda i, j: (i, 0, j))],
        out_shape=[jax.ShapeDtypeStruct((m, d), F32),
                   jax.ShapeDtypeStruct((m // tm * groups, SUBLANES, f), F32)],
        scratch_shapes=scratch,
        compiler_params=_cparams("arbitrary", "arbitrary"),
        name="conv_ffn",
    )(*args)


def _gdn_in_kernel(*refs, per_seq, tiles_per_seq, nq, nqkv, nz, qscale):
    x_ref, g_ref, w_ref, cw_ref = refs[:4]
    pos = 4
    st_ref = carry_ref = None
    if per_seq:
        st_ref = refs[pos]
        pos += 1
    o_ref, ba_ref, tail_ref, xn_ref = refs[pos:pos + 4]
    if not per_seq:
        carry_ref = refs[pos + 4]
    i = pl.program_id(0)
    j = pl.program_id(1)

    @pl.when(j == 0)
    def _():
        xn_ref[...] = _rms(x_ref[...], g_ref[...]).astype(BF16)

    pre = _dot(xn_ref[...], w_ref[...])
    tm, tn = pre.shape

    def conv_silu():
        cw = cw_ref[...]
        if per_seq:
            p3 = pre.reshape(tm // SUBLANES, SUBLANES, tn)
            prev = st_ref[...]
            tail_ref[...] = p3
            taps = [cw[t:t + 1][None] for t in range(4)]
        else:
            p3 = pre
            jc = jnp.minimum(j, nqkv - 1)

            @pl.when((i % tiles_per_seq) == 0)
            def _():
                carry_ref[jc] = jnp.zeros((SUBLANES, tn), F32)

            prev = carry_ref[jc]
            tail = pre[tm - SUBLANES:]
            tail_ref[0] = tail
            carry_ref[jc] = tail
            taps = [cw[t:t + 1] for t in range(4)]
        conv = taps[3] * p3
        for t in range(3):
            conv = conv + taps[t] * _shift_rows(p3, prev, 3 - t)
        return _silu(conv).reshape(tm, tn)

    def l2n(y, s):
        parts = []
        for a in range(tn // LANES):
            ya = y[:, a * LANES:(a + 1) * LANES]
            r = lax.rsqrt(jnp.sum(ya * ya, axis=-1, keepdims=True) + EPS)
            parts.append(ya * (r * s) if s != 1.0 else ya * r)
        return jnp.concatenate(parts, axis=1)

    @pl.when(j < nq)
    def _():
        o_ref[...] = l2n(conv_silu(), qscale)

    @pl.when((j >= nq) & (j < 2 * nq))
    def _():
        o_ref[...] = l2n(conv_silu(), 1.0)

    @pl.when((j >= 2 * nq) & (j < nqkv))
    def _():
        o_ref[...] = conv_silu()

    @pl.when((j >= nqkv) & (j < nqkv + nz))
    def _():
        o_ref[...] = pre

    @pl.when(j == nqkv + nz)
    def _():
        ba_ref[...] = pre[:, :2 * LANES]


def _gdn_in(x, g, w_all, cw, *, state=None, seq_len, a_qk, a_qkv, a_vd, qscale, tm, tn=512):
    m, d = x.shape
    per_seq = state is not None
    nq, nqkv, nz = a_qk // tn, a_qkv // tn, a_vd // tn
    nb = nqkv + nz + 1
    groups = tm // SUBLANES if per_seq else 1
    in_specs = [pl.BlockSpec((tm, d), lambda i, j: (i, 0)),
                pl.BlockSpec((1, d), lambda i, j: (0, 0)),
                pl.BlockSpec((d, tn), lambda i, j: (0, j)),
                pl.BlockSpec((SUBLANES, tn), lambda i, j: (0, jnp.minimum(j, nqkv - 1)))]
    args = [x, g.reshape(1, d), w_all, cw]
    if per_seq:
        in_specs.append(pl.BlockSpec((groups, SUBLANES, tn), lambda i, j: (i, 0, jnp.minimum(j, nqkv - 1))))
        args.append(state)
    scratch = [pltpu.VMEM((tm, d), BF16)]
    if not per_seq:
        scratch.append(pltpu.VMEM((nqkv, SUBLANES, tn), F32))
    return pl.pallas_call(
        functools.partial(_gdn_in_kernel, per_seq=per_seq, tiles_per_seq=max(seq_len // tm, 1),
                          nq=nq, nqkv=nqkv, nz=nz, qscale=qscale),
        grid=(m // tm, nb),
        in_specs=in_specs,
        out_specs=[pl.BlockSpec((tm, tn), lambda i, j: (i, jnp.minimum(j, nqkv + nz - 1))),
                   pl.BlockSpec((tm, 2 * LANES), lambda i, j: (i, 0)),
                   pl.BlockSpec((groups, SUBLANES, tn), lambda i, j: (i, 0, jnp.minimum(j, nqkv - 1)))],
        out_shape=[jax.ShapeDtypeStruct((m, a_qkv + a_vd), F32),
                   jax.ShapeDtypeStruct((m, 2 * LANES), F32),
                   jax.ShapeDtypeStruct((m // tm * groups, SUBLANES, a_qkv), F32)],
        scratch_shapes=scratch,
        compiler_params=_cparams("arbitrary", "arbitrary"),
        name="gdn_in",
    )(*args)


def _split(a):
    hi = a.astype(BF16)
    return hi, (a - hi.astype(F32)).astype(BF16)


def _dot3(a, b, nt=False):
    f = _dot_nt if nt else _dot
    return f(a[0], b[0]) + (f(a[0], b[1]) + f(a[1], b[0]))


def _unit_lower_inverses(lms, row, col, eye):
    def same(s):
        return (row >> s) == (col >> s)

    nd = [jnp.where(same(3), -lm, 0.0) for lm in lms]
    nds = [_split(a) for a in nd]
    nd2 = [_dot3(a, a) for a in nds]
    nd2s = [_split(a) for a in nd2]
    nd4 = [_dot3(a, a) for a in nd2s]
    x = [_dot3(_split(eye + a), _split(eye + b)) for a, b in zip(nd, nd2)]
    x = [_dot3(_split(a), _split(eye + b)) for a, b in zip(x, nd4)]
    for s in (3, 4, 5):
        mask = same(s + 1) & jnp.logical_not(same(s))
        offs = [_split(jnp.where(mask, lm, 0.0)) for lm in lms]
        xs = [_split(a) for a in x]
        y = [_dot3(a, b) for a, b in zip(xs, offs)]
        x = [a - _dot3(_split(b), c) for a, b, c in zip(x, y, xs)]
    return x


def _gdn_chunk_kernel(*refs, heads, dk, dv, has_s0):
    q_ref, k_ref, v_ref, z_ref, ba_ref, na_ref, dt_ref, gain_ref = refs[:8]
    pos = 8
    s0_ref = None
    if has_s0:
        s0_ref = refs[pos]
        pos += 1
    o_ref, sout_ref, s_ref = refs[pos:pos + 3]
    n = pl.program_id(1)

    @pl.when(n == 0)
    def _():
        if has_s0:
            s_ref[...] = s0_ref[0]
        else:
            s_ref[...] = jnp.zeros_like(s_ref)

    cr = q_ref.shape[0]

    def pad(a):
        if cr == CHUNK:
            return a
        return jnp.concatenate([a, jnp.zeros((CHUNK - cr, a.shape[1]), a.dtype)], axis=0)

    ba = ba_ref[...]
    beta = pad(jax.nn.sigmoid(ba[:, :LANES]))
    g = pad(na_ref[...] * jax.nn.softplus(ba[:, LANES:] + dt_ref[...]))
    q, k, v = pad(q_ref[...]), pad(k_ref[...]), pad(v_ref[...])
    z = z_ref[...]

    row = lax.broadcasted_iota(jnp.int32, (CHUNK, CHUNK), 0)
    col = lax.broadcasted_iota(jnp.int32, (CHUNK, CHUNK), 1)
    lower = row >= col
    strict = row > col
    eye = (row == col).astype(F32)
    gc = _dot_hi(lower.astype(F32), g)
    gct = gc.T
    glast = gc[CHUNK - 1:CHUNK]
    eg = jnp.exp(gc)
    egl = jnp.exp(glast - gc)
    eglast = jnp.exp(glast)
    gain = gain_ref[...]

    hs = range(heads)
    qh = [q[:, h * dk:(h + 1) * dk] for h in hs]
    kh = [k[:, h * dk:(h + 1) * dk] for h in hs]
    vh = [v[:, h * dv:(h + 1) * dv] for h in hs]
    bcol = [beta[:, h:h + 1] for h in hs]
    decay = [jnp.where(lower, jnp.exp(gc[:, h:h + 1] - gct[h:h + 1, :]), 0.0) for h in hs]
    kb = [kh[h] * bcol[h] for h in hs]
    kk = [_dot3(_split(kb[h]), _split(kh[h]), nt=True) for h in hs]
    lm = [jnp.where(strict, kk[h] * decay[h], 0.0) for h in hs]
    t = _unit_lower_inverses(lm, row, col, eye)
    rhs = [jnp.concatenate([vh[h] * bcol[h], kb[h] * eg[:, h:h + 1]], axis=1) for h in hs]
    sol = [_dot3(_split(t[h]), _split(rhs[h])) for h in hs]
    kh16 = [kh[h].astype(BF16) for h in hs]
    attn = [(_dot_nt(qh[h].astype(BF16), kh16[h]) * decay[h]).astype(BF16) for h in hs]
    qe16 = [(qh[h] * eg[:, h:h + 1]).astype(BF16) for h in hs]
    kd16 = [(kh[h] * egl[:, h:h + 1]).astype(BF16) for h in hs]
    sh = [s_ref[h] for h in hs]
    sh16 = [a.astype(BF16) for a in sh]
    v_new = [sol[h][:, :dv] - _dot(sol[h][:, dv:].astype(BF16), sh16[h]) for h in hs]
    vn16 = [a.astype(BF16) for a in v_new]
    o = [_dot(qe16[h], sh16[h]) + _dot(attn[h], vn16[h]) for h in hs]
    for h in hs:
        s_ref[h] = sh[h] * eglast[:, h:h + 1] + _dot_tn(kd16[h], vn16[h])
    for h in hs:
        oh = o[h][:cr]
        zh = z[:, h * dv:(h + 1) * dv]
        on = oh * lax.rsqrt(jnp.mean(oh * oh, axis=-1, keepdims=True) + EPS) * gain * _silu(zh)
        o_ref[:, h * dv:(h + 1) * dv] = on.astype(o_ref.dtype)

    @pl.when(n == pl.num_programs(1) - 1)
    def _():
        sout_ref[0] = s_ref[...]


def _gdn_chunk(qkvz, ba, neg_a, dt_bias, gain, *, s0, batch, seq_len, heads, dk, dv):
    m = qkvz.shape[0]
    cr = min(CHUNK, seq_len)
    nc = seq_len // cr
    hd = heads * dk
    has_s0 = s0 is not None

    def blk(c):
        return pl.BlockSpec((cr, hd), lambda b, n: (b * nc + n, c))

    in_specs = [blk(0), blk(1), blk(2), blk(3),
                pl.BlockSpec((cr, 2 * LANES), lambda b, n: (b * nc + n, 0)),
                pl.BlockSpec((1, LANES), lambda b, n: (0, 0)),
                pl.BlockSpec((1, LANES), lambda b, n: (0, 0)),
                pl.BlockSpec((1, dv), lambda b, n: (0, 0))]
    args = [qkvz, qkvz, qkvz, qkvz, ba, neg_a, dt_bias, gain.reshape(1, dv)]
    if has_s0:
        in_specs.append(pl.BlockSpec((1, heads, dk, dv), lambda b, n: (b, 0, 0, 0)))
        args.append(s0)
    return pl.pallas_call(
        functools.partial(_gdn_chunk_kernel, heads=heads, dk=dk, dv=dv, has_s0=has_s0),
        grid=(batch, nc),
        in_specs=in_specs,
        out_specs=[pl.BlockSpec((cr, hd), lambda b, n: (b * nc + n, 0)),
                   pl.BlockSpec((1, heads, dk, dv), lambda b, n: (b, 0, 0, 0))],
        out_shape=[jax.ShapeDtypeStruct((m, hd), BF16),
                   jax.ShapeDtypeStruct((batch, heads, dk, dv), F32)],
        scratch_shapes=[pltpu.VMEM((heads, dk, dv), F32)],
        compiler_params=_cparams("parallel", "arbitrary"),
        name="gdn_chunk",
    )(*args)


def _lambda(lv_ref, lam_init):
    lv = lv_ref[...]
    a = jnp.sum(lv[0:1] * lv[1:2], axis=-1, keepdims=True)
    b = jnp.sum(lv[2:3] * lv[3:4], axis=-1, keepdims=True)
    return jnp.exp(a) - jnp.exp(b) + lam_init


def _flash_kernel(qt_ref, kt_ref, q_ref, k_ref, v_ref, lv_ref, sub_ref, o_ref, qs_ref, m_ref, l_ref, acc_ref,
                  *, dh, lam_init):
    step = pl.program_id(2)
    qi = qt_ref[step]
    ki = kt_ref[step]
    tq = q_ref.shape[1]
    tk = k_ref.shape[0]

    @pl.when(ki == 0)
    def _():
        q = q_ref[...]
        feat = lax.broadcasted_iota(jnp.int32, q.shape, 0)
        zero = jnp.zeros_like(q)
        qs_ref[:, :tq] = jnp.where(feat < dh, q, zero)
        qs_ref[:, tq:] = jnp.where(feat >= dh, q, zero)
        m_ref[...] = jnp.full_like(m_ref, -jnp.inf)
        l_ref[...] = jnp.zeros_like(l_ref)
        acc_ref[...] = jnp.zeros_like(acc_ref)

    def accumulate(masked):
        s = _dot(k_ref[...], qs_ref[...])
        if masked:
            kpos = ki * tk + lax.broadcasted_iota(jnp.int32, s.shape, 0)
            qpos = qi * tq + (lax.broadcasted_iota(jnp.int32, s.shape, 1) & (tq - 1))
            s = jnp.where(kpos <= qpos, s, -jnp.inf)
        m_old = m_ref[...]
        m_new = jnp.maximum(m_old, jnp.max(s, axis=0, keepdims=True))
        alpha = jnp.exp(m_old - m_new)
        p = jnp.exp(s - m_new)
        l_ref[...] = alpha * l_ref[...] + jnp.sum(p, axis=0, keepdims=True)
        acc_ref[...] = alpha * acc_ref[...] + _dot(v_ref[...], p.astype(BF16))
        m_ref[...] = m_new

    below_diagonal = (ki + 1) * tk - 1 <= qi * tq

    @pl.when(below_diagonal)
    def _():
        accumulate(False)

    @pl.when(jnp.logical_not(below_diagonal))
    def _():
        accumulate(True)

    @pl.when((ki + 1) * tk == (qi + 1) * tq)
    def _():
        lam = _lambda(lv_ref, lam_init)
        a = acc_ref[...] / l_ref[...]
        o = a[:, :tq] - lam * a[:, tq:]
        r = lax.rsqrt(jnp.mean(o * o, axis=0, keepdims=True) + EPS)
        o = o * r * (sub_ref[...] * (1.0 - lam_init))
        o_ref[...] = o.T.astype(o_ref.dtype)


def _flash_prompt(q_t, k, v_t, lam_vecs, subln, *, batch, seq_len, dh, vd, lam_init, tq=1024, tk=512):
    m, width = k.shape
    pairs = width // (2 * dh)
    tq = min(tq, seq_len)
    tk = min(tk, tq)
    nq, nk = seq_len // tq, seq_len // tk
    assert tq & (tq - 1) == 0 and tq % tk == 0
    steps = [(i, j) for i in range(nq) for j in range((i + 1) * tq // tk)]
    q_tab = jnp.asarray([s[0] for s in steps], jnp.int32)
    k_tab = jnp.asarray([s[1] for s in steps], jnp.int32)
    grid_spec = pltpu.PrefetchScalarGridSpec(
        num_scalar_prefetch=2,
        grid=(batch, pairs, len(steps)),
        in_specs=[pl.BlockSpec((2 * dh, tq), lambda b, h, s, qt, kt: (h, b * nq + qt[s])),
                  pl.BlockSpec((tk, 2 * dh), lambda b, h, s, qt, kt: (b * nk + kt[s], h)),
                  pl.BlockSpec((vd, tk), lambda b, h, s, qt, kt: (h, b * nk + kt[s])),
                  pl.BlockSpec(lam_vecs.shape, lambda b, h, s, qt, kt: (0, 0)),
                  pl.BlockSpec((vd, 1), lambda b, h, s, qt, kt: (0, 0))],
        out_specs=pl.BlockSpec((tq, vd), lambda b, h, s, qt, kt: (b * nq + qt[s], h)),
        scratch_shapes=[pltpu.VMEM((2 * dh, 2 * tq), BF16),
                        pltpu.VMEM((1, 2 * tq), F32),
                        pltpu.VMEM((1, 2 * tq), F32),
                        pltpu.VMEM((vd, 2 * tq), F32)])
    return pl.pallas_call(
        functools.partial(_flash_kernel, dh=dh, lam_init=lam_init),
        grid_spec=grid_spec,
        out_shape=jax.ShapeDtypeStruct((m, pairs * vd), BF16),
        compiler_params=_cparams("parallel", "parallel", "arbitrary"),
        name="diff_flash",
    )(q_tab, k_tab, q_t, k, v_t, lam_vecs, subln.reshape(vd, 1))


def _paged_kernel(*refs, nh, dh, vd, lam_init, group):
    q_ref, kn_ref, vn_ref = refs[1:4]
    kc_refs = refs[4:4 + group]
    vc_refs = refs[4 + group:4 + 2 * group]
    lv_ref, sub_ref, o_ref, qbd_ref, m_ref, l_ref, acc_ref = refs[4 + 2 * group:]
    p = pl.program_id(1)
    t = q_ref.shape[1]
    width = q_ref.shape[2]
    rows = nh * t
    page = kc_refs[0].shape[2]
    nvh = width // vd

    def accum(kt16, v16, mask):
        s = _dot(qbd_ref[...], kt16)
        if mask is not None:
            s = jnp.where(mask, s, -jnp.inf)
        m_old = m_ref[...]
        m_new = jnp.maximum(m_old, jnp.max(s, axis=-1, keepdims=True))
        alpha = jnp.exp(m_old - m_new)
        pr = jnp.exp(s - m_new)
        l_ref[...] = alpha * l_ref[...] + jnp.sum(pr, axis=-1, keepdims=True)
        acc_ref[...] = alpha * acc_ref[...] + _dot(pr.astype(BF16), v16)
        m_ref[...] = m_new

    @pl.when(p == 0)
    def _():
        q = q_ref[0].astype(F32)
        q3 = jnp.broadcast_to(q[None], (nh, t, width))
        hd = lax.broadcasted_iota(jnp.int32, (nh, t, width), 0)
        ln = lax.broadcasted_iota(jnp.int32, (nh, t, width), 2)
        qbd = jnp.where((ln >= hd * dh) & (ln < (hd + 1) * dh), q3, 0.0)
        qbd_ref[...] = qbd.reshape(rows, width).astype(BF16)
        m_ref[...] = jnp.full_like(m_ref, -jnp.inf)
        l_ref[...] = jnp.zeros_like(l_ref)
        acc_ref[...] = jnp.zeros_like(acc_ref)
        zpad = jnp.zeros((page - t, width), F32)
        kt16 = jnp.concatenate([kn_ref[0], zpad], axis=0).T.astype(BF16)
        v16 = jnp.concatenate([vn_ref[0], zpad], axis=0).astype(BF16)
        r = lax.broadcasted_iota(jnp.int32, (rows, page), 0)
        c = lax.broadcasted_iota(jnp.int32, (rows, page), 1)
        accum(kt16, v16, c <= (r & (t - 1)))

    @pl.when(p > 0)
    def _():
        v = jnp.concatenate(
            [jnp.concatenate([vc[0, pl.ds(h, page, stride=nvh), :] for h in range(nvh)], axis=1).astype(BF16)
             for vc in vc_refs], axis=0)
        kt = jnp.concatenate([kc[0].astype(BF16) for kc in kc_refs], axis=1)
        accum(kt, v, None)

    @pl.when(p == pl.num_programs(1) - 1)
    def _():
        lam = _lambda(lv_ref, lam_init)
        r = lax.broadcasted_iota(jnp.int32, (rows, 1), 0)
        odd = ((r // t) & 1) == 1
        wgt = jnp.where(odd, -lam, 1.0) / l_ref[...]
        a3 = (acc_ref[...] * wgt).reshape(nh, t, width)
        hd = lax.broadcasted_iota(jnp.int32, (nh, t, width), 0)
        ln = lax.broadcasted_iota(jnp.int32, (nh, t, width), 2)
        pair = hd >> 1
        o = jnp.sum(jnp.where((ln >= pair * vd) & (ln < (pair + 1) * vd), a3, 0.0), axis=0)
        sub = sub_ref[...]
        for h in range(width // vd):
            oh = o[:, h * vd:(h + 1) * vd]
            o_ref[0, :, h * vd:(h + 1) * vd] = (_rms(oh, sub) * (1.0 - lam_init)).astype(o_ref.dtype)


def _paged_attention(q, k_new, v_new, cache_k, cache_v, page_table, lam_vecs, subln, *, dh, vd, lam_init, group=4):
    b, t, width = q.shape
    npg = page_table.shape[1]
    page = cache_k.shape[2]
    nh = width // dh
    nvh = width // vd
    assert t & (t - 1) == 0 and t <= page and nh == 2 * nvh and cache_v.shape[1] == page * nvh
    group = math.gcd(npg, group)

    def pidx(i):
        return lambda bb, p, pt: (pt[bb * npg + jnp.maximum(p - 1, 0) * group + i], 0, 0)

    grid_spec = pltpu.PrefetchScalarGridSpec(
        num_scalar_prefetch=1,
        grid=(b, npg // group + 1),
        in_specs=[pl.BlockSpec((1, t, width), lambda bb, p, pt: (bb, 0, 0)),
                  pl.BlockSpec((1, t, width), lambda bb, p, pt: (bb, 0, 0)),
                  pl.BlockSpec((1, t, width), lambda bb, p, pt: (bb, 0, 0))]
        + [pl.BlockSpec((1, width, page), pidx(i)) for i in range(group)]
        + [pl.BlockSpec((1, page * nvh, vd), pidx(i)) for i in range(group)]
        + [pl.BlockSpec(lam_vecs.shape, lambda bb, p, pt: (0, 0)),
           pl.BlockSpec((1, vd), lambda bb, p, pt: (0, 0))],
        out_specs=pl.BlockSpec((1, t, width), lambda bb, p, pt: (bb, 0, 0)),
        scratch_shapes=[pltpu.VMEM((nh * t, width), BF16),
                        pltpu.VMEM((nh * t, 1), F32),
                        pltpu.VMEM((nh * t, 1), F32),
                        pltpu.VMEM((nh * t, width), F32)])
    return pl.pallas_call(
        functools.partial(_paged_kernel, nh=nh, dh=dh, vd=vd, lam_init=lam_init, group=group),
        grid_spec=grid_spec,
        out_shape=jax.ShapeDtypeStruct((b, t, width), BF16),
        compiler_params=_cparams("parallel", "arbitrary"),
        name="diff_paged",
    )(page_table.reshape(-1), q, k_new, v_new, *([cache_k] * group), *([cache_v] * group),
      lam_vecs, subln.reshape(1, vd))


def _rot_weight(w, dh):
    k, n = w.shape
    w4 = w.reshape(k, n // dh, 2, dh // 2)
    return jnp.stack([-w4[:, :, 1], w4[:, :, 0]], axis=2).reshape(k, n)


def _rope_tables(pos, dh):
    half = dh // 2
    inv = 1.0 / (ROPE_THETA ** (jnp.arange(half, dtype=F32) / half))
    ang = pos.astype(F32)[:, None] * inv[None, :]
    reps = LANES // half
    return jnp.tile(jnp.cos(ang), (1, reps)), jnp.tile(jnp.sin(ang), (1, reps))


def _pad_rows(a, rows, front):
    pad = [(0, 0)] * a.ndim
    pad[-2] = (rows - a.shape[-2], 0) if front else (0, rows - a.shape[-2])
    return jnp.pad(a, pad)


def _prep_weights(p):
    n_a, d, a_in = p["a_w_in"].shape
    heads = p["a_A_log"].shape[1]
    dv = p["a_o_gain"].shape[1]
    a_vd = heads * dv
    a_qkv = a_in - a_vd - 2 * heads
    dh = p["b_lambda"].shape[-1]
    w = {}
    w_in = p["a_w_in"]
    tn = 512
    zpad = jnp.zeros((n_a, d, LANES - heads), F32)
    w["a_w_all"] = jnp.concatenate(
        [w_in[:, :, :a_qkv + a_vd], w_in[:, :, a_qkv + a_vd:a_qkv + a_vd + heads], zpad,
         w_in[:, :, a_qkv + a_vd + heads:], zpad, jnp.zeros((n_a, d, tn - 2 * LANES), F32)], axis=2).astype(BF16)
    w["a_cw"] = _pad_rows(p["a_conv_w"], SUBLANES, front=False)
    hp = jnp.zeros((n_a, LANES - heads), F32)
    w["a_neg_a"] = jnp.concatenate([-jnp.exp(p["a_A_log"].astype(F32)), hp], axis=1)[:, None, :]
    w["a_dt"] = jnp.concatenate([p["a_dt_bias"].astype(F32), hp], axis=1)[:, None, :]
    w["a_w_out"] = p["a_w_out"].astype(BF16)
    kq = p["w_kv"].shape[1] - (p["b_w_out"].shape[1])
    w["w_k"] = p["w_kv"][:, :kq].astype(BF16)
    w["w_k_rot"] = _rot_weight(p["w_kv"][:, :kq], dh).astype(BF16)
    w["w_v"] = p["w_kv"][:, kq:].astype(BF16)
    w["b_w_q"] = p["b_w_q"].astype(BF16)
    w["b_w_q_rot"] = jnp.stack([_rot_weight(p["b_w_q"][j], dh) for j in range(p["b_w_q"].shape[0])]).astype(BF16)
    w["b_w_out"] = p["b_w_out"].astype(BF16)
    w["w_v_t"] = w["w_v"].T
    w["b_w_q_t"] = jnp.swapaxes(w["b_w_q"], 1, 2)
    w["b_w_q_rot_t"] = jnp.swapaxes(w["b_w_q_rot"], 1, 2)
    f = p["f_w_down"].shape[1]
    w["f_wg"] = p["f_w_up"][:, :, :f].astype(BF16)
    w["f_wv"] = p["f_w_up"][:, :, f:].astype(BF16)
    w["f_cw"] = _pad_rows(p["f_conv_w"], SUBLANES, front=False)
    w["f_wd"] = p["f_w_down"].astype(BF16)
    return w


def _trunk(x, pos, p, w, *, delta0, dconv0, fconv0, cache_k, cache_v, page_table):
    b, l, d = x.shape
    m = b * l
    sample = page_table is not None
    depth = p["f_norm"].shape[0]
    n_a = p["a_norm"].shape[0]
    heads = p["a_A_log"].shape[1]
    dv = p["a_o_gain"].shape[1]
    dk = (p["a_w_in"].shape[2] - 2 * heads - 2 * heads * dv) // (2 * heads)
    a_vd = heads * dv
    a_qk = heads * dk
    a_qkv = 2 * a_qk + a_vd
    dh = p["b_lambda"].shape[-1]
    vd = p["b_subln"].shape[-1]
    f = p["f_w_down"].shape[1]
    tm = m if sample else min(512, l)
    assert m % tm == 0 and (sample or l % tm == 0)

    h = x.reshape(m, d)
    cos, sin = _rope_tables(pos, dh)
    if sample:
        cos, sin = jnp.tile(cos, (b, 1)), jnp.tile(sin, (b, 1))

    def tails_to_state(tails, rows):
        if sample:
            return tails[:, SUBLANES - rows:, :]
        per = l // tm
        return tails.reshape(b, per, SUBLANES, -1)[:, per - 1, SUBLANES - rows:, :]

    deltas, dconvs, fconvs = [], [], []
    k_new = v_new = k16 = v16_t = None
    for layer in range(depth):
        if layer < n_a:
            st = _pad_rows(dconv0[layer], SUBLANES, front=True) if sample else None
            qkvz, ba, tails = _gdn_in(h, p["a_norm"][layer], w["a_w_all"][layer], w["a_cw"][layer], state=st,
                                      seq_len=l, a_qk=a_qk, a_qkv=a_qkv, a_vd=a_vd, qscale=dk ** -0.5, tm=tm)
            dconvs.append(tails_to_state(tails, p["a_conv_w"].shape[1] - 1))
            o, s_fin = _gdn_chunk(qkvz, ba, w["a_neg_a"][layer], w["a_dt"][layer], p["a_o_gain"][layer],
                                  s0=delta0[layer] if sample else None, batch=b, seq_len=l,
                                  heads=heads, dk=dk, dv=dv)
            deltas.append(s_fin)
            h = _mm_res(o, w["a_w_out"][layer], h, tm=tm)
        else:
            if layer == n_a:
                k_new, k16 = _norm_proj(h, p["kv_norm"], w["w_k"], w_rot=w["w_k_rot"], cos=cos, sin=sin,
                                        out_dtypes=(F32, BF16), tm=tm)
                (v_new,) = _norm_proj(h, p["kv_norm"], w["w_v"], out_dtypes=(F32,), tm=tm)
                if not sample:
                    v16_t = _norm_proj_t(h, p["kv_norm"], w["w_v_t"], tm=tm)
            j = layer - n_a
            lam_init = 0.8 - 0.6 * math.exp(-0.3 * layer)
            if sample:
                (q16,) = _norm_proj(h, p["b_norm"][j], w["b_w_q"][j], w_rot=w["b_w_q_rot"][j], cos=cos, sin=sin,
                                    scale=dh ** -0.5, out_dtypes=(BF16,), tm=tm)
                width = q16.shape[1]
                o = _paged_attention(q16.reshape(b, l, width), k_new.reshape(b, l, width), v_new.reshape(b, l, width),
                                     cache_k, cache_v, page_table, p["b_lambda"][j], p["b_subln"][j],
                                     dh=dh, vd=vd, lam_init=lam_init).reshape(m, width)
            else:
                q16_t = _norm_proj_t(h, p["b_norm"][j], w["b_w_q_t"][j], wt_rot=w["b_w_q_rot_t"][j],
                                     cos_t=cos.T, sin_t=sin.T, scale=dh ** -0.5, tm=tm)
                o = _flash_prompt(q16_t, k16, v16_t, p["b_lambda"][j], p["b_subln"][j], batch=b, seq_len=l,
                                  dh=dh, vd=vd, lam_init=lam_init)
            h = _mm_res(o, w["b_w_out"][j], h, tm=tm)
        st = _pad_rows(fconv0[layer], SUBLANES, front=True) if sample else None
        h, tails = _conv_ffn(h, p["f_norm"][layer], w["f_wg"][layer], w["f_wv"][layer], w["f_cw"][layer],
                             w["f_wd"][layer], state=st, final_g=p["final_norm"] if layer == depth - 1 else None,
                             seq_len=l, tm=tm)
        fconvs.append(tails_to_state(tails, p["f_conv_w"].shape[1] - 1))
    nkh = k_new.shape[1] // dh
    return (h.reshape(b, l, d), jnp.stack(deltas), jnp.stack(dconvs), jnp.stack(fconvs),
            k_new.reshape(b, l, nkh, dh), v_new.reshape(b, l, v_new.shape[1] // vd, vd))


def kernel(x_prompt, x_sample, state_delta, state_dconv, state_fconv, cache_k, cache_v, page_table, a_norm, a_w_in, a_conv_w, a_A_log, a_dt_bias, a_o_gain, a_w_out, kv_norm, w_kv, b_norm, b_w_q, b_lambda, b_subln, b_w_out, f_norm, f_w_up, f_conv_w, f_w_down, final_norm):
    p = dict(a_norm=a_norm, a_w_in=a_w_in, a_conv_w=a_conv_w, a_A_log=a_A_log, a_dt_bias=a_dt_bias,
             a_o_gain=a_o_gain, a_w_out=a_w_out, kv_norm=kv_norm, w_kv=w_kv, b_norm=b_norm, b_w_q=b_w_q,
             b_lambda=b_lambda, b_subln=b_subln, b_w_out=b_w_out, f_norm=f_norm, f_w_up=f_w_up,
             f_conv_w=f_conv_w, f_w_down=f_w_down, final_norm=final_norm)
    w = _prep_weights(p)
    lp = x_prompt.shape[1]
    prompt = _trunk(x_prompt, jnp.arange(lp, dtype=jnp.int32), p, w, delta0=None, dconv0=None, fconv0=None,
                    cache_k=None, cache_v=None, page_table=None)
    ls = x_sample.shape[1]
    past_len = page_table.shape[1] * cache_k.shape[1]
    pool, page = cache_k.shape[:2]
    sample = _trunk(x_sample, past_len + jnp.arange(ls, dtype=jnp.int32), p, w, delta0=state_delta,
                    dconv0=state_dconv, fconv0=state_fconv,
                    cache_k=cache_k.transpose(0, 2, 3, 1).reshape(pool, -1, page),
                    cache_v=cache_v.reshape(pool, page * cache_v.shape[2], cache_v.shape[3]),
                    page_table=page_table)
    return (prompt[0], sample[0]) + prompt[1:] + sample[1:]
```

```python
import functools
import math

import jax
import jax.numpy as jnp
from jax import lax
from jax.experimental import pallas as pl
from jax.experimental.pallas import tpu as pltpu

F32 = jnp.float32
BF16 = jnp.bfloat16
EPS = 1e-6
ROPE_THETA = 10000.0
LANES = 128
SUBLANES = 8
CHUNK = 64
VMEM_LIMIT = 48 * 1024 * 1024
HI = lax.Precision.HIGHEST


def _cparams(*sem):
    return pltpu.CompilerParams(dimension_semantics=sem, vmem_limit_bytes=VMEM_LIMIT)


def _dot(a, b):
    return jnp.dot(a, b, preferred_element_type=F32)


def _dot_nt(a, b, precision=None):
    return lax.dot_general(a, b, (((1,), (1,)), ((), ())), precision=precision,
                           preferred_element_type=F32)


def _dot_tn(a, b, precision=None):
    return lax.dot_general(a, b, (((0,), (0,)), ((), ())), precision=precision,
                           preferred_element_type=F32)


def _dot_hi(a, b):
    return jnp.dot(a, b, precision=HI, preferred_element_type=F32)


def _rms(x, g):
    r = lax.rsqrt(jnp.mean(x * x, axis=-1, keepdims=True) + EPS)
    return x * r * g


def _silu(x):
    return x * jax.nn.sigmoid(x)


def _proj_kernel(*refs, rope, scale, n_out):
    if rope:
        x_ref, g_ref, w_ref, wr_ref, cos_ref, sin_ref = refs[:6]
        rest = refs[6:]
    else:
        x_ref, g_ref, w_ref = refs[:3]
        rest = refs[3:]
    outs, xn_ref = rest[:n_out], rest[n_out]

    @pl.when(pl.program_id(1) == 0)
    def _():
        xn_ref[...] = _rms(x_ref[...], g_ref[...]).astype(BF16)

    xn = xn_ref[...]
    y = _dot(xn, w_ref[...])
    if rope:
        yr = _dot(xn, wr_ref[...])
        reps = y.shape[1] // LANES
        cos = jnp.concatenate([cos_ref[...]] * reps, axis=1)
        sin = jnp.concatenate([sin_ref[...]] * reps, axis=1)
        y = y * cos + yr * sin
    if scale != 1.0:
        y = y * scale
    for o in outs:
        o[...] = y.astype(o.dtype)


def _norm_proj(x, g, w, *, w_rot=None, cos=None, sin=None, scale=1.0, out_dtypes=(F32,), tm, tn=512):
    m, d = x.shape
    n = w.shape[1]
    tn = min(tn, n)
    rope = w_rot is not None
    in_specs = [pl.BlockSpec((tm, d), lambda i, j: (i, 0)),
                pl.BlockSpec((1, d), lambda i, j: (0, 0)),
                pl.BlockSpec((d, tn), lambda i, j: (0, j))]
    args = [x, g.reshape(1, d), w]
    if rope:
        pt = cos.shape[0] // tm
        in_specs += [pl.BlockSpec((d, tn), lambda i, j: (0, j)),
                     pl.BlockSpec((tm, LANES), lambda i, j: (i % pt, 0)),
                     pl.BlockSpec((tm, LANES), lambda i, j: (i % pt, 0))]
        args += [w_rot, cos, sin]
    res = pl.pallas_call(
        functools.partial(_proj_kernel, rope=rope, scale=scale, n_out=len(out_dtypes)),
        grid=(m // tm, n // tn),
        in_specs=in_specs,
        out_specs=[pl.BlockSpec((tm, tn), lambda i, j: (i, j)) for _ in out_dtypes],
        out_shape=[jax.ShapeDtypeStruct((m, n), dt) for dt in out_dtypes],
        scratch_shapes=[pltpu.VMEM((tm, d), BF16)],
        compiler_params=_cparams("parallel", "arbitrary"),
        name="norm_proj_rope" if rope else "norm_proj",
    )(*args)
    return res


def _proj_t_kernel(*refs, rope, scale):
    if rope:
        x_ref, g_ref, wt_ref, wrt_ref, cos_ref, sin_ref, o_ref, xn_ref = refs
    else:
        x_ref, g_ref, wt_ref, o_ref, xn_ref = refs

    @pl.when(pl.program_id(1) == 0)
    def _():
        xn_ref[...] = _rms(x_ref[...], g_ref[...]).astype(BF16)

    xn = xn_ref[...]
    y = _dot_nt(wt_ref[...], xn)
    if rope:
        yr = _dot_nt(wrt_ref[...], xn)
        reps = y.shape[0] // LANES
        cos = jnp.concatenate([cos_ref[...]] * reps, axis=0)
        sin = jnp.concatenate([sin_ref[...]] * reps, axis=0)
        y = y * cos + yr * sin
    if scale != 1.0:
        y = y * scale
    o_ref[...] = y.astype(o_ref.dtype)


def _norm_proj_t(x, g, wt, *, wt_rot=None, cos_t=None, sin_t=None, scale=1.0, tm, tn=512):
    m, d = x.shape
    n = wt.shape[0]
    tn = min(tn, n)
    rope = wt_rot is not None
    in_specs = [pl.BlockSpec((tm, d), lambda i, j: (i, 0)),
                pl.BlockSpec((1, d), lambda i, j: (0, 0)),
                pl.BlockSpec((tn, d), lambda i, j: (j, 0))]
    args = [x, g.reshape(1, d), wt]
    if rope:
        pt = cos_t.shape[1] // tm
        in_specs += [pl.BlockSpec((tn, d), lambda i, j: (j, 0)),
                     pl.BlockSpec((LANES, tm), lambda i, j: (0, i % pt)),
                     pl.BlockSpec((LANES, tm), lambda i, j: (0, i % pt))]
        args += [wt_rot, cos_t, sin_t]
    return pl.pallas_call(
        functools.partial(_proj_t_kernel, rope=rope, scale=scale),
        grid=(m // tm, n // tn),
        in_specs=in_specs,
        out_specs=pl.BlockSpec((tn, tm), lambda i, j: (j, i)),
        out_shape=jax.ShapeDtypeStruct((n, m), BF16),
        scratch_shapes=[pltpu.VMEM((tm, d), BF16)],
        compiler_params=_cparams("parallel", "arbitrary"),
        name="norm_proj_t_rope" if rope else "norm_proj_t",
    )(*args)


def _mm_res_kernel(x_ref, w_ref, r_ref, o_ref):
    o_ref[...] = r_ref[...] + _dot(x_ref[...], w_ref[...])


def _mm_res(x, w, res, *, tm, tn=512):
    m, k = x.shape
    n = w.shape[1]
    tn = min(tn, n)
    return pl.pallas_call(
        _mm_res_kernel,
        grid=(m // tm, n // tn),
        in_specs=[pl.BlockSpec((tm, k), lambda i, j: (i, 0)),
                  pl.BlockSpec((k, tn), lambda i, j: (0, j)),
                  pl.BlockSpec((tm, tn), lambda i, j: (i, j))],
        out_specs=pl.BlockSpec((tm, tn), lambda i, j: (i, j)),
        out_shape=jax.ShapeDtypeStruct((m, n), F32),
        compiler_params=_cparams("parallel", "arbitrary"),
        name="mm_residual",
    )(x, w, res)


def _resident(shape):
    return pl.BlockSpec(shape, lambda i: (0,) * len(shape), pipeline_mode=pl.Buffered(1))


def _ffn_kernel(*refs, per_seq, tiles_per_seq, tf, final):
    x_ref, g_ref, wg_ref, wv_ref, cw_ref, wd_ref = refs[:6]
    pos = 6
    st_ref = fg_ref = carry_ref = None
    if per_seq:
        st_ref = refs[pos]
        pos += 1
    if final:
        fg_ref = refs[pos]
        pos += 1
    o_ref, tail_ref, act_ref, pad_ref = refs[pos:pos + 4]
    if not per_seq:
        carry_ref = refs[pos + 4]
    i = pl.program_id(0)
    tm = x_ref.shape[0]
    nf = wg_ref.shape[1] // tf
    x = x_ref[...]
    xn = _rms(x, g_ref[...]).astype(BF16)

    if not per_seq:
        @pl.when((i % tiles_per_seq) == 0)
        def _():
            carry_ref[...] = jnp.zeros_like(carry_ref)

    for j in range(nf):
        cols = slice(j * tf, (j + 1) * tf)
        gate = _dot(xn, wg_ref[:, cols])
        val = _dot(xn, wv_ref[:, cols])
        cw = cw_ref[:, cols]
        pad = pad_ref.at[j % 2]
        if per_seq:
            g3 = gate.reshape(tm // SUBLANES, SUBLANES, tf)
            pad[:, :SUBLANES, :] = st_ref[:, :, cols]
            pad[:, SUBLANES:, :] = g3
            tail_ref[:, :, cols] = g3
            conv = (cw[2:3][None] * g3 + cw[1:2][None] * pad[:, SUBLANES - 1:2 * SUBLANES - 1, :]
                    + cw[0:1][None] * pad[:, SUBLANES - 2:2 * SUBLANES - 2, :])
        else:
            tail = gate[tm - SUBLANES:]
            pad[:SUBLANES, :] = carry_ref[:, cols]
            pad[SUBLANES:, :] = gate
            tail_ref[0, :, cols] = tail
            carry_ref[:, cols] = tail
            conv = (cw[2:3] * gate + cw[1:2] * pad[pl.ds(SUBLANES - 1, tm), :]
                    + cw[0:1] * pad[pl.ds(SUBLANES - 2, tm), :])
        act_ref[:, cols] = (_silu(conv).reshape(tm, tf) * val).astype(BF16)

    y = x + _dot(act_ref[...], wd_ref[...])
    if final:
        y = _rms(y, fg_ref[...])
    o_ref[...] = y


def _conv_ffn(x, g, wg, wv, cw, wd, *, state=None, final_g=None, seq_len, tm, tf=256):
    m, d = x.shape
    f = wg.shape[1]
    per_seq = state is not None
    final = final_g is not None
    groups = tm // SUBLANES if per_seq else 1
    in_specs = [pl.BlockSpec((tm, d), lambda i: (i, 0)),
                _resident((1, d)), _resident((d, f)), _resident((d, f)), _resident((SUBLANES, f)), _resident((f, d))]
    args = [x, g.reshape(1, d), wg, wv, cw, wd]
    if per_seq:
        in_specs.append(pl.BlockSpec((groups, SUBLANES, f), lambda i: (i, 0, 0)))
        args.append(state)
    if final:
        in_specs.append(_resident((1, d)))
        args.append(final_g.reshape(1, d))
    if per_seq:
        scratch = [pltpu.VMEM((tm, f), BF16), pltpu.VMEM((2, groups, 2 * SUBLANES, tf), F32)]
    else:
        scratch = [pltpu.VMEM((tm, f), BF16), pltpu.VMEM((2, SUBLANES + tm, tf), F32), pltpu.VMEM((SUBLANES, f), F32)]
    return pl.pallas_call(
        functools.partial(_ffn_kernel, per_seq=per_seq, tiles_per_seq=max(seq_len // tm, 1), tf=tf, final=final),
        grid=(m // tm,),
        in_specs=in_specs,
        out_specs=[pl.BlockSpec((tm, d), lambda i: (i, 0)),
                   pl.BlockSpec((groups, SUBLANES, f), lambda i: (i, 0, 0))],
        out_shape=[jax.ShapeDtypeStruct((m, d), F32),
                   jax.ShapeDtypeStruct((m // tm * groups, SUBLANES, f), F32)],
        scratch_shapes=scratch,
        compiler_params=_cparams("arbitrary"),
        name="conv_ffn",
    )(*args)


def _gdn_in_kernel(*refs, per_seq, tiles_per_seq, tn, nq, nqkv, nz, qscale):
    x_ref, g_ref, w_ref, cw_ref = refs[:4]
    pos = 4
    st_ref = carry_ref = None
    if per_seq:
        st_ref = refs[pos]
        pos += 1
    o_ref, ba_ref, tail_ref, pad_ref = refs[pos:pos + 4]
    if not per_seq:
        carry_ref = refs[pos + 4]
    i = pl.program_id(0)
    tm = x_ref.shape[0]
    xn = _rms(x_ref[...], g_ref[...]).astype(BF16)

    if not per_seq:
        @pl.when((i % tiles_per_seq) == 0)
        def _():
            carry_ref[...] = jnp.zeros_like(carry_ref)

    def l2n(y, s):
        parts = []
        for a in range(tn // LANES):
            ya = y[:, a * LANES:(a + 1) * LANES]
            r = lax.rsqrt(jnp.sum(ya * ya, axis=-1, keepdims=True) + EPS)
            parts.append(ya * (r * s) if s != 1.0 else ya * r)
        return jnp.concatenate(parts, axis=1)

    for j in range(nqkv + nz + 1):
        cols = slice(j * tn, (j + 1) * tn)
        pre = _dot(xn, w_ref[:, cols])
        if j == nqkv + nz:
            ba_ref[...] = pre[:, :2 * LANES]
        elif j >= nqkv:
            o_ref[:, cols] = pre
        else:
            cw = cw_ref[:, cols]
            pad = pad_ref.at[j % 2]
            if per_seq:
                p3 = pre.reshape(tm // SUBLANES, SUBLANES, tn)
                pad[:, :SUBLANES, :] = st_ref[:, :, cols]
                pad[:, SUBLANES:, :] = p3
                tail_ref[:, :, cols] = p3
                conv = cw[3:4][None] * p3
                for t in range(3):
                    conv = conv + cw[t:t + 1][None] * pad[:, SUBLANES - 3 + t:2 * SUBLANES - 3 + t, :]
            else:
                tail = pre[tm - SUBLANES:]
                pad[:SUBLANES, :] = carry_ref[:, cols]
                pad[SUBLANES:, :] = pre
                tail_ref[0, :, cols] = tail
                carry_ref[:, cols] = tail
                conv = cw[3:4] * pre
                for t in range(3):
                    conv = conv + cw[t:t + 1] * pad[pl.ds(SUBLANES - 3 + t, tm), :]
            y = _silu(conv).reshape(tm, tn)
            if j < nq:
                y = l2n(y, qscale)
            elif j < 2 * nq:
                y = l2n(y, 1.0)
            o_ref[:, cols] = y


def _gdn_in(x, g, w_all, cw, *, state=None, seq_len, a_qk, a_qkv, a_vd, qscale, tm, tn=512):
    m, d = x.shape
    per_seq = state is not None
    nq, nqkv, nz = a_qk // tn, a_qkv // tn, a_vd // tn
    groups = tm // SUBLANES if per_seq else 1
    in_specs = [pl.BlockSpec((tm, d), lambda i: (i, 0)),
                _resident((1, d)), _resident(w_all.shape), _resident((SUBLANES, a_qkv))]
    args = [x, g.reshape(1, d), w_all, cw]
    if per_seq:
        in_specs.append(pl.BlockSpec((groups, SUBLANES, a_qkv), lambda i: (i, 0, 0)))
        args.append(state)
    if per_seq:
        scratch = [pltpu.VMEM((2, groups, 2 * SUBLANES, tn), F32)]
    else:
        scratch = [pltpu.VMEM((2, SUBLANES + tm, tn), F32), pltpu.VMEM((SUBLANES, a_qkv), F32)]
    return pl.pallas_call(
        functools.partial(_gdn_in_kernel, per_seq=per_seq, tiles_per_seq=max(seq_len // tm, 1),
                          tn=tn, nq=nq, nqkv=nqkv, nz=nz, qscale=qscale),
        grid=(m // tm,),
        in_specs=in_specs,
        out_specs=[pl.BlockSpec((tm, a_qkv + a_vd), lambda i: (i, 0)),
                   pl.BlockSpec((tm, 2 * LANES), lambda i: (i, 0)),
                   pl.BlockSpec((groups, SUBLANES, a_qkv), lambda i: (i, 0, 0))],
        out_shape=[jax.ShapeDtypeStruct((m, a_qkv + a_vd), F32),
                   jax.ShapeDtypeStruct((m, 2 * LANES), F32),
                   jax.ShapeDtypeStruct((m // tm * groups, SUBLANES, a_qkv), F32)],
        scratch_shapes=scratch,
        compiler_params=_cparams("arbitrary"),
        name="gdn_in",
    )(*args)


def _split(a):
    hi = a.astype(BF16)
    return hi, (a - hi.astype(F32)).astype(BF16)


def _dot3(a, b, nt=False):
    f = _dot_nt if nt else _dot
    return f(a[0], b[0]) + (f(a[0], b[1]) + f(a[1], b[0]))


def _unit_lower_inverses(lms, row, col, eye):
    def same(s):
        return (row >> s) == (col >> s)

    nd = [jnp.where(same(3), -lm, 0.0) for lm in lms]
    nds = [_split(a) for a in nd]
    nd2 = [_dot3(a, a) for a in nds]
    nd2s = [_split(a) for a in nd2]
    nd4 = [_dot3(a, a) for a in nd2s]
    x = [_dot3(_split(eye + a), _split(eye + b)) for a, b in zip(nd, nd2)]
    x = [_dot3(_split(a), _split(eye + b)) for a, b in zip(x, nd4)]
    for s in (3, 4, 5):
        mask = same(s + 1) & jnp.logical_not(same(s))
        offs = [_split(jnp.where(mask, lm, 0.0)) for lm in lms]
        xs = [_split(a) for a in x]
        y = [_dot3(a, b) for a, b in zip(xs, offs)]
        x = [a - _dot3(_split(b), c) for a, b, c in zip(x, y, xs)]
    return x


def _gdn_chunk_kernel(*refs, heads, dk, dv, has_s0):
    q_ref, k_ref, v_ref, z_ref, ba_ref, na_ref, dt_ref, gain_ref = refs[:8]
    pos = 8
    s0_ref = None
    if has_s0:
        s0_ref = refs[pos]
        pos += 1
    o_ref, sout_ref, s_ref = refs[pos:pos + 3]
    n = pl.program_id(1)

    @pl.when(n == 0)
    def _():
        if has_s0:
            s_ref[...] = s0_ref[0]
        else:
            s_ref[...] = jnp.zeros_like(s_ref)

    cr = q_ref.shape[0]

    def pad(a):
        if cr == CHUNK:
            return a
        return jnp.concatenate([a, jnp.zeros((CHUNK - cr, a.shape[1]), a.dtype)], axis=0)

    ba = ba_ref[...]
    beta = pad(jax.nn.sigmoid(ba[:, :LANES]))
    g = pad(na_ref[...] * jax.nn.softplus(ba[:, LANES:] + dt_ref[...]))
    q, k, v = pad(q_ref[...]), pad(k_ref[...]), pad(v_ref[...])
    z = z_ref[...]

    row = lax.broadcasted_iota(jnp.int32, (CHUNK, CHUNK), 0)
    col = lax.broadcasted_iota(jnp.int32, (CHUNK, CHUNK), 1)
    lower = row >= col
    strict = row > col
    eye = (row == col).astype(F32)
    gc = _dot_hi(lower.astype(F32), g)
    gct = gc.T
    glast = gc[CHUNK - 1:CHUNK]
    eg = jnp.exp(gc)
    egl = jnp.exp(glast - gc)
    eglast = jnp.exp(glast)
    gain = gain_ref[...]

    hs = range(heads)
    qh = [q[:, h * dk:(h + 1) * dk] for h in hs]
    kh = [k[:, h * dk:(h + 1) * dk] for h in hs]
    vh = [v[:, h * dv:(h + 1) * dv] for h in hs]
    bcol = [beta[:, h:h + 1] for h in hs]
    decay = [jnp.where(lower, jnp.exp(gc[:, h:h + 1] - gct[h:h + 1, :]), 0.0) for h in hs]
    kb = [kh[h] * bcol[h] for h in hs]
    kk = [_dot3(_split(kb[h]), _split(kh[h]), nt=True) for h in hs]
    lm = [jnp.where(strict, kk[h] * decay[h], 0.0) for h in hs]
    t = _unit_lower_inverses(lm, row, col, eye)
    rhs = [jnp.concatenate([vh[h] * bcol[h], kb[h] * eg[:, h:h + 1]], axis=1) for h in hs]
    sol = [_dot3(_split(t[h]), _split(rhs[h])) for h in hs]
    kh16 = [kh[h].astype(BF16) for h in hs]
    attn = [(_dot_nt(qh[h].astype(BF16), kh16[h]) * decay[h]).astype(BF16) for h in hs]
    qe16 = [(qh[h] * eg[:, h:h + 1]).astype(BF16) for h in hs]
    kd16 = [(kh[h] * egl[:, h:h + 1]).astype(BF16) for h in hs]
    sh = [s_ref[h] for h in hs]
    sh16 = [a.astype(BF16) for a in sh]
    v_new = [sol[h][:, :dv] - _dot(sol[h][:, dv:].astype(BF16), sh16[h]) for h in hs]
    vn16 = [a.astype(BF16) for a in v_new]
    o = [_dot(qe16[h], sh16[h]) + _dot(attn[h], vn16[h]) for h in hs]
    for h in hs:
        s_ref[h] = sh[h] * eglast[:, h:h + 1] + _dot_tn(kd16[h], vn16[h])
    for h in hs:
        oh = o[h][:cr]
        zh = z[:, h * dv:(h + 1) * dv]
        on = oh * lax.rsqrt(jnp.mean(oh * oh, axis=-1, keepdims=True) + EPS) * gain * _silu(zh)
        o_ref[:, h * dv:(h + 1) * dv] = on.astype(o_ref.dtype)

    @pl.when(n == pl.num_programs(1) - 1)
    def _():
        sout_ref[0] = s_ref[...]


def _gdn_chunk(qkvz, ba, neg_a, dt_bias, gain, *, s0, batch, seq_len, heads, dk, dv):
    m = qkvz.shape[0]
    cr = min(CHUNK, seq_len)
    nc = seq_len // cr
    hd = heads * dk
    has_s0 = s0 is not None

    def blk(c):
        return pl.BlockSpec((cr, hd), lambda b, n: (b * nc + n, c))

    in_specs = [blk(0), blk(1), blk(2), blk(3),
                pl.BlockSpec((cr, 2 * LANES), lambda b, n: (b * nc + n, 0)),
                pl.BlockSpec((1, LANES), lambda b, n: (0, 0)),
                pl.BlockSpec((1, LANES), lambda b, n: (0, 0)),
                pl.BlockSpec((1, dv), lambda b, n: (0, 0))]
    args = [qkvz, qkvz, qkvz, qkvz, ba, neg_a, dt_bias, gain.reshape(1, dv)]
    if has_s0:
        in_specs.append(pl.BlockSpec((1, heads, dk, dv), lambda b, n: (b, 0, 0, 0)))
        args.append(s0)
    return pl.pallas_call(
        functools.partial(_gdn_chunk_kernel, heads=heads, dk=dk, dv=dv, has_s0=has_s0),
        grid=(batch, nc),
        in_specs=in_specs,
        out_specs=[pl.BlockSpec((cr, hd), lambda b, n: (b * nc + n, 0)),
                   pl.BlockSpec((1, heads, dk, dv), lambda b, n: (b, 0, 0, 0))],
        out_shape=[jax.ShapeDtypeStruct((m, hd), BF16),
                   jax.ShapeDtypeStruct((batch, heads, dk, dv), F32)],
        scratch_shapes=[pltpu.VMEM((heads, dk, dv), F32)],
        compiler_params=_cparams("parallel", "arbitrary"),
        name="gdn_chunk",
    )(*args)


def _lambda(lv_ref, lam_init):
    lv = lv_ref[...]
    a = jnp.sum(lv[0:1] * lv[1:2], axis=-1, keepdims=True)
    b = jnp.sum(lv[2:3] * lv[3:4], axis=-1, keepdims=True)
    return jnp.exp(a) - jnp.exp(b) + lam_init


def _flash_kernel(qt_ref, kt_ref, q_ref, k_ref, v_ref, lv_ref, sub_ref, o_ref, qs_ref, m_ref, acc_ref,
                  *, dh, lam_init):
    step = pl.program_id(2)
    qi = qt_ref[step]
    ki = kt_ref[step]
    tq = q_ref.shape[1]
    tk = k_ref.shape[0]
    vd = v_ref.shape[0]

    @pl.when(ki == 0)
    def _():
        q = q_ref[...]
        feat = lax.broadcasted_iota(jnp.int32, q.shape, 0)
        zero = jnp.zeros_like(q)
        qs_ref[:, :tq] = jnp.where(feat < dh, q, zero)
        qs_ref[:, tq:] = jnp.where(feat >= dh, q, zero)
        m_ref[...] = jnp.full_like(m_ref, -jnp.inf)
        acc_ref[...] = jnp.zeros_like(acc_ref)

    def accumulate(masked):
        s = _dot(k_ref[...], qs_ref[...])
        if masked:
            kpos = ki * tk + lax.broadcasted_iota(jnp.int32, s.shape, 0)
            qpos = qi * tq + (lax.broadcasted_iota(jnp.int32, s.shape, 1) & (tq - 1))
            s = jnp.where(kpos <= qpos, s, -jnp.inf)
        m_old = m_ref[...]
        m_new = jnp.maximum(m_old, jnp.max(s, axis=0, keepdims=True))
        alpha = jnp.exp2(m_old - m_new)
        p = jnp.exp2(s - m_new).astype(BF16)
        v_ones = jnp.concatenate([v_ref[...], jnp.ones((acc_ref.shape[0] - vd, tk), BF16)], axis=0)
        acc_ref[...] = alpha * acc_ref[...] + _dot(v_ones, p)
        m_ref[...] = m_new

    below_diagonal = (ki + 1) * tk - 1 <= qi * tq

    @pl.when(below_diagonal)
    def _():
        accumulate(False)

    @pl.when(jnp.logical_not(below_diagonal))
    def _():
        accumulate(True)

    @pl.when((ki + 1) * tk == (qi + 1) * tq)
    def _():
        lam = _lambda(lv_ref, lam_init)
        a = acc_ref[:vd] / acc_ref[vd:vd + 1]
        o = a[:, :tq] - lam * a[:, tq:]
        r = lax.rsqrt(jnp.mean(o * o, axis=0, keepdims=True) + EPS)
        o = o * r * (sub_ref[...] * (1.0 - lam_init))
        o_ref[...] = o.T.astype(o_ref.dtype)


def _flash_prompt(q_t, k, v_t, lam_vecs, subln, *, batch, seq_len, dh, vd, lam_init, tq=1024, tk=512):
    m, width = k.shape
    pairs = width // (2 * dh)
    tq = min(tq, seq_len)
    tk = min(tk, tq)
    nq, nk = seq_len // tq, seq_len // tk
    assert tq & (tq - 1) == 0 and tq % tk == 0
    steps = [(i, j) for i in range(nq) for j in range((i + 1) * tq // tk)]
    q_tab = jnp.asarray([s[0] for s in steps], jnp.int32)
    k_tab = jnp.asarray([s[1] for s in steps], jnp.int32)
    grid_spec = pltpu.PrefetchScalarGridSpec(
        num_scalar_prefetch=2,
        grid=(batch, pairs, len(steps)),
        in_specs=[pl.BlockSpec((2 * dh, tq), lambda b, h, s, qt, kt: (h, b * nq + qt[s])),
                  pl.BlockSpec((tk, 2 * dh), lambda b, h, s, qt, kt: (b * nk + kt[s], h)),
                  pl.BlockSpec((vd, tk), lambda b, h, s, qt, kt: (h, b * nk + kt[s])),
                  pl.BlockSpec(lam_vecs.shape, lambda b, h, s, qt, kt: (0, 0)),
                  pl.BlockSpec((vd, 1), lambda b, h, s, qt, kt: (0, 0))],
        out_specs=pl.BlockSpec((tq, vd), lambda b, h, s, qt, kt: (b * nq + qt[s], h)),
        scratch_shapes=[pltpu.VMEM((2 * dh, 2 * tq), BF16),
                        pltpu.VMEM((1, 2 * tq), F32),
                        pltpu.VMEM((vd + 2 * SUBLANES, 2 * tq), F32)])
    return pl.pallas_call(
        functools.partial(_flash_kernel, dh=dh, lam_init=lam_init),
        grid_spec=grid_spec,
        out_shape=jax.ShapeDtypeStruct((m, pairs * vd), BF16),
        compiler_params=_cparams("parallel", "parallel", "arbitrary"),
        name="diff_flash",
    )(q_tab, k_tab, q_t, k, v_t, lam_vecs, subln.reshape(vd, 1))


def _paged_kernel(*refs, nh, dh, vd, lam_init, group):
    q_ref, kn_ref, vn_ref = refs[1:4]
    kc_refs = refs[4:4 + group]
    vc_refs = refs[4 + group:4 + 2 * group]
    lv_ref, sub_ref, o_ref, qbd_ref, m_ref, l_ref, acc_ref = refs[4 + 2 * group:]
    p = pl.program_id(1)
    t = q_ref.shape[1]
    width = q_ref.shape[2]
    rows = nh * t
    page = kc_refs[0].shape[2]
    nvh = width // vd

    def accum(kt16, v16, mask):
        s = _dot(qbd_ref[...], kt16)
        if mask is not None:
            s = jnp.where(mask, s, -jnp.inf)
        m_old = m_ref[...]
        m_new = jnp.maximum(m_old, jnp.max(s, axis=-1, keepdims=True))
        alpha = jnp.exp(m_old - m_new)
        pr = jnp.exp(s - m_new)
        l_ref[...] = alpha * l_ref[...] + jnp.sum(pr, axis=-1, keepdims=True)
        acc_ref[...] = alpha * acc_ref[...] + _dot(pr.astype(BF16), v16)
        m_ref[...] = m_new

    @pl.when(p == 0)
    def _():
        q = q_ref[0].astype(F32)
        q3 = jnp.broadcast_to(q[None], (nh, t, width))
        hd = lax.broadcasted_iota(jnp.int32, (nh, t, width), 0)
        ln = lax.broadcasted_iota(jnp.int32, (nh, t, width), 2)
        qbd = jnp.where((ln >= hd * dh) & (ln < (hd + 1) * dh), q3, 0.0)
        qbd_ref[...] = qbd.reshape(rows, width).astype(BF16)
        m_ref[...] = jnp.full_like(m_ref, -jnp.inf)
        l_ref[...] = jnp.zeros_like(l_ref)
        acc_ref[...] = jnp.zeros_like(acc_ref)
        zpad = jnp.zeros((page - t, width), F32)
        kt16 = jnp.concatenate([kn_ref[0], zpad], axis=0).T.astype(BF16)
        v16 = jnp.concatenate([vn_ref[0], zpad], axis=0).astype(BF16)
        r = lax.broadcasted_iota(jnp.int32, (rows, page), 0)
        c = lax.broadcasted_iota(jnp.int32, (rows, page), 1)
        accum(kt16, v16, c <= (r & (t - 1)))

    @pl.when(p > 0)
    def _():
        v = jnp.concatenate(
            [jnp.concatenate([vc[0, pl.ds(h, page, stride=nvh), :] for h in range(nvh)], axis=1).astype(BF16)
             for vc in vc_refs], axis=0)
        kt = jnp.concatenate([kc[0].astype(BF16) for kc in kc_refs], axis=1)
        accum(kt, v, None)

    @pl.when(p == pl.num_programs(1) - 1)
    def _():
        lam = _lambda(lv_ref, lam_init)
        r = lax.broadcasted_iota(jnp.int32, (rows, 1), 0)
        odd = ((r // t) & 1) == 1
        wgt = jnp.where(odd, -lam, 1.0) / l_ref[...]
        a3 = (acc_ref[...] * wgt).reshape(nh, t, width)
        hd = lax.broadcasted_iota(jnp.int32, (nh, t, width), 0)
        ln = lax.broadcasted_iota(jnp.int32, (nh, t, width), 2)
        pair = hd >> 1
        o = jnp.sum(jnp.where((ln >= pair * vd) & (ln < (pair + 1) * vd), a3, 0.0), axis=0)
        sub = sub_ref[...]
        for h in range(width // vd):
            oh = o[:, h * vd:(h + 1) * vd]
            o_ref[0, :, h * vd:(h + 1) * vd] = (_rms(oh, sub) * (1.0 - lam_init)).astype(o_ref.dtype)


def _paged_attention(q, k_new, v_new, cache_k, cache_v, page_table, lam_vecs, subln, *, dh, vd, lam_init, group=8):
    b, t, width = q.shape
    npg = page_table.shape[1]
    page = cache_k.shape[2]
    nh = width // dh
    nvh = width // vd
    assert t & (t - 1) == 0 and t <= page and nh == 2 * nvh and cache_v.shape[1] == page * nvh
    group = math.gcd(npg, group)

    def pidx(i):
        return lambda bb, p, pt: (pt[bb * npg + jnp.maximum(p - 1, 0) * group + i], 0, 0)

    grid_spec = pltpu.PrefetchScalarGridSpec(
        num_scalar_prefetch=1,
        grid=(b, npg // group + 1),
        in_specs=[pl.BlockSpec((1, t, width), lambda bb, p, pt: (bb, 0, 0)),
                  pl.BlockSpec((1, t, width), lambda bb, p, pt: (bb, 0, 0)),
                  pl.BlockSpec((1, t, width), lambda bb, p, pt: (bb, 0, 0))]
        + [pl.BlockSpec((1, width, page), pidx(i)) for i in range(group)]
        + [pl.BlockSpec((1, page * nvh, vd), pidx(i)) for i in range(group)]
        + [pl.BlockSpec(lam_vecs.shape, lambda bb, p, pt: (0, 0)),
           pl.BlockSpec((1, vd), lambda bb, p, pt: (0, 0))],
        out_specs=pl.BlockSpec((1, t, width), lambda bb, p, pt: (bb, 0, 0)),
        scratch_shapes=[pltpu.VMEM((nh * t, width), BF16),
                        pltpu.VMEM((nh * t, 1), F32),
                        pltpu.VMEM((nh * t, 1), F32),
                        pltpu.VMEM((nh * t, width), F32)])
    return pl.pallas_call(
        functools.partial(_paged_kernel, nh=nh, dh=dh, vd=vd, lam_init=lam_init, group=group),
        grid_spec=grid_spec,
        out_shape=jax.ShapeDtypeStruct((b, t, width), BF16),
        compiler_params=_cparams("parallel", "arbitrary"),
        name="diff_paged",
    )(page_table.reshape(-1), q, k_new, v_new, *([cache_k] * group), *([cache_v] * group),
      lam_vecs, subln.reshape(1, vd))


def _rot_weight(w, dh):
    k, n = w.shape
    w4 = w.reshape(k, n // dh, 2, dh // 2)
    return jnp.stack([-w4[:, :, 1], w4[:, :, 0]], axis=2).reshape(k, n)


def _rope_tables(pos, dh):
    half = dh // 2
    inv = 1.0 / (ROPE_THETA ** (jnp.arange(half, dtype=F32) / half))
    ang = pos.astype(F32)[:, None] * inv[None, :]
    reps = LANES // half
    return jnp.tile(jnp.cos(ang), (1, reps)), jnp.tile(jnp.sin(ang), (1, reps))


def _pad_rows(a, rows, front):
    pad = [(0, 0)] * a.ndim
    pad[-2] = (rows - a.shape[-2], 0) if front else (0, rows - a.shape[-2])
    return jnp.pad(a, pad)


def _prep_weights(p):
    n_a, d, a_in = p["a_w_in"].shape
    heads = p["a_A_log"].shape[1]
    dv = p["a_o_gain"].shape[1]
    a_vd = heads * dv
    a_qkv = a_in - a_vd - 2 * heads
    dh = p["b_lambda"].shape[-1]
    w = {}
    w_in = p["a_w_in"]
    tn = 512
    zpad = jnp.zeros((n_a, d, LANES - heads), F32)
    w["a_w_all"] = jnp.concatenate(
        [w_in[:, :, :a_qkv + a_vd], w_in[:, :, a_qkv + a_vd:a_qkv + a_vd + heads], zpad,
         w_in[:, :, a_qkv + a_vd + heads:], zpad, jnp.zeros((n_a, d, tn - 2 * LANES), F32)], axis=2).astype(BF16)
    w["a_cw"] = _pad_rows(p["a_conv_w"], SUBLANES, front=False)
    hp = jnp.zeros((n_a, LANES - heads), F32)
    w["a_neg_a"] = jnp.concatenate([-jnp.exp(p["a_A_log"].astype(F32)), hp], axis=1)[:, None, :]
    w["a_dt"] = jnp.concatenate([p["a_dt_bias"].astype(F32), hp], axis=1)[:, None, :]
    w["a_w_out"] = p["a_w_out"].astype(BF16)
    kq = p["w_kv"].shape[1] - (p["b_w_out"].shape[1])
    w["w_k"] = p["w_kv"][:, :kq].astype(BF16)
    w["w_k_rot"] = _rot_weight(p["w_kv"][:, :kq], dh).astype(BF16)
    w["w_v"] = p["w_kv"][:, kq:].astype(BF16)
    w["b_w_q"] = p["b_w_q"].astype(BF16)
    w["b_w_q_rot"] = jnp.stack([_rot_weight(p["b_w_q"][j], dh) for j in range(p["b_w_q"].shape[0])]).astype(BF16)
    w["b_w_out"] = p["b_w_out"].astype(BF16)
    w["w_v_t"] = w["w_v"].T
    w["b_w_q_t"] = jnp.swapaxes(w["b_w_q"], 1, 2)
    w["b_w_q_rot_t"] = jnp.swapaxes(w["b_w_q_rot"], 1, 2)
    f = p["f_w_down"].shape[1]
    w["f_wg"] = p["f_w_up"][:, :, :f].astype(BF16)
    w["f_wv"] = p["f_w_up"][:, :, f:].astype(BF16)
    w["f_cw"] = _pad_rows(p["f_conv_w"], SUBLANES, front=False)
    w["f_wd"] = p["f_w_down"].astype(BF16)
    return w


def _trunk(x, pos, p, w, *, delta0, dconv0, fconv0, cache_k, cache_v, page_table):
    b, l, d = x.shape
    m = b * l
    sample = page_table is not None
    depth = p["f_norm"].shape[0]
    n_a = p["a_norm"].shape[0]
    heads = p["a_A_log"].shape[1]
    dv = p["a_o_gain"].shape[1]
    dk = (p["a_w_in"].shape[2] - 2 * heads - 2 * heads * dv) // (2 * heads)
    a_vd = heads * dv
    a_qk = heads * dk
    a_qkv = 2 * a_qk + a_vd
    dh = p["b_lambda"].shape[-1]
    vd = p["b_subln"].shape[-1]
    f = p["f_w_down"].shape[1]
    tm = m if sample else min(512, l)
    assert m % tm == 0 and (sample or l % tm == 0)

    h = x.reshape(m, d)
    cos, sin = _rope_tables(pos, dh)
    if sample:
        cos, sin = jnp.tile(cos, (b, 1)), jnp.tile(sin, (b, 1))

    def tails_to_state(tails, rows):
        if sample:
            return tails[:, SUBLANES - rows:, :]
        per = l // tm
        return tails.reshape(b, per, SUBLANES, -1)[:, per - 1, SUBLANES - rows:, :]

    deltas, dconvs, fconvs = [], [], []
    k_new = v_new = k16 = v16_t = None
    for layer in range(depth):
        if layer < n_a:
            st = _pad_rows(dconv0[layer], SUBLANES, front=True) if sample else None
            qkvz, ba, tails = _gdn_in(h, p["a_norm"][layer], w["a_w_all"][layer], w["a_cw"][layer], state=st,
                                      seq_len=l, a_qk=a_qk, a_qkv=a_qkv, a_vd=a_vd, qscale=dk ** -0.5, tm=tm)
            dconvs.append(tails_to_state(tails, p["a_conv_w"].shape[1] - 1))
            o, s_fin = _gdn_chunk(qkvz, ba, w["a_neg_a"][layer], w["a_dt"][layer], p["a_o_gain"][layer],
                                  s0=delta0[layer] if sample else None, batch=b, seq_len=l,
                                  heads=heads, dk=dk, dv=dv)
            deltas.append(s_fin)
            h = _mm_res(o, w["a_w_out"][layer], h, tm=tm)
        else:
            if layer == n_a:
                k_new, k16 = _norm_proj(h, p["kv_norm"], w["w_k"], w_rot=w["w_k_rot"], cos=cos, sin=sin,
                                        out_dtypes=(F32, BF16), tm=tm)
                (v_new,) = _norm_proj(h, p["kv_norm"], w["w_v"], out_dtypes=(F32,), tm=tm)
                if not sample:
                    v16_t = _norm_proj_t(h, p["kv_norm"], w["w_v_t"], tm=tm)
            j = layer - n_a
            lam_init = 0.8 - 0.6 * math.exp(-0.3 * layer)
            if sample:
                (q16,) = _norm_proj(h, p["b_norm"][j], w["b_w_q"][j], w_rot=w["b_w_q_rot"][j], cos=cos, sin=sin,
                                    scale=dh ** -0.5, out_dtypes=(BF16,), tm=tm)
                width = q16.shape[1]
                o = _paged_attention(q16.reshape(b, l, width), k_new.reshape(b, l, width), v_new.reshape(b, l, width),
                                     cache_k, cache_v, page_table, p["b_lambda"][j], p["b_subln"][j],
                                     dh=dh, vd=vd, lam_init=lam_init).reshape(m, width)
            else:
                q16_t = _norm_proj_t(h, p["b_norm"][j], w["b_w_q_t"][j], wt_rot=w["b_w_q_rot_t"][j],
                                     cos_t=cos.T, sin_t=sin.T, scale=dh ** -0.5 * math.log2(math.e), tm=tm)
                o = _flash_prompt(q16_t, k16, v16_t, p["b_lambda"][j], p["b_subln"][j], batch=b, seq_len=l,
                                  dh=dh, vd=vd, lam_init=lam_init)
            h = _mm_res(o, w["b_w_out"][j], h, tm=tm)
        st = _pad_rows(fconv0[layer], SUBLANES, front=True) if sample else None
        h, tails = _conv_ffn(h, p["f_norm"][layer], w["f_wg"][layer], w["f_wv"][layer], w["f_cw"][layer],
                             w["f_wd"][layer], state=st, final_g=p["final_norm"] if layer == depth - 1 else None,
                             seq_len=l, tm=tm)
        fconvs.append(tails_to_state(tails, p["f_conv_w"].shape[1] - 1))
    nkh = k_new.shape[1] // dh
    return (h.reshape(b, l, d), jnp.stack(deltas), jnp.stack(dconvs), jnp.stack(fconvs),
            k_new.reshape(b, l, nkh, dh), v_new.reshape(b, l, v_new.shape[1] // vd, vd))


def kernel(x_prompt, x_sample, state_delta, state_dconv, state_fconv, cache_k, cache_v, page_table, a_norm, a_w_in, a_conv_w, a_A_log, a_dt_bias, a_o_gain, a_w_out, kv_norm, w_kv, b_norm, b_w_q, b_lambda, b_subln, b_w_out, f_norm, f_w_up, f_conv_w, f_w_down, final_norm):
    p = dict(a_norm=a_norm, a_w_in=a_w_in, a_conv_w=a_conv_w, a_A_log=a_A_log, a_dt_bias=a_dt_bias,
             a_o_gain=a_o_gain, a_w_out=a_w_out, kv_norm=kv_norm, w_kv=w_kv, b_norm=b_norm, b_w_q=b_w_q,
             b_lambda=b_lambda, b_subln=b_subln, b_w_out=b_w_out, f_norm=f_norm, f_w_up=f_w_up,
             f_conv_w=f_conv_w, f_w_down=f_w_down, final_norm=final_norm)
    w = _prep_weights(p)
    lp = x_prompt.shape[1]
    prompt = _trunk(x_prompt, jnp.arange(lp, dtype=jnp.int32), p, w, delta0=None, dconv0=None, fconv0=None,
                    cache_k=None, cache_v=None, page_table=None)
    ls = x_sample.shape[1]
    past_len = page_table.shape[1] * cache_k.shape[1]
    pool, page = cache_k.shape[:2]
    sample = _trunk(x_sample, past_len + jnp.arange(ls, dtype=jnp.int32), p, w, delta0=state_delta,
                    dconv0=state_dconv, fconv0=state_fconv,
                    cache_k=cache_k.transpose(0, 2, 3, 1).reshape(pool, -1, page),
                    cache_v=cache_v.reshape(pool, page * cache_v.shape[2], cache_v.shape[3]),
                    page_table=page_table)
    return (prompt[0], sample[0]) + prompt[1:] + sample[1:]
```

```python
import functools
import math

import jax
import jax.numpy as jnp
from jax import lax
from jax.experimental import pallas as pl
from jax.experimental.pallas import tpu as pltpu

F32 = jnp.float32
BF16 = jnp.bfloat16
EPS = 1e-6
ROPE_THETA = 10000.0
LANES = 128
SUBLANES = 8
CHUNK = 64
VMEM_LIMIT = 48 * 1024 * 1024
HI = lax.Precision.HIGHEST


def _cparams(*sem):
    return pltpu.CompilerParams(dimension_semantics=sem, vmem_limit_bytes=VMEM_LIMIT)


def _dot(a, b):
    return jnp.dot(a, b, preferred_element_type=F32)


def _dot_nt(a, b, precision=None):
    return lax.dot_general(a, b, (((1,), (1,)), ((), ())), precision=precision,
                           preferred_element_type=F32)


def _dot_tn(a, b, precision=None):
    return lax.dot_general(a, b, (((0,), (0,)), ((), ())), precision=precision,
                           preferred_element_type=F32)


def _dot_hi(a, b):
    return jnp.dot(a, b, precision=HI, preferred_element_type=F32)


def _rms(x, g):
    r = lax.rsqrt(jnp.mean(x * x, axis=-1, keepdims=True) + EPS)
    return x * r * g


def _silu(x):
    return x * jax.nn.sigmoid(x)


def _proj_kernel(*refs, rope, scale, n_out):
    if rope:
        x_ref, g_ref, w_ref, wr_ref, cos_ref, sin_ref = refs[:6]
        rest = refs[6:]
    else:
        x_ref, g_ref, w_ref = refs[:3]
        rest = refs[3:]
    outs, xn_ref = rest[:n_out], rest[n_out]

    @pl.when(pl.program_id(1) == 0)
    def _():
        xn_ref[...] = _rms(x_ref[...], g_ref[...]).astype(BF16)

    xn = xn_ref[...]
    y = _dot(xn, w_ref[...])
    if rope:
        yr = _dot(xn, wr_ref[...])
        reps = y.shape[1] // LANES
        cos = jnp.concatenate([cos_ref[...]] * reps, axis=1)
        sin = jnp.concatenate([sin_ref[...]] * reps, axis=1)
        y = y * cos + yr * sin
    if scale != 1.0:
        y = y * scale
    for o in outs:
        o[...] = y.astype(o.dtype)


def _norm_proj(x, g, w, *, w_rot=None, cos=None, sin=None, scale=1.0, out_dtypes=(F32,), tm, tn=1024):
    m, d = x.shape
    n = w.shape[1]
    tn = min(tn, n)
    rope = w_rot is not None
    in_specs = [pl.BlockSpec((tm, d), lambda i, j: (i, 0)),
                pl.BlockSpec((1, d), lambda i, j: (0, 0)),
                pl.BlockSpec((d, tn), lambda i, j: (0, j))]
    args = [x, g.reshape(1, d), w]
    if rope:
        pt = cos.shape[0] // tm
        in_specs += [pl.BlockSpec((d, tn), lambda i, j: (0, j)),
                     pl.BlockSpec((tm, LANES), lambda i, j: (i % pt, 0)),
                     pl.BlockSpec((tm, LANES), lambda i, j: (i % pt, 0))]
        args += [w_rot, cos, sin]
    res = pl.pallas_call(
        functools.partial(_proj_kernel, rope=rope, scale=scale, n_out=len(out_dtypes)),
        grid=(m // tm, n // tn),
        in_specs=in_specs,
        out_specs=[pl.BlockSpec((tm, tn), lambda i, j: (i, j)) for _ in out_dtypes],
        out_shape=[jax.ShapeDtypeStruct((m, n), dt) for dt in out_dtypes],
        scratch_shapes=[pltpu.VMEM((tm, d), BF16)],
        compiler_params=_cparams("parallel", "arbitrary"),
        name="norm_proj_rope" if rope else "norm_proj",
    )(*args)
    return res


def _proj_t_kernel(*refs, rope, scale):
    if rope:
        x_ref, g_ref, wt_ref, wrt_ref, cos_ref, sin_ref, o_ref, xn_ref = refs
    else:
        x_ref, g_ref, wt_ref, o_ref, xn_ref = refs

    @pl.when(pl.program_id(1) == 0)
    def _():
        xn_ref[...] = _rms(x_ref[...], g_ref[...]).astype(BF16)

    xn = xn_ref[...]
    y = _dot_nt(wt_ref[...], xn)
    if rope:
        yr = _dot_nt(wrt_ref[...], xn)
        reps = y.shape[0] // LANES
        cos = jnp.concatenate([cos_ref[...]] * reps, axis=0)
        sin = jnp.concatenate([sin_ref[...]] * reps, axis=0)
        y = y * cos + yr * sin
    if scale != 1.0:
        y = y * scale
    o_ref[...] = y.astype(o_ref.dtype)


def _norm_proj_t(x, g, wt, *, wt_rot=None, cos_t=None, sin_t=None, scale=1.0, tm, tn=1024):
    m, d = x.shape
    n = wt.shape[0]
    tn = min(tn, n)
    rope = wt_rot is not None
    in_specs = [pl.BlockSpec((tm, d), lambda i, j: (i, 0)),
                pl.BlockSpec((1, d), lambda i, j: (0, 0)),
                pl.BlockSpec((tn, d), lambda i, j: (j, 0))]
    args = [x, g.reshape(1, d), wt]
    if rope:
        pt = cos_t.shape[1] // tm
        in_specs += [pl.BlockSpec((tn, d), lambda i, j: (j, 0)),
                     pl.BlockSpec((LANES, tm), lambda i, j: (0, i % pt)),
                     pl.BlockSpec((LANES, tm), lambda i, j: (0, i % pt))]
        args += [wt_rot, cos_t, sin_t]
    return pl.pallas_call(
        functools.partial(_proj_t_kernel, rope=rope, scale=scale),
        grid=(m // tm, n // tn),
        in_specs=in_specs,
        out_specs=pl.BlockSpec((tn, tm), lambda i, j: (j, i)),
        out_shape=jax.ShapeDtypeStruct((n, m), BF16),
        scratch_shapes=[pltpu.VMEM((tm, d), BF16)],
        compiler_params=_cparams("parallel", "arbitrary"),
        name="norm_proj_t_rope" if rope else "norm_proj_t",
    )(*args)


def _resident(shape):
    return pl.BlockSpec(shape, lambda *_: (0,) * len(shape), pipeline_mode=pl.Buffered(1))


def _ffn_kernel(*refs, per_seq, tiles_per_seq, tf, final):
    x_ref, mix_ref, wo_ref, g_ref, wg_ref, wv_ref, cw_ref, wd_ref = refs[:8]
    pos = 8
    st_ref = fg_ref = carry_ref = None
    if per_seq:
        st_ref = refs[pos]
        pos += 1
    if final:
        fg_ref = refs[pos]
        pos += 1
    o_ref, tail_ref, act_ref, pad_ref = refs[pos:pos + 4]
    if not per_seq:
        carry_ref = refs[pos + 4]
    i = pl.program_id(0)
    tm = x_ref.shape[0]
    nf = wg_ref.shape[1] // tf
    x = x_ref[...] + _dot(mix_ref[...], wo_ref[...])
    xn = _rms(x, g_ref[...]).astype(BF16)

    if not per_seq:
        @pl.when((i % tiles_per_seq) == 0)
        def _():
            carry_ref[...] = jnp.zeros_like(carry_ref)

    for j in range(nf):
        cols = slice(j * tf, (j + 1) * tf)
        gate = _dot(xn, wg_ref[:, cols])
        val = _dot(xn, wv_ref[:, cols])
        cw = cw_ref[:, cols]
        pad = pad_ref.at[j % 2]
        if per_seq:
            g3 = gate.reshape(tm // SUBLANES, SUBLANES, tf)
            pad[:, :SUBLANES, :] = st_ref[:, :, cols]
            pad[:, SUBLANES:, :] = g3
            tail_ref[:, :, cols] = g3
            conv = (cw[2:3][None] * g3 + cw[1:2][None] * pad[:, SUBLANES - 1:2 * SUBLANES - 1, :]
                    + cw[0:1][None] * pad[:, SUBLANES - 2:2 * SUBLANES - 2, :])
        else:
            tail = gate[tm - SUBLANES:]
            pad[:SUBLANES, :] = carry_ref[:, cols]
            pad[SUBLANES:, :] = gate
            tail_ref[0, :, cols] = tail
            carry_ref[:, cols] = tail
            conv = (cw[2:3] * gate + cw[1:2] * pad[pl.ds(SUBLANES - 1, tm), :]
                    + cw[0:1] * pad[pl.ds(SUBLANES - 2, tm), :])
        act_ref[:, cols] = (_silu(conv).reshape(tm, tf) * val).astype(BF16)

    y = x + _dot(act_ref[...], wd_ref[...])
    if final:
        y = _rms(y, fg_ref[...])
    o_ref[...] = y


def _conv_ffn(x, mix, w_o, g, wg, wv, cw, wd, *, state=None, final_g=None, seq_len, tm, tf=256):
    m, d = x.shape
    f = wg.shape[1]
    per_seq = state is not None
    final = final_g is not None
    groups = tm // SUBLANES if per_seq else 1
    in_specs = [pl.BlockSpec((tm, d), lambda i: (i, 0)),
                pl.BlockSpec((tm, mix.shape[1]), lambda i: (i, 0)), _resident(w_o.shape),
                _resident((1, d)), _resident((d, f)), _resident((d, f)), _resident((SUBLANES, f)), _resident((f, d))]
    args = [x, mix, w_o, g.reshape(1, d), wg, wv, cw, wd]
    if per_seq:
        in_specs.append(pl.BlockSpec((groups, SUBLANES, f), lambda i: (i, 0, 0)))
        args.append(state)
    if final:
        in_specs.append(_resident((1, d)))
        args.append(final_g.reshape(1, d))
    if per_seq:
        scratch = [pltpu.VMEM((tm, f), BF16), pltpu.VMEM((2, groups, 2 * SUBLANES, tf), F32)]
    else:
        scratch = [pltpu.VMEM((tm, f), BF16), pltpu.VMEM((2, SUBLANES + tm, tf), F32), pltpu.VMEM((SUBLANES, f), F32)]
    return pl.pallas_call(
        functools.partial(_ffn_kernel, per_seq=per_seq, tiles_per_seq=max(seq_len // tm, 1), tf=tf, final=final),
        grid=(m // tm,),
        in_specs=in_specs,
        out_specs=[pl.BlockSpec((tm, d), lambda i: (i, 0)),
                   pl.BlockSpec((groups, SUBLANES, f), lambda i: (i, 0, 0))],
        out_shape=[jax.ShapeDtypeStruct((m, d), F32),
                   jax.ShapeDtypeStruct((m // tm * groups, SUBLANES, f), F32)],
        scratch_shapes=scratch,
        compiler_params=_cparams("arbitrary"),
        name="conv_ffn",
    )(*args)


def _gdn_in_kernel(*refs, per_seq, tiles_per_seq, tn, nq, nqkv, nz, qscale):
    x_ref, g_ref, w_ref, cw_ref = refs[:4]
    pos = 4
    st_ref = carry_ref = None
    if per_seq:
        st_ref = refs[pos]
        pos += 1
    o_ref, ba_ref, tail_ref, pad_ref = refs[pos:pos + 4]
    if not per_seq:
        carry_ref = refs[pos + 4]
    i = pl.program_id(0)
    tm = x_ref.shape[0]
    xn = _rms(x_ref[...], g_ref[...]).astype(BF16)

    if not per_seq:
        @pl.when((i % tiles_per_seq) == 0)
        def _():
            carry_ref[...] = jnp.zeros_like(carry_ref)

    def l2n(y, s):
        parts = []
        for a in range(tn // LANES):
            ya = y[:, a * LANES:(a + 1) * LANES]
            r = lax.rsqrt(jnp.sum(ya * ya, axis=-1, keepdims=True) + EPS)
            parts.append(ya * (r * s) if s != 1.0 else ya * r)
        return jnp.concatenate(parts, axis=1)

    for j in range(nqkv + nz + 1):
        cols = slice(j * tn, (j + 1) * tn)
        pre = _dot(xn, w_ref[:, cols])
        if j == nqkv + nz:
            ba_ref[...] = pre[:, :2 * LANES]
        elif j >= nqkv:
            o_ref[:, cols] = pre
        else:
            cw = cw_ref[:, cols]
            pad = pad_ref.at[j % 2]
            if per_seq:
                p3 = pre.reshape(tm // SUBLANES, SUBLANES, tn)
                pad[:, :SUBLANES, :] = st_ref[:, :, cols]
                pad[:, SUBLANES:, :] = p3
                tail_ref[:, :, cols] = p3
                conv = cw[3:4][None] * p3
                for t in range(3):
                    conv = conv + cw[t:t + 1][None] * pad[:, SUBLANES - 3 + t:2 * SUBLANES - 3 + t, :]
            else:
                tail = pre[tm - SUBLANES:]
                pad[:SUBLANES, :] = carry_ref[:, cols]
                pad[SUBLANES:, :] = pre
                tail_ref[0, :, cols] = tail
                carry_ref[:, cols] = tail
                conv = cw[3:4] * pre
                for t in range(3):
                    conv = conv + cw[t:t + 1] * pad[pl.ds(SUBLANES - 3 + t, tm), :]
            y = _silu(conv).reshape(tm, tn)
            if j < nq:
                y = l2n(y, qscale)
            elif j < 2 * nq:
                y = l2n(y, 1.0)
            o_ref[:, cols] = y


def _gdn_in(x, g, w_all, cw, *, state=None, seq_len, a_qk, a_qkv, a_vd, qscale, tm, tn=512):
    m, d = x.shape
    per_seq = state is not None
    nq, nqkv, nz = a_qk // tn, a_qkv // tn, a_vd // tn
    groups = tm // SUBLANES if per_seq else 1
    in_specs = [pl.BlockSpec((tm, d), lambda i: (i, 0)),
                _resident((1, d)), _resident(w_all.shape), _resident((SUBLANES, a_qkv))]
    args = [x, g.reshape(1, d), w_all, cw]
    if per_seq:
        in_specs.append(pl.BlockSpec((groups, SUBLANES, a_qkv), lambda i: (i, 0, 0)))
        args.append(state)
    if per_seq:
        scratch = [pltpu.VMEM((2, groups, 2 * SUBLANES, tn), F32)]
    else:
        scratch = [pltpu.VMEM((2, SUBLANES + tm, tn), F32), pltpu.VMEM((SUBLANES, a_qkv), F32)]
    return pl.pallas_call(
        functools.partial(_gdn_in_kernel, per_seq=per_seq, tiles_per_seq=max(seq_len // tm, 1),
                          tn=tn, nq=nq, nqkv=nqkv, nz=nz, qscale=qscale),
        grid=(m // tm,),
        in_specs=in_specs,
        out_specs=[pl.BlockSpec((tm, a_qkv + a_vd), lambda i: (i, 0)),
                   pl.BlockSpec((tm, 2 * LANES), lambda i: (i, 0)),
                   pl.BlockSpec((groups, SUBLANES, a_qkv), lambda i: (i, 0, 0))],
        out_shape=[jax.ShapeDtypeStruct((m, a_qkv + a_vd), F32),
                   jax.ShapeDtypeStruct((m, 2 * LANES), F32),
                   jax.ShapeDtypeStruct((m // tm * groups, SUBLANES, a_qkv), F32)],
        scratch_shapes=scratch,
        compiler_params=_cparams("arbitrary"),
        name="gdn_in",
    )(*args)


def _split(a):
    hi = a.astype(BF16)
    return hi, (a - hi.astype(F32)).astype(BF16)


def _dot3(a, b, nt=False):
    f = _dot_nt if nt else _dot
    return f(a[0], b[0]) + (f(a[0], b[1]) + f(a[1], b[0]))


def _unit_lower_inverses(lms, row, col, eye):
    def same(s):
        return (row >> s) == (col >> s)

    def dot1(a, b):
        return _dot(a.astype(BF16), b.astype(BF16))

    nd = [jnp.where(same(3), -lm, 0.0) for lm in lms]
    nd2 = [dot1(a, a) for a in nd]
    nd4 = [dot1(a, a) for a in nd2]
    x = [dot1(eye + a, eye + b) for a, b in zip(nd, nd2)]
    x = [dot1(a, eye + b) for a, b in zip(x, nd4)]
    for s in (3, 4, 5):
        mask = same(s + 1) & jnp.logical_not(same(s))
        y = [dot1(a, jnp.where(mask, lm, 0.0)) for a, lm in zip(x, lms)]
        x = [a - dot1(b, a) for a, b in zip(x, y)]
    res = [eye - _dot3(_split(eye + lm), _split(a)) for lm, a in zip(lms, x)]
    return [a + dot1(a, r) for a, r in zip(x, res)]


def _gdn_chunk_kernel(*refs, heads, dk, dv, has_s0):
    q_ref, k_ref, v_ref, z_ref, ba_ref, na_ref, dt_ref, gain_ref = refs[:8]
    pos = 8
    s0_ref = None
    if has_s0:
        s0_ref = refs[pos]
        pos += 1
    o_ref, sout_ref, s_ref = refs[pos:pos + 3]
    n = pl.program_id(1)

    @pl.when(n == 0)
    def _():
        if has_s0:
            s_ref[...] = s0_ref[...]
        else:
            s_ref[...] = jnp.zeros_like(s_ref)

    bt, cr = q_ref.shape[0], q_ref.shape[1]

    def pad(a):
        if cr == CHUNK:
            return a
        return jnp.concatenate([a, jnp.zeros((CHUNK - cr, a.shape[1]), a.dtype)], axis=0)

    row = lax.broadcasted_iota(jnp.int32, (CHUNK, CHUNK), 0)
    col = lax.broadcasted_iota(jnp.int32, (CHUNK, CHUNK), 1)
    lower = row >= col
    strict = row > col
    eye = (row == col).astype(F32)
    tri = lower.astype(F32)
    gain = gain_ref[...]

    qh, kh, vh, bcol, egcol, eglcol, eglast, decay = [], [], [], [], [], [], [], []
    for b in range(bt):
        ba = ba_ref[b]
        beta = pad(jax.nn.sigmoid(ba[:, :LANES]))
        g = pad(na_ref[...] * jax.nn.softplus(ba[:, LANES:] + dt_ref[...]))
        q, k, v = pad(q_ref[b]), pad(k_ref[b]), pad(v_ref[b])
        gc = _dot_hi(tri, g)
        gct = gc.T
        glast = gc[CHUNK - 1:CHUNK]
        eg = jnp.exp(gc)
        egl = jnp.exp(glast - gc)
        egt = jnp.exp(glast)
        for h in range(heads):
            qh.append(q[:, h * dk:(h + 1) * dk])
            kh.append(k[:, h * dk:(h + 1) * dk])
            vh.append(v[:, h * dv:(h + 1) * dv])
            bcol.append(beta[:, h:h + 1])
            egcol.append(eg[:, h:h + 1])
            eglcol.append(egl[:, h:h + 1])
            eglast.append(egt[:, h:h + 1])
            decay.append(jnp.where(lower, jnp.exp(gc[:, h:h + 1] - gct[h:h + 1, :]), 0.0))
    ps = range(bt * heads)
    kb = [kh[i] * bcol[i] for i in ps]
    kk = [_dot3(_split(kb[i]), _split(kh[i]), nt=True) for i in ps]
    lm = [jnp.where(strict, kk[i] * decay[i], 0.0) for i in ps]
    t = _unit_lower_inverses(lm, row, col, eye)
    rhs = [jnp.concatenate([vh[i] * bcol[i], kb[i] * egcol[i]], axis=1) for i in ps]
    sol = [_dot3(_split(t[i]), _split(rhs[i])) for i in ps]
    kh16 = [kh[i].astype(BF16) for i in ps]
    attn = [(_dot_nt(qh[i].astype(BF16), kh16[i]) * decay[i]).astype(BF16) for i in ps]
    qe16 = [(qh[i] * egcol[i]).astype(BF16) for i in ps]
    kd16 = [(kh[i] * eglcol[i]).astype(BF16) for i in ps]
    sh = [s_ref[i // heads, i % heads] for i in ps]
    sh16 = [a.astype(BF16) for a in sh]
    v_new = [sol[i][:, :dv] - _dot(sol[i][:, dv:].astype(BF16), sh16[i]) for i in ps]
    vn16 = [a.astype(BF16) for a in v_new]
    o = [_dot(qe16[i], sh16[i]) + _dot(attn[i], vn16[i]) for i in ps]
    for i in ps:
        s_ref[i // heads, i % heads] = sh[i] * eglast[i] + _dot_tn(kd16[i], vn16[i])
    for i in ps:
        b, h = i // heads, i % heads
        oh = o[i][:cr]
        zh = z_ref[b, :, h * dv:(h + 1) * dv]
        on = oh * lax.rsqrt(jnp.mean(oh * oh, axis=-1, keepdims=True) + EPS) * gain * _silu(zh)
        o_ref[b, :, h * dv:(h + 1) * dv] = on.astype(o_ref.dtype)

    @pl.when(n == pl.num_programs(1) - 1)
    def _():
        sout_ref[...] = s_ref[...]


def _gdn_chunk(qkvz, ba, neg_a, dt_bias, gain, *, s0, batch, seq_len, heads, dk, dv, bt=4):
    m = qkvz.shape[0]
    cr = min(CHUNK, seq_len)
    nc = seq_len // cr
    hd = heads * dk
    bt = math.gcd(batch, bt)
    has_s0 = s0 is not None
    qkvz = qkvz.reshape(batch, seq_len, -1)
    ba = ba.reshape(batch, seq_len, -1)

    def blk(c):
        return pl.BlockSpec((bt, cr, hd), lambda b, n: (b, n, c))

    in_specs = [blk(0), blk(1), blk(2), blk(3),
                pl.BlockSpec((bt, cr, 2 * LANES), lambda b, n: (b, n, 0)),
                pl.BlockSpec((1, LANES), lambda b, n: (0, 0)),
                pl.BlockSpec((1, LANES), lambda b, n: (0, 0)),
                pl.BlockSpec((1, dv), lambda b, n: (0, 0))]
    args = [qkvz, qkvz, qkvz, qkvz, ba, neg_a, dt_bias, gain.reshape(1, dv)]
    if has_s0:
        in_specs.append(pl.BlockSpec((bt, heads, dk, dv), lambda b, n: (b, 0, 0, 0)))
        args.append(s0)
    o, s_fin = pl.pallas_call(
        functools.partial(_gdn_chunk_kernel, heads=heads, dk=dk, dv=dv, has_s0=has_s0),
        grid=(batch // bt, nc),
        in_specs=in_specs,
        out_specs=[pl.BlockSpec((bt, cr, hd), lambda b, n: (b, n, 0)),
                   pl.BlockSpec((bt, heads, dk, dv), lambda b, n: (b, 0, 0, 0))],
        out_shape=[jax.ShapeDtypeStruct((batch, seq_len, hd), BF16),
                   jax.ShapeDtypeStruct((batch, heads, dk, dv), F32)],
        scratch_shapes=[pltpu.VMEM((bt, heads, dk, dv), F32)],
        compiler_params=_cparams("parallel", "arbitrary"),
        name="gdn_chunk",
    )(*args)
    return o.reshape(m, hd), s_fin


def _lambda(lv_ref, lam_init):
    lv = lv_ref[...]
    a = jnp.sum(lv[0:1] * lv[1:2], axis=-1, keepdims=True)
    b = jnp.sum(lv[2:3] * lv[3:4], axis=-1, keepdims=True)
    return jnp.exp(a) - jnp.exp(b) + lam_init


def _flash_kernel(qt_ref, kt_ref, q_ref, k_ref, v_ref, lv_ref, sub_ref, o_ref, qs_ref, m_ref, acc_ref,
                  *, dh, lam_init):
    step = pl.program_id(2)
    qi = qt_ref[step]
    ki = kt_ref[step]
    tq = q_ref.shape[1]
    tk = k_ref.shape[0]
    vd = v_ref.shape[0]

    @pl.when(ki == 0)
    def _():
        q = q_ref[...]
        feat = lax.broadcasted_iota(jnp.int32, q.shape, 0)
        zero = jnp.zeros_like(q)
        qs_ref[:, :tq] = jnp.where(feat < dh, q, zero)
        qs_ref[:, tq:] = jnp.where(feat >= dh, q, zero)
        m_ref[...] = jnp.full_like(m_ref, -jnp.inf)
        acc_ref[...] = jnp.zeros_like(acc_ref)

    def accumulate(masked):
        s = _dot(k_ref[...], qs_ref[...])
        if masked:
            kpos = ki * tk + lax.broadcasted_iota(jnp.int32, s.shape, 0)
            qpos = qi * tq + (lax.broadcasted_iota(jnp.int32, s.shape, 1) & (tq - 1))
            s = jnp.where(kpos <= qpos, s, -jnp.inf)
        m_old = m_ref[...]
        m_new = jnp.maximum(m_old, jnp.max(s, axis=0, keepdims=True))
        alpha = jnp.exp2(m_old - m_new)
        p = jnp.exp2(s - m_new).astype(BF16)
        v_ones = jnp.concatenate([v_ref[...], jnp.ones((acc_ref.shape[0] - vd, tk), BF16)], axis=0)
        acc_ref[...] = alpha * acc_ref[...] + _dot(v_ones, p)
        m_ref[...] = m_new

    below_diagonal = (ki + 1) * tk - 1 <= qi * tq

    @pl.when(below_diagonal)
    def _():
        accumulate(False)

    @pl.when(jnp.logical_not(below_diagonal))
    def _():
        accumulate(True)

    @pl.when((ki + 1) * tk == (qi + 1) * tq)
    def _():
        lam = _lambda(lv_ref, lam_init)
        a = acc_ref[:vd] / acc_ref[vd:vd + 1]
        o = a[:, :tq] - lam * a[:, tq:]
        r = lax.rsqrt(jnp.mean(o * o, axis=0, keepdims=True) + EPS)
        o = o * r * (sub_ref[...] * (1.0 - lam_init))
        o_ref[...] = o.T.astype(o_ref.dtype)


def _flash_prompt(q_t, k, v_t, lam_vecs, subln, *, batch, seq_len, dh, vd, lam_init, tq=1024, tk=512):
    m, width = k.shape
    pairs = width // (2 * dh)
    tq = min(tq, seq_len)
    tk = min(tk, tq)
    nq, nk = seq_len // tq, seq_len // tk
    assert tq & (tq - 1) == 0 and tq % tk == 0
    steps = [(i, j) for i in range(nq) for j in range((i + 1) * tq // tk)]
    q_tab = jnp.asarray([s[0] for s in steps], jnp.int32)
    k_tab = jnp.asarray([s[1] for s in steps], jnp.int32)
    grid_spec = pltpu.PrefetchScalarGridSpec(
        num_scalar_prefetch=2,
        grid=(batch, pairs, len(steps)),
        in_specs=[pl.BlockSpec((2 * dh, tq), lambda b, h, s, qt, kt: (h, b * nq + qt[s])),
                  pl.BlockSpec((tk, 2 * dh), lambda b, h, s, qt, kt: (b * nk + kt[s], h)),
                  pl.BlockSpec((vd, tk), lambda b, h, s, qt, kt: (h, b * nk + kt[s])),
                  pl.BlockSpec(lam_vecs.shape, lambda b, h, s, qt, kt: (0, 0)),
                  pl.BlockSpec((vd, 1), lambda b, h, s, qt, kt: (0, 0))],
        out_specs=pl.BlockSpec((tq, vd), lambda b, h, s, qt, kt: (b * nq + qt[s], h)),
        scratch_shapes=[pltpu.VMEM((2 * dh, 2 * tq), BF16),
                        pltpu.VMEM((1, 2 * tq), F32),
                        pltpu.VMEM((vd + 2 * SUBLANES, 2 * tq), F32)])
    return pl.pallas_call(
        functools.partial(_flash_kernel, dh=dh, lam_init=lam_init),
        grid_spec=grid_spec,
        out_shape=jax.ShapeDtypeStruct((m, pairs * vd), BF16),
        compiler_params=_cparams("parallel", "parallel", "arbitrary"),
        name="diff_flash",
    )(q_tab, k_tab, q_t, k, v_t, lam_vecs, subln.reshape(vd, 1))


def _paged_kernel(*refs, nh, dh, vd, lam_init, group):
    q_ref, kn_ref, vn_ref = refs[1:4]
    kc_refs = refs[4:4 + group]
    vc_refs = refs[4 + group:4 + 2 * group]
    lv_ref, sub_ref, o_ref, qbd_ref, m_ref, l_ref, acc_ref = refs[4 + 2 * group:]
    p = pl.program_id(1)
    t = q_ref.shape[1]
    width = q_ref.shape[2]
    rows = nh * t
    page = kc_refs[0].shape[2]
    nvh = width // vd

    def accum(kt16, v16, mask):
        s = _dot(qbd_ref[...], kt16)
        if mask is not None:
            s = jnp.where(mask, s, -jnp.inf)
        m_old = m_ref[...]
        m_new = jnp.maximum(m_old, jnp.max(s, axis=-1, keepdims=True))
        alpha = jnp.exp(m_old - m_new)
        pr = jnp.exp(s - m_new)
        l_ref[...] = alpha * l_ref[...] + jnp.sum(pr, axis=-1, keepdims=True)
        acc_ref[...] = alpha * acc_ref[...] + _dot(pr.astype(BF16), v16)
        m_ref[...] = m_new

    @pl.when(p == 0)
    def _():
        q = q_ref[0].astype(F32)
        q3 = jnp.broadcast_to(q[None], (nh, t, width))
        hd = lax.broadcasted_iota(jnp.int32, (nh, t, width), 0)
        ln = lax.broadcasted_iota(jnp.int32, (nh, t, width), 2)
        qbd = jnp.where((ln >= hd * dh) & (ln < (hd + 1) * dh), q3, 0.0)
        qbd_ref[...] = qbd.reshape(rows, width).astype(BF16)
        m_ref[...] = jnp.full_like(m_ref, -jnp.inf)
        l_ref[...] = jnp.zeros_like(l_ref)
        acc_ref[...] = jnp.zeros_like(acc_ref)
        zpad = jnp.zeros((page - t, width), F32)
        kt16 = jnp.concatenate([kn_ref[0], zpad], axis=0).T.astype(BF16)
        v16 = jnp.concatenate([vn_ref[0], zpad], axis=0).astype(BF16)
        r = lax.broadcasted_iota(jnp.int32, (rows, page), 0)
        c = lax.broadcasted_iota(jnp.int32, (rows, page), 1)
        accum(kt16, v16, c <= (r & (t - 1)))

    @pl.when(p > 0)
    def _():
        v = jnp.concatenate(
            [jnp.concatenate([vc[0, pl.ds(h, page, stride=nvh), :] for h in range(nvh)], axis=1).astype(BF16)
             for vc in vc_refs], axis=0)
        kt = jnp.concatenate([kc[0].astype(BF16) for kc in kc_refs], axis=1)
        accum(kt, v, None)

    @pl.when(p == pl.num_programs(1) - 1)
    def _():
        lam = _lambda(lv_ref, lam_init)
        r = lax.broadcasted_iota(jnp.int32, (rows, 1), 0)
        odd = ((r // t) & 1) == 1
        wgt = jnp.where(odd, -lam, 1.0) / l_ref[...]
        a3 = (acc_ref[...] * wgt).reshape(nh, t, width)
        hd = lax.broadcasted_iota(jnp.int32, (nh, t, width), 0)
        ln = lax.broadcasted_iota(jnp.int32, (nh, t, width), 2)
        pair = hd >> 1
        o = jnp.sum(jnp.where((ln >= pair * vd) & (ln < (pair + 1) * vd), a3, 0.0), axis=0)
        sub = sub_ref[...]
        for h in range(width // vd):
            oh = o[:, h * vd:(h + 1) * vd]
            o_ref[0, :, h * vd:(h + 1) * vd] = (_rms(oh, sub) * (1.0 - lam_init)).astype(o_ref.dtype)


def _paged_attention(q, k_new, v_new, cache_k, cache_v, page_table, lam_vecs, subln, *, dh, vd, lam_init, group=8):
    b, t, width = q.shape
    npg = page_table.shape[1]
    page = cache_k.shape[2]
    nh = width // dh
    nvh = width // vd
    assert t & (t - 1) == 0 and t <= page and nh == 2 * nvh and cache_v.shape[1] == page * nvh
    group = math.gcd(npg, group)

    def pidx(i):
        return lambda bb, p, pt: (pt[bb * npg + jnp.maximum(p - 1, 0) * group + i], 0, 0)

    grid_spec = pltpu.PrefetchScalarGridSpec(
        num_scalar_prefetch=1,
        grid=(b, npg // group + 1),
        in_specs=[pl.BlockSpec((1, t, width), lambda bb, p, pt: (bb, 0, 0)),
                  pl.BlockSpec((1, t, width), lambda bb, p, pt: (bb, 0, 0)),
                  pl.BlockSpec((1, t, width), lambda bb, p, pt: (bb, 0, 0))]
        + [pl.BlockSpec((1, width, page), pidx(i)) for i in range(group)]
        + [pl.BlockSpec((1, page * nvh, vd), pidx(i)) for i in range(group)]
        + [pl.BlockSpec(lam_vecs.shape, lambda bb, p, pt: (0, 0)),
           pl.BlockSpec((1, vd), lambda bb, p, pt: (0, 0))],
        out_specs=pl.BlockSpec((1, t, width), lambda bb, p, pt: (bb, 0, 0)),
        scratch_shapes=[pltpu.VMEM((nh * t, width), BF16),
                        pltpu.VMEM((nh * t, 1), F32),
                        pltpu.VMEM((nh * t, 1), F32),
                        pltpu.VMEM((nh * t, width), F32)])
    return pl.pallas_call(
        functools.partial(_paged_kernel, nh=nh, dh=dh, vd=vd, lam_init=lam_init, group=group),
        grid_spec=grid_spec,
        out_shape=jax.ShapeDtypeStruct((b, t, width), BF16),
        compiler_params=_cparams("parallel", "arbitrary"),
        name="diff_paged",
    )(page_table.reshape(-1), q, k_new, v_new, *([cache_k] * group), *([cache_v] * group),
      lam_vecs, subln.reshape(1, vd))


def _rot_weight(w, dh):
    k, n = w.shape
    w4 = w.reshape(k, n // dh, 2, dh // 2)
    return jnp.stack([-w4[:, :, 1], w4[:, :, 0]], axis=2).reshape(k, n)


def _rope_tables(pos, dh):
    half = dh // 2
    inv = 1.0 / (ROPE_THETA ** (jnp.arange(half, dtype=F32) / half))
    ang = pos.astype(F32)[:, None] * inv[None, :]
    reps = LANES // half
    return jnp.tile(jnp.cos(ang), (1, reps)), jnp.tile(jnp.sin(ang), (1, reps))


def _pad_rows(a, rows, front):
    pad = [(0, 0)] * a.ndim
    pad[-2] = (rows - a.shape[-2], 0) if front else (0, rows - a.shape[-2])
    return jnp.pad(a, pad)


def _prep_weights(p):
    n_a, d, a_in = p["a_w_in"].shape
    heads = p["a_A_log"].shape[1]
    dv = p["a_o_gain"].shape[1]
    a_vd = heads * dv
    a_qkv = a_in - a_vd - 2 * heads
    dh = p["b_lambda"].shape[-1]
    w = {}
    w_in = p["a_w_in"]
    tn = 512
    zpad = jnp.zeros((n_a, d, LANES - heads), F32)
    w["a_w_all"] = jnp.concatenate(
        [w_in[:, :, :a_qkv + a_vd], w_in[:, :, a_qkv + a_vd:a_qkv + a_vd + heads], zpad,
         w_in[:, :, a_qkv + a_vd + heads:], zpad, jnp.zeros((n_a, d, tn - 2 * LANES), F32)], axis=2).astype(BF16)
    w["a_cw"] = _pad_rows(p["a_conv_w"], SUBLANES, front=False)
    hp = jnp.zeros((n_a, LANES - heads), F32)
    w["a_neg_a"] = jnp.concatenate([-jnp.exp(p["a_A_log"].astype(F32)), hp], axis=1)[:, None, :]
    w["a_dt"] = jnp.concatenate([p["a_dt_bias"].astype(F32), hp], axis=1)[:, None, :]
    w["a_w_out"] = p["a_w_out"].astype(BF16)
    kq = p["w_kv"].shape[1] - (p["b_w_out"].shape[1])
    w["w_k"] = p["w_kv"][:, :kq].astype(BF16)
    w["w_k_rot"] = _rot_weight(p["w_kv"][:, :kq], dh).astype(BF16)
    w["w_v"] = p["w_kv"][:, kq:].astype(BF16)
    w["b_w_q"] = p["b_w_q"].astype(BF16)
    w["b_w_q_rot"] = jnp.stack([_rot_weight(p["b_w_q"][j], dh) for j in range(p["b_w_q"].shape[0])]).astype(BF16)
    w["b_w_out"] = p["b_w_out"].astype(BF16)
    w["w_v_t"] = w["w_v"].T
    w["b_w_q_t"] = jnp.swapaxes(w["b_w_q"], 1, 2)
    w["b_w_q_rot_t"] = jnp.swapaxes(w["b_w_q_rot"], 1, 2)
    f = p["f_w_down"].shape[1]
    w["f_wg"] = p["f_w_up"][:, :, :f].astype(BF16)
    w["f_wv"] = p["f_w_up"][:, :, f:].astype(BF16)
    w["f_cw"] = _pad_rows(p["f_conv_w"], SUBLANES, front=False)
    w["f_wd"] = p["f_w_down"].astype(BF16)
    return w


def _trunk(x, pos, p, w, *, delta0, dconv0, fconv0, cache_k, cache_v, page_table):
    b, l, d = x.shape
    m = b * l
    sample = page_table is not None
    depth = p["f_norm"].shape[0]
    n_a = p["a_norm"].shape[0]
    heads = p["a_A_log"].shape[1]
    dv = p["a_o_gain"].shape[1]
    dk = (p["a_w_in"].shape[2] - 2 * heads - 2 * heads * dv) // (2 * heads)
    a_vd = heads * dv
    a_qk = heads * dk
    a_qkv = 2 * a_qk + a_vd
    dh = p["b_lambda"].shape[-1]
    vd = p["b_subln"].shape[-1]
    f = p["f_w_down"].shape[1]
    tm = m if sample else min(512, l)
    assert m % tm == 0 and (sample or l % tm == 0)

    h = x.reshape(m, d)
    cos, sin = _rope_tables(pos, dh)
    if sample:
        cos, sin = jnp.tile(cos, (b, 1)), jnp.tile(sin, (b, 1))

    def tails_to_state(tails, rows):
        if sample:
            return tails[:, SUBLANES - rows:, :]
        per = l // tm
        return tails.reshape(b, per, SUBLANES, -1)[:, per - 1, SUBLANES - rows:, :]

    deltas, dconvs, fconvs = [], [], []
    k_new = v_new = k16 = v16_t = None
    for layer in range(depth):
        if layer < n_a:
            st = _pad_rows(dconv0[layer], SUBLANES, front=True) if sample else None
            qkvz, ba, tails = _gdn_in(h, p["a_norm"][layer], w["a_w_all"][layer], w["a_cw"][layer], state=st,
                                      seq_len=l, a_qk=a_qk, a_qkv=a_qkv, a_vd=a_vd, qscale=dk ** -0.5, tm=tm)
            dconvs.append(tails_to_state(tails, p["a_conv_w"].shape[1] - 1))
            o, s_fin = _gdn_chunk(qkvz, ba, w["a_neg_a"][layer], w["a_dt"][layer], p["a_o_gain"][layer],
                                  s0=delta0[layer] if sample else None, batch=b, seq_len=l,
                                  heads=heads, dk=dk, dv=dv)
            deltas.append(s_fin)
            w_o = w["a_w_out"][layer]
        else:
            if layer == n_a:
                k_new, k16 = _norm_proj(h, p["kv_norm"], w["w_k"], w_rot=w["w_k_rot"], cos=cos, sin=sin,
                                        out_dtypes=(F32, BF16), tm=tm)
                (v_new,) = _norm_proj(h, p["kv_norm"], w["w_v"], out_dtypes=(F32,), tm=tm)
                if not sample:
                    v16_t = _norm_proj_t(h, p["kv_norm"], w["w_v_t"], tm=tm)
            j = layer - n_a
            lam_init = 0.8 - 0.6 * math.exp(-0.3 * layer)
            if sample:
                (q16,) = _norm_proj(h, p["b_norm"][j], w["b_w_q"][j], w_rot=w["b_w_q_rot"][j], cos=cos, sin=sin,
                                    scale=dh ** -0.5, out_dtypes=(BF16,), tm=tm)
                width = q16.shape[1]
                o = _paged_attention(q16.reshape(b, l, width), k_new.reshape(b, l, width), v_new.reshape(b, l, width),
                                     cache_k, cache_v, page_table, p["b_lambda"][j], p["b_subln"][j],
                                     dh=dh, vd=vd, lam_init=lam_init).reshape(m, width)
            else:
                q16_t = _norm_proj_t(h, p["b_norm"][j], w["b_w_q_t"][j], wt_rot=w["b_w_q_rot_t"][j],
                                     cos_t=cos.T, sin_t=sin.T, scale=dh ** -0.5 * math.log2(math.e), tm=tm)
                o = _flash_prompt(q16_t, k16, v16_t, p["b_lambda"][j], p["b_subln"][j], batch=b, seq_len=l,
                                  dh=dh, vd=vd, lam_init=lam_init)
            w_o = w["b_w_out"][j]
        st = _pad_rows(fconv0[layer], SUBLANES, front=True) if sample else None
        h, tails = _conv_ffn(h, o, w_o, p["f_norm"][layer], w["f_wg"][layer], w["f_wv"][layer], w["f_cw"][layer],
                             w["f_wd"][layer], state=st, final_g=p["final_norm"] if layer == depth - 1 else None,
                             seq_len=l, tm=tm)
        fconvs.append(tails_to_state(tails, p["f_conv_w"].shape[1] - 1))
    nkh = k_new.shape[1] // dh
    return (h.reshape(b, l, d), jnp.stack(deltas), jnp.stack(dconvs), jnp.stack(fconvs),
            k_new.reshape(b, l, nkh, dh), v_new.reshape(b, l, v_new.shape[1] // vd, vd))


def kernel(x_prompt, x_sample, state_delta, state_dconv, state_fconv, cache_k, cache_v, page_table, a_norm, a_w_in, a_conv_w, a_A_log, a_dt_bias, a_o_gain, a_w_out, kv_norm, w_kv, b_norm, b_w_q, b_lambda, b_subln, b_w_out, f_norm, f_w_up, f_conv_w, f_w_down, final_norm):
    p = dict(a_norm=a_norm, a_w_in=a_w_in, a_conv_w=a_conv_w, a_A_log=a_A_log, a_dt_bias=a_dt_bias,
             a_o_gain=a_o_gain, a_w_out=a_w_out, kv_norm=kv_norm, w_kv=w_kv, b_norm=b_norm, b_w_q=b_w_q,
             b_lambda=b_lambda, b_subln=b_subln, b_w_out=b_w_out, f_norm=f_norm, f_w_up=f_w_up,
             f_conv_w=f_conv_w, f_w_down=f_w_down, final_norm=final_norm)
    w = _prep_weights(p)
    lp = x_prompt.shape[1]
    prompt = _trunk(x_prompt, jnp.arange(lp, dtype=jnp.int32), p, w, delta0=None, dconv0=None, fconv0=None,
                    cache_k=None, cache_v=None, page_table=None)
    ls = x_sample.shape[1]
    past_len = page_table.shape[1] * cache_k.shape[1]
    pool, page = cache_k.shape[:2]
    sample = _trunk(x_sample, past_len + jnp.arange(ls, dtype=jnp.int32), p, w, delta0=state_delta,
                    dconv0=state_dconv, fconv0=state_fconv,
                    cache_k=cache_k.transpose(0, 2, 3, 1).reshape(pool, -1, page),
                    cache_v=cache_v.reshape(pool, page * cache_v.shape[2], cache_v.shape[3]),
                    page_table=page_table)
    return (prompt[0], sample[0]) + prompt[1:] + sample[1:]
```

```python
import functools
import math

import jax
import jax.numpy as jnp
from jax import lax
from jax.experimental import pallas as pl
from jax.experimental.pallas import tpu as pltpu

F32 = jnp.float32
BF16 = jnp.bfloat16
EPS = 1e-6
ROPE_THETA = 10000.0
LANES = 128
SUBLANES = 8
CHUNK = 64
VMEM_LIMIT = 48 * 1024 * 1024
HI = lax.Precision.HIGHEST


def _cparams(*sem):
    return pltpu.CompilerParams(dimension_semantics=sem, vmem_limit_bytes=VMEM_LIMIT)


def _dot(a, b):
    return jnp.dot(a, b, preferred_element_type=F32)


def _dot_nt(a, b, precision=None):
    return lax.dot_general(a, b, (((1,), (1,)), ((), ())), precision=precision,
                           preferred_element_type=F32)


def _dot_tn(a, b, precision=None):
    return lax.dot_general(a, b, (((0,), (0,)), ((), ())), precision=precision,
                           preferred_element_type=F32)


def _dot_hi(a, b):
    return jnp.dot(a, b, precision=HI, preferred_element_type=F32)


def _rms(x, g):
    r = lax.rsqrt(jnp.mean(x * x, axis=-1, keepdims=True) + EPS)
    return x * r * g


def _silu(x):
    return x * jax.nn.sigmoid(x)


def _proj_kernel(*refs, rope, scale, n_out):
    if rope:
        x_ref, g_ref, w_ref, wr_ref, cos_ref, sin_ref = refs[:6]
        rest = refs[6:]
    else:
        x_ref, g_ref, w_ref = refs[:3]
        rest = refs[3:]
    outs, xn_ref = rest[:n_out], rest[n_out]

    @pl.when(pl.program_id(1) == 0)
    def _():
        xn_ref[...] = _rms(x_ref[...], g_ref[...]).astype(BF16)

    xn = xn_ref[...]
    y = _dot(xn, w_ref[...])
    if rope:
        yr = _dot(xn, wr_ref[...])
        reps = y.shape[1] // LANES
        cos = jnp.concatenate([cos_ref[...]] * reps, axis=1)
        sin = jnp.concatenate([sin_ref[...]] * reps, axis=1)
        y = y * cos + yr * sin
    if scale != 1.0:
        y = y * scale
    for o in outs:
        o[...] = y.astype(o.dtype)


def _norm_proj(x, g, w, *, w_rot=None, cos=None, sin=None, scale=1.0, out_dtypes=(F32,), tm, tn=1024):
    m, d = x.shape
    n = w.shape[1]
    tn = min(tn, n)
    rope = w_rot is not None
    in_specs = [pl.BlockSpec((tm, d), lambda i, j: (i, 0)),
                pl.BlockSpec((1, d), lambda i, j: (0, 0)),
                pl.BlockSpec((d, tn), lambda i, j: (0, j))]
    args = [x, g.reshape(1, d), w]
    if rope:
        pt = cos.shape[0] // tm
        in_specs += [pl.BlockSpec((d, tn), lambda i, j: (0, j)),
                     pl.BlockSpec((tm, LANES), lambda i, j: (i % pt, 0)),
                     pl.BlockSpec((tm, LANES), lambda i, j: (i % pt, 0))]
        args += [w_rot, cos, sin]
    res = pl.pallas_call(
        functools.partial(_proj_kernel, rope=rope, scale=scale, n_out=len(out_dtypes)),
        grid=(m // tm, n // tn),
        in_specs=in_specs,
        out_specs=[pl.BlockSpec((tm, tn), lambda i, j: (i, j)) for _ in out_dtypes],
        out_shape=[jax.ShapeDtypeStruct((m, n), dt) for dt in out_dtypes],
        scratch_shapes=[pltpu.VMEM((tm, d), BF16)],
        compiler_params=_cparams("parallel", "arbitrary"),
        name="norm_proj_rope" if rope else "norm_proj",
    )(*args)
    return res


def _proj_t_kernel(*refs, rope, scale):
    if rope:
        x_ref, g_ref, wt_ref, wrt_ref, cos_ref, sin_ref, o_ref, xn_ref = refs
    else:
        x_ref, g_ref, wt_ref, o_ref, xn_ref = refs

    @pl.when(pl.program_id(1) == 0)
    def _():
        xn_ref[...] = _rms(x_ref[...], g_ref[...]).astype(BF16)

    xn = xn_ref[...]
    y = _dot_nt(wt_ref[...], xn)
    if rope:
        yr = _dot_nt(wrt_ref[...], xn)
        reps = y.shape[0] // LANES
        cos = jnp.concatenate([cos_ref[...]] * reps, axis=0)
        sin = jnp.concatenate([sin_ref[...]] * reps, axis=0)
        y = y * cos + yr * sin
    if scale != 1.0:
        y = y * scale
    o_ref[...] = y.astype(o_ref.dtype)


def _norm_proj_t(x, g, wt, *, wt_rot=None, cos_t=None, sin_t=None, scale=1.0, tm, tn=1024):
    m, d = x.shape
    n = wt.shape[0]
    tn = min(tn, n)
    rope = wt_rot is not None
    in_specs = [pl.BlockSpec((tm, d), lambda i, j: (i, 0)),
                pl.BlockSpec((1, d), lambda i, j: (0, 0)),
                pl.BlockSpec((tn, d), lambda i, j: (j, 0))]
    args = [x, g.reshape(1, d), wt]
    if rope:
        pt = cos_t.shape[1] // tm
        in_specs += [pl.BlockSpec((tn, d), lambda i, j: (j, 0)),
                     pl.BlockSpec((LANES, tm), lambda i, j: (0, i % pt)),
                     pl.BlockSpec((LANES, tm), lambda i, j: (0, i % pt))]
        args += [wt_rot, cos_t, sin_t]
    return pl.pallas_call(
        functools.partial(_proj_t_kernel, rope=rope, scale=scale),
        grid=(m // tm, n // tn),
        in_specs=in_specs,
        out_specs=pl.BlockSpec((tn, tm), lambda i, j: (j, i)),
        out_shape=jax.ShapeDtypeStruct((n, m), BF16),
        scratch_shapes=[pltpu.VMEM((tm, d), BF16)],
        compiler_params=_cparams("parallel", "arbitrary"),
        name="norm_proj_t_rope" if rope else "norm_proj_t",
    )(*args)


def _resident(shape):
    return pl.BlockSpec(shape, lambda *_: (0,) * len(shape), pipeline_mode=pl.Buffered(1))


def _ffn_kernel(*refs, per_seq, tiles_per_seq, tf, final):
    x_ref, mix_ref, wo_ref, g_ref, wu_ref, cw_ref, wd_ref = refs[:7]
    pos = 7
    st_ref = fg_ref = carry_ref = None
    if per_seq:
        st_ref = refs[pos]
        pos += 1
    if final:
        fg_ref = refs[pos]
        pos += 1
    o_ref, tail_ref, act_ref, pad_ref = refs[pos:pos + 4]
    if not per_seq:
        carry_ref = refs[pos + 4]
    i = pl.program_id(0)
    tm = x_ref.shape[0]
    f = wd_ref.shape[0]
    nf = f // tf
    x = x_ref[...] + _dot(mix_ref[...], wo_ref[...])
    xn = _rms(x, g_ref[...]).astype(BF16)

    if not per_seq:
        @pl.when((i % tiles_per_seq) == 0)
        def _():
            carry_ref[...] = jnp.zeros_like(carry_ref)

    for j in range(nf):
        cols = slice(j * tf, (j + 1) * tf)
        gate = _dot(xn, wu_ref[:, cols])
        val = _dot(xn, wu_ref[:, f + j * tf:f + (j + 1) * tf])
        cw = cw_ref[:, cols]
        pad = pad_ref.at[j % 2]
        if per_seq:
            g3 = gate.reshape(tm // SUBLANES, SUBLANES, tf)
            pad[:, :SUBLANES, :] = st_ref[:, :, cols]
            pad[:, SUBLANES:, :] = g3
            tail_ref[:, :, cols] = g3
            conv = (cw[2:3][None] * g3 + cw[1:2][None] * pad[:, SUBLANES - 1:2 * SUBLANES - 1, :]
                    + cw[0:1][None] * pad[:, SUBLANES - 2:2 * SUBLANES - 2, :])
        else:
            tail = gate[tm - SUBLANES:]
            pad[:SUBLANES, :] = carry_ref[:, cols]
            pad[SUBLANES:, :] = gate
            tail_ref[0, :, cols] = tail
            carry_ref[:, cols] = tail
            conv = (cw[2:3] * gate + cw[1:2] * pad[pl.ds(SUBLANES - 1, tm), :]
                    + cw[0:1] * pad[pl.ds(SUBLANES - 2, tm), :])
        act_ref[:, cols] = (_silu(conv).reshape(tm, tf) * val).astype(BF16)

    y = x + _dot(act_ref[...], wd_ref[...])
    if final:
        y = _rms(y, fg_ref[...])
    o_ref[...] = y


def _conv_ffn(x, mix, w_o, g, wu, cw, wd, *, state=None, final_g=None, seq_len, tm, tf=256):
    m, d = x.shape
    f = wd.shape[0]
    per_seq = state is not None
    final = final_g is not None
    groups = tm // SUBLANES if per_seq else 1
    in_specs = [pl.BlockSpec((tm, d), lambda i: (i, 0)),
                pl.BlockSpec((tm, mix.shape[1]), lambda i: (i, 0)), _resident(w_o.shape),
                _resident((1, d)), _resident((d, 2 * f)), _resident((SUBLANES, f)), _resident((f, d))]
    args = [x, mix, w_o, g.reshape(1, d), wu, cw, wd]
    if per_seq:
        in_specs.append(pl.BlockSpec((groups, SUBLANES, f), lambda i: (i, 0, 0)))
        args.append(state)
    if final:
        in_specs.append(_resident((1, d)))
        args.append(final_g.reshape(1, d))
    if per_seq:
        scratch = [pltpu.VMEM((tm, f), BF16), pltpu.VMEM((2, groups, 2 * SUBLANES, tf), F32)]
    else:
        scratch = [pltpu.VMEM((tm, f), BF16), pltpu.VMEM((2, SUBLANES + tm, tf), F32), pltpu.VMEM((SUBLANES, f), F32)]
    return pl.pallas_call(
        functools.partial(_ffn_kernel, per_seq=per_seq, tiles_per_seq=max(seq_len // tm, 1), tf=tf, final=final),
        grid=(m // tm,),
        in_specs=in_specs,
        out_specs=[pl.BlockSpec((tm, d), lambda i: (i, 0)),
                   pl.BlockSpec((groups, SUBLANES, f), lambda i: (i, 0, 0))],
        out_shape=[jax.ShapeDtypeStruct((m, d), F32),
                   jax.ShapeDtypeStruct((m // tm * groups, SUBLANES, f), F32)],
        scratch_shapes=scratch,
        compiler_params=_cparams("arbitrary"),
        name="conv_ffn",
    )(*args)


def _gdn_in_kernel(*refs, per_seq, tiles_per_seq, tn, nq, nqkv, nz, qscale):
    x_ref, g_ref, w_ref, cw_ref = refs[:4]
    pos = 4
    st_ref = carry_ref = None
    if per_seq:
        st_ref = refs[pos]
        pos += 1
    o_ref, ba_ref, tail_ref, pad_ref = refs[pos:pos + 4]
    if not per_seq:
        carry_ref = refs[pos + 4]
    i = pl.program_id(0)
    tm = x_ref.shape[0]
    xn = _rms(x_ref[...], g_ref[...]).astype(BF16)

    if not per_seq:
        @pl.when((i % tiles_per_seq) == 0)
        def _():
            carry_ref[...] = jnp.zeros_like(carry_ref)

    def l2n(y, s):
        parts = []
        for a in range(tn // LANES):
            ya = y[:, a * LANES:(a + 1) * LANES]
            r = lax.rsqrt(jnp.sum(ya * ya, axis=-1, keepdims=True) + EPS)
            parts.append(ya * (r * s) if s != 1.0 else ya * r)
        return jnp.concatenate(parts, axis=1)

    for j in range(nqkv + nz + 1):
        cols = slice(j * tn, (j + 1) * tn)
        pre = _dot(xn, w_ref[:, cols])
        if j == nqkv + nz:
            ba_ref[...] = pre[:, :2 * LANES]
        elif j >= nqkv:
            o_ref[:, cols] = pre
        else:
            cw = cw_ref[:, cols]
            pad = pad_ref.at[j % 2]
            if per_seq:
                p3 = pre.reshape(tm // SUBLANES, SUBLANES, tn)
                pad[:, :SUBLANES, :] = st_ref[:, :, cols]
                pad[:, SUBLANES:, :] = p3
                tail_ref[:, :, cols] = p3
                conv = cw[3:4][None] * p3
                for t in range(3):
                    conv = conv + cw[t:t + 1][None] * pad[:, SUBLANES - 3 + t:2 * SUBLANES - 3 + t, :]
            else:
                tail = pre[tm - SUBLANES:]
                pad[:SUBLANES, :] = carry_ref[:, cols]
                pad[SUBLANES:, :] = pre
                tail_ref[0, :, cols] = tail
                carry_ref[:, cols] = tail
                conv = cw[3:4] * pre
                for t in range(3):
                    conv = conv + cw[t:t + 1] * pad[pl.ds(SUBLANES - 3 + t, tm), :]
            y = _silu(conv).reshape(tm, tn)
            if j < nq:
                y = l2n(y, qscale)
            elif j < 2 * nq:
                y = l2n(y, 1.0)
            o_ref[:, cols] = y


def _gdn_in(x, g, w_all, cw, *, state=None, seq_len, a_qk, a_qkv, a_vd, qscale, tm, tn=512):
    m, d = x.shape
    per_seq = state is not None
    nq, nqkv, nz = a_qk // tn, a_qkv // tn, a_vd // tn
    groups = tm // SUBLANES if per_seq else 1
    in_specs = [pl.BlockSpec((tm, d), lambda i: (i, 0)),
                _resident((1, d)), _resident(w_all.shape), _resident((SUBLANES, a_qkv))]
    args = [x, g.reshape(1, d), w_all, cw]
    if per_seq:
        in_specs.append(pl.BlockSpec((groups, SUBLANES, a_qkv), lambda i: (i, 0, 0)))
        args.append(state)
    if per_seq:
        scratch = [pltpu.VMEM((2, groups, 2 * SUBLANES, tn), F32)]
    else:
        scratch = [pltpu.VMEM((2, SUBLANES + tm, tn), F32), pltpu.VMEM((SUBLANES, a_qkv), F32)]
    return pl.pallas_call(
        functools.partial(_gdn_in_kernel, per_seq=per_seq, tiles_per_seq=max(seq_len // tm, 1),
                          tn=tn, nq=nq, nqkv=nqkv, nz=nz, qscale=qscale),
        grid=(m // tm,),
        in_specs=in_specs,
        out_specs=[pl.BlockSpec((tm, a_qkv + a_vd), lambda i: (i, 0)),
                   pl.BlockSpec((tm, 2 * LANES), lambda i: (i, 0)),
                   pl.BlockSpec((groups, SUBLANES, a_qkv), lambda i: (i, 0, 0))],
        out_shape=[jax.ShapeDtypeStruct((m, a_qkv + a_vd), F32),
                   jax.ShapeDtypeStruct((m, 2 * LANES), F32),
                   jax.ShapeDtypeStruct((m // tm * groups, SUBLANES, a_qkv), F32)],
        scratch_shapes=scratch,
        compiler_params=_cparams("arbitrary"),
        name="gdn_in",
    )(*args)


def _split(a):
    hi = a.astype(BF16)
    return hi, (a - hi.astype(F32)).astype(BF16)


def _dot3(a, b, nt=False):
    f = _dot_nt if nt else _dot
    return f(a[0], b[0]) + (f(a[0], b[1]) + f(a[1], b[0]))


def _unit_lower_inverses(lms, row, col, eye):
    def same(s):
        return (row >> s) == (col >> s)

    def dot1(a, b):
        return _dot(a.astype(BF16), b.astype(BF16))

    nd = [jnp.where(same(3), -lm, 0.0) for lm in lms]
    nd2 = [dot1(a, a) for a in nd]
    nd4 = [dot1(a, a) for a in nd2]
    x = [dot1(eye + a, eye + b) for a, b in zip(nd, nd2)]
    x = [dot1(a, eye + b) for a, b in zip(x, nd4)]
    for s in (3, 4, 5):
        mask = same(s + 1) & jnp.logical_not(same(s))
        y = [dot1(a, jnp.where(mask, lm, 0.0)) for a, lm in zip(x, lms)]
        x = [a - dot1(b, a) for a, b in zip(x, y)]
    res = [eye - _dot3(_split(eye + lm), _split(a)) for lm, a in zip(lms, x)]
    return [a + dot1(a, r) for a, r in zip(x, res)]


def _gdn_chunk_kernel(*refs, heads, dk, dv, has_s0):
    q_ref, k_ref, v_ref, z_ref, ba_ref, na_ref, dt_ref, gain_ref = refs[:8]
    pos = 8
    s0_ref = None
    if has_s0:
        s0_ref = refs[pos]
        pos += 1
    o_ref, sout_ref, s_ref = refs[pos:pos + 3]
    n = pl.program_id(1)

    @pl.when(n == 0)
    def _():
        if has_s0:
            s_ref[...] = s0_ref[...]
        else:
            s_ref[...] = jnp.zeros_like(s_ref)

    bt, cr = q_ref.shape[0], q_ref.shape[1]

    def pad(a):
        if cr == CHUNK:
            return a
        return jnp.concatenate([a, jnp.zeros((CHUNK - cr, a.shape[1]), a.dtype)], axis=0)

    row = lax.broadcasted_iota(jnp.int32, (CHUNK, CHUNK), 0)
    col = lax.broadcasted_iota(jnp.int32, (CHUNK, CHUNK), 1)
    lower = row >= col
    strict = row > col
    eye = (row == col).astype(F32)
    tri = lower.astype(F32)
    gain = gain_ref[...]

    qh, kh, vh, bcol, egcol, eglcol, eglast, decay = [], [], [], [], [], [], [], []
    for b in range(bt):
        ba = ba_ref[b]
        beta = pad(jax.nn.sigmoid(ba[:, :LANES]))
        g = pad(na_ref[...] * jax.nn.softplus(ba[:, LANES:] + dt_ref[...]))
        q, k, v = pad(q_ref[b]), pad(k_ref[b]), pad(v_ref[b])
        gc = _dot_hi(tri, g)
        gct = gc.T
        glast = gc[CHUNK - 1:CHUNK]
        eg = jnp.exp(gc)
        egl = jnp.exp(glast - gc)
        egt = jnp.exp(glast)
        for h in range(heads):
            qh.append(q[:, h * dk:(h + 1) * dk])
            kh.append(k[:, h * dk:(h + 1) * dk])
            vh.append(v[:, h * dv:(h + 1) * dv])
            bcol.append(beta[:, h:h + 1])
            egcol.append(eg[:, h:h + 1])
            eglcol.append(egl[:, h:h + 1])
            eglast.append(egt[:, h:h + 1])
            decay.append(jnp.where(lower, jnp.exp(gc[:, h:h + 1] - gct[h:h + 1, :]), 0.0))
    ps = range(bt * heads)
    kb = [kh[i] * bcol[i] for i in ps]
    kk = [_dot3(_split(kb[i]), _split(kh[i]), nt=True) for i in ps]
    lm = [jnp.where(strict, kk[i] * decay[i], 0.0) for i in ps]
    t = _unit_lower_inverses(lm, row, col, eye)
    rhs = [jnp.concatenate([vh[i] * bcol[i], kb[i] * egcol[i]], axis=1) for i in ps]
    sol = [_dot3(_split(t[i]), _split(rhs[i])) for i in ps]
    kh16 = [kh[i].astype(BF16) for i in ps]
    attn = [(_dot_nt(qh[i].astype(BF16), kh16[i]) * decay[i]).astype(BF16) for i in ps]
    qe16 = [(qh[i] * egcol[i]).astype(BF16) for i in ps]
    kd16 = [(kh[i] * eglcol[i]).astype(BF16) for i in ps]
    sh = [s_ref[i // heads, i % heads] for i in ps]
    sh16 = [a.astype(BF16) for a in sh]
    v_new = [sol[i][:, :dv] - _dot(sol[i][:, dv:].astype(BF16), sh16[i]) for i in ps]
    vn16 = [a.astype(BF16) for a in v_new]
    o = [_dot(qe16[i], sh16[i]) + _dot(attn[i], vn16[i]) for i in ps]
    for i in ps:
        s_ref[i // heads, i % heads] = sh[i] * eglast[i] + _dot_tn(kd16[i], vn16[i])
    for i in ps:
        b, h = i // heads, i % heads
        oh = o[i][:cr]
        zh = z_ref[b, :, h * dv:(h + 1) * dv]
        on = oh * lax.rsqrt(jnp.mean(oh * oh, axis=-1, keepdims=True) + EPS) * gain * _silu(zh)
        o_ref[b, :, h * dv:(h + 1) * dv] = on.astype(o_ref.dtype)

    @pl.when(n == pl.num_programs(1) - 1)
    def _():
        sout_ref[...] = s_ref[...]


def _gdn_chunk(qkvz, ba, neg_a, dt_bias, gain, *, s0, batch, seq_len, heads, dk, dv, bt=4):
    m = qkvz.shape[0]
    cr = min(CHUNK, seq_len)
    nc = seq_len // cr
    hd = heads * dk
    bt = math.gcd(batch, bt)
    has_s0 = s0 is not None
    qkvz = qkvz.reshape(batch, seq_len, -1)
    ba = ba.reshape(batch, seq_len, -1)

    def blk(c):
        return pl.BlockSpec((bt, cr, hd), lambda b, n: (b, n, c))

    in_specs = [blk(0), blk(1), blk(2), blk(3),
                pl.BlockSpec((bt, cr, 2 * LANES), lambda b, n: (b, n, 0)),
                pl.BlockSpec((1, LANES), lambda b, n: (0, 0)),
                pl.BlockSpec((1, LANES), lambda b, n: (0, 0)),
                pl.BlockSpec((1, dv), lambda b, n: (0, 0))]
    args = [qkvz, qkvz, qkvz, qkvz, ba, neg_a, dt_bias, gain.reshape(1, dv)]
    if has_s0:
        in_specs.append(pl.BlockSpec((bt, heads, dk, dv), lambda b, n: (b, 0, 0, 0)))
        args.append(s0)
    o, s_fin = pl.pallas_call(
        functools.partial(_gdn_chunk_kernel, heads=heads, dk=dk, dv=dv, has_s0=has_s0),
        grid=(batch // bt, nc),
        in_specs=in_specs,
        out_specs=[pl.BlockSpec((bt, cr, hd), lambda b, n: (b, n, 0)),
                   pl.BlockSpec((bt, heads, dk, dv), lambda b, n: (b, 0, 0, 0))],
        out_shape=[jax.ShapeDtypeStruct((batch, seq_len, hd), BF16),
                   jax.ShapeDtypeStruct((batch, heads, dk, dv), F32)],
        scratch_shapes=[pltpu.VMEM((bt, heads, dk, dv), F32)],
        compiler_params=_cparams("parallel", "arbitrary"),
        name="gdn_chunk",
    )(*args)
    return o.reshape(m, hd), s_fin


def _lambda(lv_ref, lam_init):
    lv = lv_ref[...]
    a = jnp.sum(lv[0:1] * lv[1:2], axis=-1, keepdims=True)
    b = jnp.sum(lv[2:3] * lv[3:4], axis=-1, keepdims=True)
    return jnp.exp(a) - jnp.exp(b) + lam_init


def _flash_kernel(qt_ref, kt_ref, q_ref, k_ref, v_ref, lv_ref, sub_ref, o_ref, qs_ref, m_ref, acc_ref,
                  *, dh, lam_init):
    step = pl.program_id(2)
    qi = qt_ref[step]
    ki = kt_ref[step]
    tq = q_ref.shape[1]
    tk = k_ref.shape[0]
    vd = v_ref.shape[0]

    @pl.when(ki == 0)
    def _():
        q = q_ref[...]
        feat = lax.broadcasted_iota(jnp.int32, q.shape, 0)
        zero = jnp.zeros_like(q)
        qs_ref[:, :tq] = jnp.where(feat < dh, q, zero)
        qs_ref[:, tq:] = jnp.where(feat >= dh, q, zero)
        m_ref[...] = jnp.full_like(m_ref, -jnp.inf)
        acc_ref[...] = jnp.zeros_like(acc_ref)

    def accumulate(masked):
        s = _dot(k_ref[...], qs_ref[...])
        if masked:
            kpos = ki * tk + lax.broadcasted_iota(jnp.int32, s.shape, 0)
            qpos = qi * tq + (lax.broadcasted_iota(jnp.int32, s.shape, 1) & (tq - 1))
            s = jnp.where(kpos <= qpos, s, -jnp.inf)
        m_old = m_ref[...]
        m_new = jnp.maximum(m_old, jnp.max(s, axis=0, keepdims=True))
        alpha = jnp.exp2(m_old - m_new)
        p = jnp.exp2(s - m_new).astype(BF16)
        v_ones = jnp.concatenate([v_ref[...], jnp.ones((acc_ref.shape[0] - vd, tk), BF16)], axis=0)
        acc_ref[...] = alpha * acc_ref[...] + _dot(v_ones, p)
        m_ref[...] = m_new

    below_diagonal = (ki + 1) * tk - 1 <= qi * tq

    @pl.when(below_diagonal)
    def _():
        accumulate(False)

    @pl.when(jnp.logical_not(below_diagonal))
    def _():
        accumulate(True)

    @pl.when((ki + 1) * tk == (qi + 1) * tq)
    def _():
        lam = _lambda(lv_ref, lam_init)
        a = acc_ref[:vd] / acc_ref[vd:vd + 1]
        o = a[:, :tq] - lam * a[:, tq:]
        r = lax.rsqrt(jnp.mean(o * o, axis=0, keepdims=True) + EPS)
        o = o * r * (sub_ref[...] * (1.0 - lam_init))
        o_ref[...] = o.T.astype(o_ref.dtype)


def _flash_prompt(q_t, k, v_t, lam_vecs, subln, *, batch, seq_len, dh, vd, lam_init, tq=1024, tk=1024):
    m, width = k.shape
    pairs = width // (2 * dh)
    tq = min(tq, seq_len)
    tk = min(tk, tq)
    nq, nk = seq_len // tq, seq_len // tk
    assert tq & (tq - 1) == 0 and tq % tk == 0
    steps = [(i, j) for i in range(nq) for j in range((i + 1) * tq // tk)]
    q_tab = jnp.asarray([s[0] for s in steps], jnp.int32)
    k_tab = jnp.asarray([s[1] for s in steps], jnp.int32)
    grid_spec = pltpu.PrefetchScalarGridSpec(
        num_scalar_prefetch=2,
        grid=(batch, pairs, len(steps)),
        in_specs=[pl.BlockSpec((2 * dh, tq), lambda b, h, s, qt, kt: (h, b * nq + qt[s])),
                  pl.BlockSpec((tk, 2 * dh), lambda b, h, s, qt, kt: (b * nk + kt[s], h)),
                  pl.BlockSpec((vd, tk), lambda b, h, s, qt, kt: (h, b * nk + kt[s])),
                  pl.BlockSpec(lam_vecs.shape, lambda b, h, s, qt, kt: (0, 0)),
                  pl.BlockSpec((vd, 1), lambda b, h, s, qt, kt: (0, 0))],
        out_specs=pl.BlockSpec((tq, vd), lambda b, h, s, qt, kt: (b * nq + qt[s], h)),
        scratch_shapes=[pltpu.VMEM((2 * dh, 2 * tq), BF16),
                        pltpu.VMEM((1, 2 * tq), F32),
                        pltpu.VMEM((vd + 2 * SUBLANES, 2 * tq), F32)])
    return pl.pallas_call(
        functools.partial(_flash_kernel, dh=dh, lam_init=lam_init),
        grid_spec=grid_spec,
        out_shape=jax.ShapeDtypeStruct((m, pairs * vd), BF16),
        compiler_params=_cparams("parallel", "parallel", "arbitrary"),
        name="diff_flash",
    )(q_tab, k_tab, q_t, k, v_t, lam_vecs, subln.reshape(vd, 1))


def _paged_kernel(*refs, nh, dh, vd, lam_init, group):
    q_ref, kn_ref, vn_ref = refs[1:4]
    kc_refs = refs[4:4 + group]
    vc_refs = refs[4 + group:4 + 2 * group]
    lv_ref, sub_ref, o_ref, qbd_ref, m_ref, l_ref, acc_ref = refs[4 + 2 * group:]
    p = pl.program_id(1)
    t = q_ref.shape[1]
    width = q_ref.shape[2]
    rows = nh * t
    page = kc_refs[0].shape[2]
    nvh = width // vd

    def accum(kt16, v16, mask):
        s = _dot(qbd_ref[...], kt16)
        if mask is not None:
            s = jnp.where(mask, s, -jnp.inf)
        m_old = m_ref[...]
        m_new = jnp.maximum(m_old, jnp.max(s, axis=-1, keepdims=True))
        alpha = jnp.exp(m_old - m_new)
        pr = jnp.exp(s - m_new)
        l_ref[...] = alpha * l_ref[...] + jnp.sum(pr, axis=-1, keepdims=True)
        acc_ref[...] = alpha * acc_ref[...] + _dot(pr.astype(BF16), v16)
        m_ref[...] = m_new

    @pl.when(p == 0)
    def _():
        q = q_ref[0].astype(F32)
        q3 = jnp.broadcast_to(q[None], (nh, t, width))
        hd = lax.broadcasted_iota(jnp.int32, (nh, t, width), 0)
        ln = lax.broadcasted_iota(jnp.int32, (nh, t, width), 2)
        qbd = jnp.where((ln >= hd * dh) & (ln < (hd + 1) * dh), q3, 0.0)
        qbd_ref[...] = qbd.reshape(rows, width).astype(BF16)
        m_ref[...] = jnp.full_like(m_ref, -jnp.inf)
        l_ref[...] = jnp.zeros_like(l_ref)
        acc_ref[...] = jnp.zeros_like(acc_ref)
        zpad = jnp.zeros((page - t, width), F32)
        kt16 = jnp.concatenate([kn_ref[0], zpad], axis=0).T.astype(BF16)
        v16 = jnp.concatenate([vn_ref[0], zpad], axis=0).astype(BF16)
        r = lax.broadcasted_iota(jnp.int32, (rows, page), 0)
        c = lax.broadcasted_iota(jnp.int32, (rows, page), 1)
        accum(kt16, v16, c <= (r & (t - 1)))

    @pl.when(p > 0)
    def _():
        v = jnp.concatenate(
            [jnp.concatenate([vc[0, pl.ds(h, page, stride=nvh), :] for h in range(nvh)], axis=1).astype(BF16)
             for vc in vc_refs], axis=0)
        kt = jnp.concatenate([kc[0].astype(BF16) for kc in kc_refs], axis=1)
        accum(kt, v, None)

    @pl.when(p == pl.num_programs(1) - 1)
    def _():
        lam = _lambda(lv_ref, lam_init)
        r = lax.broadcasted_iota(jnp.int32, (rows, 1), 0)
        odd = ((r // t) & 1) == 1
        wgt = jnp.where(odd, -lam, 1.0) / l_ref[...]
        a3 = (acc_ref[...] * wgt).reshape(nh, t, width)
        hd = lax.broadcasted_iota(jnp.int32, (nh, t, width), 0)
        ln = lax.broadcasted_iota(jnp.int32, (nh, t, width), 2)
        pair = hd >> 1
        o = jnp.sum(jnp.where((ln >= pair * vd) & (ln < (pair + 1) * vd), a3, 0.0), axis=0)
        sub = sub_ref[...]
        for h in range(width // vd):
            oh = o[:, h * vd:(h + 1) * vd]
            o_ref[0, :, h * vd:(h + 1) * vd] = (_rms(oh, sub) * (1.0 - lam_init)).astype(o_ref.dtype)


def _paged_attention(q, k_new, v_new, cache_k, cache_v, page_table, lam_vecs, subln, *, dh, vd, lam_init, group=8):
    b, t, width = q.shape
    npg = page_table.shape[1]
    page = cache_k.shape[2]
    nh = width // dh
    nvh = width // vd
    assert t & (t - 1) == 0 and t <= page and nh == 2 * nvh and cache_v.shape[1] == page * nvh
    group = math.gcd(npg, group)

    def pidx(i):
        return lambda bb, p, pt: (pt[bb * npg + jnp.maximum(p - 1, 0) * group + i], 0, 0)

    grid_spec = pltpu.PrefetchScalarGridSpec(
        num_scalar_prefetch=1,
        grid=(b, npg // group + 1),
        in_specs=[pl.BlockSpec((1, t, width), lambda bb, p, pt: (bb, 0, 0)),
                  pl.BlockSpec((1, t, width), lambda bb, p, pt: (bb, 0, 0)),
                  pl.BlockSpec((1, t, width), lambda bb, p, pt: (bb, 0, 0))]
        + [pl.BlockSpec((1, width, page), pidx(i)) for i in range(group)]
        + [pl.BlockSpec((1, page * nvh, vd), pidx(i)) for i in range(group)]
        + [pl.BlockSpec(lam_vecs.shape, lambda bb, p, pt: (0, 0)),
           pl.BlockSpec((1, vd), lambda bb, p, pt: (0, 0))],
        out_specs=pl.BlockSpec((1, t, width), lambda bb, p, pt: (bb, 0, 0)),
        scratch_shapes=[pltpu.VMEM((nh * t, width), BF16),
                        pltpu.VMEM((nh * t, 1), F32),
                        pltpu.VMEM((nh * t, 1), F32),
                        pltpu.VMEM((nh * t, width), F32)])
    return pl.pallas_call(
        functools.partial(_paged_kernel, nh=nh, dh=dh, vd=vd, lam_init=lam_init, group=group),
        grid_spec=grid_spec,
        out_shape=jax.ShapeDtypeStruct((b, t, width), BF16),
        compiler_params=_cparams("parallel", "arbitrary"),
        name="diff_paged",
    )(page_table.reshape(-1), q, k_new, v_new, *([cache_k] * group), *([cache_v] * group),
      lam_vecs, subln.reshape(1, vd))


def _rot_weight(w, dh):
    k, n = w.shape
    w4 = w.reshape(k, n // dh, 2, dh // 2)
    return jnp.stack([-w4[:, :, 1], w4[:, :, 0]], axis=2).reshape(k, n)


def _rope_tables(pos, dh):
    half = dh // 2
    inv = 1.0 / (ROPE_THETA ** (jnp.arange(half, dtype=F32) / half))
    ang = pos.astype(F32)[:, None] * inv[None, :]
    reps = LANES // half
    return jnp.tile(jnp.cos(ang), (1, reps)), jnp.tile(jnp.sin(ang), (1, reps))


def _pad_rows(a, rows, front):
    pad = [(0, 0)] * a.ndim
    pad[-2] = (rows - a.shape[-2], 0) if front else (0, rows - a.shape[-2])
    return jnp.pad(a, pad)


def _prep_weights(p):
    n_a, d, a_in = p["a_w_in"].shape
    heads = p["a_A_log"].shape[1]
    dv = p["a_o_gain"].shape[1]
    a_vd = heads * dv
    a_qkv = a_in - a_vd - 2 * heads
    dh = p["b_lambda"].shape[-1]
    w = {}
    w_in = p["a_w_in"]
    tn = 512
    zpad = jnp.zeros((n_a, d, LANES - heads), F32)
    w["a_w_all"] = jnp.concatenate(
        [w_in[:, :, :a_qkv + a_vd], w_in[:, :, a_qkv + a_vd:a_qkv + a_vd + heads], zpad,
         w_in[:, :, a_qkv + a_vd + heads:], zpad, jnp.zeros((n_a, d, tn - 2 * LANES), F32)], axis=2).astype(BF16)
    w["a_cw"] = _pad_rows(p["a_conv_w"], SUBLANES, front=False)
    hp = jnp.zeros((n_a, LANES - heads), F32)
    w["a_neg_a"] = jnp.concatenate([-jnp.exp(p["a_A_log"].astype(F32)), hp], axis=1)[:, None, :]
    w["a_dt"] = jnp.concatenate([p["a_dt_bias"].astype(F32), hp], axis=1)[:, None, :]
    w["a_w_out"] = p["a_w_out"].astype(BF16)
    kq = p["w_kv"].shape[1] - (p["b_w_out"].shape[1])
    w["w_k"] = p["w_kv"][:, :kq].astype(BF16)
    w["w_k_rot"] = _rot_weight(p["w_kv"][:, :kq], dh).astype(BF16)
    w["w_v"] = p["w_kv"][:, kq:].astype(BF16)
    w["b_w_q"] = p["b_w_q"].astype(BF16)
    w["b_w_q_rot"] = jnp.stack([_rot_weight(p["b_w_q"][j], dh) for j in range(p["b_w_q"].shape[0])]).astype(BF16)
    w["b_w_out"] = p["b_w_out"].astype(BF16)
    w["w_v_t"] = w["w_v"].T
    w["b_w_q_t"] = jnp.swapaxes(w["b_w_q"], 1, 2)
    w["b_w_q_rot_t"] = jnp.swapaxes(w["b_w_q_rot"], 1, 2)
    f = p["f_w_down"].shape[1]
    w["f_wu"] = p["f_w_up"].astype(BF16)
    w["f_cw"] = _pad_rows(p["f_conv_w"], SUBLANES, front=False)
    w["f_wd"] = p["f_w_down"].astype(BF16)
    return w


def _trunk(x, pos, p, w, *, delta0, dconv0, fconv0, cache_k, cache_v, page_table):
    b, l, d = x.shape
    m = b * l
    sample = page_table is not None
    depth = p["f_norm"].shape[0]
    n_a = p["a_norm"].shape[0]
    heads = p["a_A_log"].shape[1]
    dv = p["a_o_gain"].shape[1]
    dk = (p["a_w_in"].shape[2] - 2 * heads - 2 * heads * dv) // (2 * heads)
    a_vd = heads * dv
    a_qk = heads * dk
    a_qkv = 2 * a_qk + a_vd
    dh = p["b_lambda"].shape[-1]
    vd = p["b_subln"].shape[-1]
    f = p["f_w_down"].shape[1]
    tm = m if sample else min(512, l)
    assert m % tm == 0 and (sample or l % tm == 0)

    h = x.reshape(m, d)
    cos, sin = _rope_tables(pos, dh)
    if sample:
        cos, sin = jnp.tile(cos, (b, 1)), jnp.tile(sin, (b, 1))

    def tails_to_state(tails, rows):
        if sample:
            return tails[:, SUBLANES - rows:, :]
        per = tails.shape[0] // b
        return tails.reshape(b, per, SUBLANES, -1)[:, per - 1, SUBLANES - rows:, :]

    deltas, dconvs, fconvs = [], [], []
    k_new = v_new = k16 = v16_t = None
    for layer in range(depth):
        if layer < n_a:
            st = _pad_rows(dconv0[layer], SUBLANES, front=True) if sample else None
            qkvz, ba, tails = _gdn_in(h, p["a_norm"][layer], w["a_w_all"][layer], w["a_cw"][layer], state=st,
                                      seq_len=l, a_qk=a_qk, a_qkv=a_qkv, a_vd=a_vd, qscale=dk ** -0.5,
                                      tm=tm if sample else min(tm, 256))
            dconvs.append(tails_to_state(tails, p["a_conv_w"].shape[1] - 1))
            o, s_fin = _gdn_chunk(qkvz, ba, w["a_neg_a"][layer], w["a_dt"][layer], p["a_o_gain"][layer],
                                  s0=delta0[layer] if sample else None, batch=b, seq_len=l,
                                  heads=heads, dk=dk, dv=dv)
            deltas.append(s_fin)
            w_o = w["a_w_out"][layer]
        else:
            if layer == n_a:
                k_new, k16 = _norm_proj(h, p["kv_norm"], w["w_k"], w_rot=w["w_k_rot"], cos=cos, sin=sin,
                                        out_dtypes=(F32, BF16), tm=tm)
                (v_new,) = _norm_proj(h, p["kv_norm"], w["w_v"], out_dtypes=(F32,), tm=tm)
                if not sample:
                    v16_t = _norm_proj_t(h, p["kv_norm"], w["w_v_t"], tm=tm)
            j = layer - n_a
            lam_init = 0.8 - 0.6 * math.exp(-0.3 * layer)
            if sample:
                (q16,) = _norm_proj(h, p["b_norm"][j], w["b_w_q"][j], w_rot=w["b_w_q_rot"][j], cos=cos, sin=sin,
                                    scale=dh ** -0.5, out_dtypes=(BF16,), tm=tm)
                width = q16.shape[1]
                o = _paged_attention(q16.reshape(b, l, width), k_new.reshape(b, l, width), v_new.reshape(b, l, width),
                                     cache_k, cache_v, page_table, p["b_lambda"][j], p["b_subln"][j],
                                     dh=dh, vd=vd, lam_init=lam_init).reshape(m, width)
            else:
                q16_t = _norm_proj_t(h, p["b_norm"][j], w["b_w_q_t"][j], wt_rot=w["b_w_q_rot_t"][j],
                                     cos_t=cos.T, sin_t=sin.T, scale=dh ** -0.5 * math.log2(math.e), tm=tm)
                o = _flash_prompt(q16_t, k16, v16_t, p["b_lambda"][j], p["b_subln"][j], batch=b, seq_len=l,
                                  dh=dh, vd=vd, lam_init=lam_init)
            w_o = w["b_w_out"][j]
        st = _pad_rows(fconv0[layer], SUBLANES, front=True) if sample else None
        h, tails = _conv_ffn(h, o, w_o, p["f_norm"][layer], w["f_wu"][layer], w["f_cw"][layer],
                             w["f_wd"][layer], state=st, final_g=p["final_norm"] if layer == depth - 1 else None,
                             seq_len=l, tm=tm)
        fconvs.append(tails_to_state(tails, p["f_conv_w"].shape[1] - 1))
    nkh = k_new.shape[1] // dh
    return (h.reshape(b, l, d), jnp.stack(deltas), jnp.stack(dconvs), jnp.stack(fconvs),
            k_new.reshape(b, l, nkh, dh), v_new.reshape(b, l, v_new.shape[1] // vd, vd))


def kernel(x_prompt, x_sample, state_delta, state_dconv, state_fconv, cache_k, cache_v, page_table, a_norm, a_w_in, a_conv_w, a_A_log, a_dt_bias, a_o_gain, a_w_out, kv_norm, w_kv, b_norm, b_w_q, b_lambda, b_subln, b_w_out, f_norm, f_w_up, f_conv_w, f_w_down, final_norm):
    p = dict(a_norm=a_norm, a_w_in=a_w_in, a_conv_w=a_conv_w, a_A_log=a_A_log, a_dt_bias=a_dt_bias,
             a_o_gain=a_o_gain, a_w_out=a_w_out, kv_norm=kv_norm, w_kv=w_kv, b_norm=b_norm, b_w_q=b_w_q,
             b_lambda=b_lambda, b_subln=b_subln, b_w_out=b_w_out, f_norm=f_norm, f_w_up=f_w_up,
             f_conv_w=f_conv_w, f_w_down=f_w_down, final_norm=final_norm)
    w = _prep_weights(p)
    lp = x_prompt.shape[1]
    prompt = _trunk(x_prompt, jnp.arange(lp, dtype=jnp.int32), p, w, delta0=None, dconv0=None, fconv0=None,
                    cache_k=None, cache_v=None, page_table=None)
    ls = x_sample.shape[1]
    past_len = page_table.shape[1] * cache_k.shape[1]
    pool, page = cache_k.shape[:2]
    sample = _trunk(x_sample, past_len + jnp.arange(ls, dtype=jnp.int32), p, w, delta0=state_delta,
                    dconv0=state_dconv, fconv0=state_fconv,
                    cache_k=cache_k.transpose(0, 2, 3, 1).reshape(pool, -1, page),
                    cache_v=cache_v.reshape(pool, page * cache_v.shape[2], cache_v.shape[3]),
                    page_table=page_table)
    return (prompt[0], sample[0]) + prompt[1:] + sample[1:]
```

```python
import functools
import math

import jax
import jax.numpy as jnp
from jax import lax
from jax.experimental import pallas as pl
from jax.experimental.pallas import tpu as pltpu

F32 = jnp.float32
BF16 = jnp.bfloat16
EPS = 1e-6
ROPE_THETA = 10000.0
LANES = 128
SUBLANES = 8
CHUNK = 64
VMEM_LIMIT = 48 * 1024 * 1024
HI = lax.Precision.HIGHEST


def _cparams(*sem):
    return pltpu.CompilerParams(dimension_semantics=sem, vmem_limit_bytes=VMEM_LIMIT)


def _dot(a, b):
    return jnp.dot(a, b, preferred_element_type=F32)


def _dot_nt(a, b, precision=None):
    return lax.dot_general(a, b, (((1,), (1,)), ((), ())), precision=precision,
                           preferred_element_type=F32)


def _dot_tn(a, b, precision=None):
    return lax.dot_general(a, b, (((0,), (0,)), ((), ())), precision=precision,
                           preferred_element_type=F32)


def _dot_hi(a, b):
    return jnp.dot(a, b, precision=HI, preferred_element_type=F32)


def _rms(x, g):
    r = lax.rsqrt(jnp.mean(x * x, axis=-1, keepdims=True) + EPS)
    return x * r * g


def _silu(x):
    return x * jax.nn.sigmoid(x)


def _proj_kernel(*refs, rope, scale, n_out):
    if rope:
        x_ref, g_ref, w_ref, wr_ref, cos_ref, sin_ref = refs[:6]
        rest = refs[6:]
    else:
        x_ref, g_ref, w_ref = refs[:3]
        rest = refs[3:]
    outs, xn_ref = rest[:n_out], rest[n_out]

    @pl.when(pl.program_id(1) == 0)
    def _():
        xn_ref[...] = _rms(x_ref[...], g_ref[...]).astype(BF16)

    xn = xn_ref[...]
    y = _dot(xn, w_ref[...])
    if rope:
        yr = _dot(xn, wr_ref[...])
        reps = y.shape[1] // LANES
        cos = jnp.concatenate([cos_ref[...]] * reps, axis=1)
        sin = jnp.concatenate([sin_ref[...]] * reps, axis=1)
        y = y * cos + yr * sin
    if scale != 1.0:
        y = y * scale
    for o in outs:
        o[...] = y.astype(o.dtype)


def _norm_proj(x, g, w, *, w_rot=None, cos=None, sin=None, scale=1.0, out_dtypes=(F32,), tm, tn=1024):
    m, d = x.shape
    n = w.shape[1]
    tn = min(tn, n)
    rope = w_rot is not None
    in_specs = [pl.BlockSpec((tm, d), lambda i, j: (i, 0)),
                pl.BlockSpec((1, d), lambda i, j: (0, 0)),
                pl.BlockSpec((d, tn), lambda i, j: (0, j))]
    args = [x, g.reshape(1, d), w]
    if rope:
        pt = cos.shape[0] // tm
        in_specs += [pl.BlockSpec((d, tn), lambda i, j: (0, j)),
                     pl.BlockSpec((tm, LANES), lambda i, j: (i % pt, 0)),
                     pl.BlockSpec((tm, LANES), lambda i, j: (i % pt, 0))]
        args += [w_rot, cos, sin]
    res = pl.pallas_call(
        functools.partial(_proj_kernel, rope=rope, scale=scale, n_out=len(out_dtypes)),
        grid=(m // tm, n // tn),
        in_specs=in_specs,
        out_specs=[pl.BlockSpec((tm, tn), lambda i, j: (i, j)) for _ in out_dtypes],
        out_shape=[jax.ShapeDtypeStruct((m, n), dt) for dt in out_dtypes],
        scratch_shapes=[pltpu.VMEM((tm, d), BF16)],
        compiler_params=_cparams("parallel", "arbitrary"),
        name="norm_proj_rope" if rope else "norm_proj",
    )(*args)
    return res


def _proj_t_kernel(*refs, rope, scale):
    if rope:
        x_ref, g_ref, wt_ref, wrt_ref, cos_ref, sin_ref, o_ref, xn_ref = refs
    else:
        x_ref, g_ref, wt_ref, o_ref, xn_ref = refs

    @pl.when(pl.program_id(1) == 0)
    def _():
        xn_ref[...] = _rms(x_ref[...], g_ref[...]).astype(BF16)

    xn = xn_ref[...]
    y = _dot_nt(wt_ref[...], xn)
    if rope:
        yr = _dot_nt(wrt_ref[...], xn)
        reps = y.shape[0] // LANES
        cos = jnp.concatenate([cos_ref[...]] * reps, axis=0)
        sin = jnp.concatenate([sin_ref[...]] * reps, axis=0)
        y = y * cos + yr * sin
    if scale != 1.0:
        y = y * scale
    o_ref[...] = y.astype(o_ref.dtype)


def _norm_proj_t(x, g, wt, *, wt_rot=None, cos_t=None, sin_t=None, scale=1.0, tm, tn=1024):
    m, d = x.shape
    n = wt.shape[0]
    tn = min(tn, n)
    rope = wt_rot is not None
    in_specs = [pl.BlockSpec((tm, d), lambda i, j: (i, 0)),
                pl.BlockSpec((1, d), lambda i, j: (0, 0)),
                pl.BlockSpec((tn, d), lambda i, j: (j, 0))]
    args = [x, g.reshape(1, d), wt]
    if rope:
        pt = cos_t.shape[1] // tm
        in_specs += [pl.BlockSpec((tn, d), lambda i, j: (j, 0)),
                     pl.BlockSpec((LANES, tm), lambda i, j: (0, i % pt)),
                     pl.BlockSpec((LANES, tm), lambda i, j: (0, i % pt))]
        args += [wt_rot, cos_t, sin_t]
    return pl.pallas_call(
        functools.partial(_proj_t_kernel, rope=rope, scale=scale),
        grid=(m // tm, n // tn),
        in_specs=in_specs,
        out_specs=pl.BlockSpec((tn, tm), lambda i, j: (j, i)),
        out_shape=jax.ShapeDtypeStruct((n, m), BF16),
        scratch_shapes=[pltpu.VMEM((tm, d), BF16)],
        compiler_params=_cparams("parallel", "arbitrary"),
        name="norm_proj_t_rope" if rope else "norm_proj_t",
    )(*args)


def _resident(shape):
    return pl.BlockSpec(shape, lambda *_: (0,) * len(shape), pipeline_mode=pl.Buffered(1))


def _ffn_kernel(*refs, per_seq, tiles_per_seq, tf, final):
    x_ref, mix_ref, wo_ref, g_ref, wu_ref, cw_ref, wd_ref = refs[:7]
    pos = 7
    st_ref = fg_ref = carry_ref = None
    if per_seq:
        st_ref = refs[pos]
        pos += 1
    if final:
        fg_ref = refs[pos]
        pos += 1
    o_ref, tail_ref, act_ref, pad_ref = refs[pos:pos + 4]
    if not per_seq:
        carry_ref = refs[pos + 4]
    i = pl.program_id(0)
    tm = x_ref.shape[0]
    f = wd_ref.shape[0]
    nf = f // tf
    x = x_ref[...] + _dot(mix_ref[...], wo_ref[...])
    xn = _rms(x, g_ref[...]).astype(BF16)

    if not per_seq:
        @pl.when((i % tiles_per_seq) == 0)
        def _():
            carry_ref[...] = jnp.zeros_like(carry_ref)

    for j in range(nf):
        cols = slice(j * tf, (j + 1) * tf)
        gate = _dot(xn, wu_ref[:, cols])
        val = _dot(xn, wu_ref[:, f + j * tf:f + (j + 1) * tf])
        cw = cw_ref[:, cols]
        pad = pad_ref.at[j % 2]
        if per_seq:
            g3 = gate.reshape(tm // SUBLANES, SUBLANES, tf)
            pad[:, :SUBLANES, :] = st_ref[:, :, cols]
            pad[:, SUBLANES:, :] = g3
            tail_ref[:, :, cols] = g3
            conv = (cw[2:3][None] * g3 + cw[1:2][None] * pad[:, SUBLANES - 1:2 * SUBLANES - 1, :]
                    + cw[0:1][None] * pad[:, SUBLANES - 2:2 * SUBLANES - 2, :])
        else:
            tail = gate[tm - SUBLANES:]
            pad[:SUBLANES, :] = carry_ref[:, cols]
            pad[SUBLANES:, :] = gate
            tail_ref[0, :, cols] = tail
            carry_ref[:, cols] = tail
            conv = (cw[2:3] * gate + cw[1:2] * pad[pl.ds(SUBLANES - 1, tm), :]
                    + cw[0:1] * pad[pl.ds(SUBLANES - 2, tm), :])
        act_ref[:, cols] = (_silu(conv).reshape(tm, tf) * val).astype(BF16)

    y = x + _dot(act_ref[...], wd_ref[...])
    if final:
        y = _rms(y, fg_ref[...])
    o_ref[...] = y


def _conv_ffn(x, mix, w_o, g, wu, cw, wd, *, state=None, final_g=None, seq_len, tm, tf=256):
    m, d = x.shape
    f = wd.shape[0]
    per_seq = state is not None
    final = final_g is not None
    groups = tm // SUBLANES if per_seq else 1
    in_specs = [pl.BlockSpec((tm, d), lambda i: (i, 0)),
                pl.BlockSpec((tm, mix.shape[1]), lambda i: (i, 0)), _resident(w_o.shape),
                _resident((1, d)), _resident((d, 2 * f)), _resident((SUBLANES, f)), _resident((f, d))]
    args = [x, mix, w_o, g.reshape(1, d), wu, cw, wd]
    if per_seq:
        in_specs.append(pl.BlockSpec((groups, SUBLANES, f), lambda i: (i, 0, 0)))
        args.append(state)
    if final:
        in_specs.append(_resident((1, d)))
        args.append(final_g.reshape(1, d))
    if per_seq:
        scratch = [pltpu.VMEM((tm, f), BF16), pltpu.VMEM((2, groups, 2 * SUBLANES, tf), F32)]
    else:
        scratch = [pltpu.VMEM((tm, f), BF16), pltpu.VMEM((2, SUBLANES + tm, tf), F32), pltpu.VMEM((SUBLANES, f), F32)]
    return pl.pallas_call(
        functools.partial(_ffn_kernel, per_seq=per_seq, tiles_per_seq=max(seq_len // tm, 1), tf=tf, final=final),
        grid=(m // tm,),
        in_specs=in_specs,
        out_specs=[pl.BlockSpec((tm, d), lambda i: (i, 0)),
                   pl.BlockSpec((groups, SUBLANES, f), lambda i: (i, 0, 0))],
        out_shape=[jax.ShapeDtypeStruct((m, d), F32),
                   jax.ShapeDtypeStruct((m // tm * groups, SUBLANES, f), F32)],
        scratch_shapes=scratch,
        compiler_params=_cparams("arbitrary"),
        name="conv_ffn",
    )(*args)


def _gdn_in_kernel(*refs, per_seq, tiles_per_seq, tn, nq, nqkv, nz, qscale):
    x_ref, g_ref, w_ref, cw_ref = refs[:4]
    pos = 4
    st_ref = carry_ref = None
    if per_seq:
        st_ref = refs[pos]
        pos += 1
    o_ref, ba_ref, tail_ref, pad_ref = refs[pos:pos + 4]
    if not per_seq:
        carry_ref = refs[pos + 4]
    i = pl.program_id(0)
    tm = x_ref.shape[0]
    xn = _rms(x_ref[...], g_ref[...]).astype(BF16)

    if not per_seq:
        @pl.when((i % tiles_per_seq) == 0)
        def _():
            carry_ref[...] = jnp.zeros_like(carry_ref)

    def l2n(y, s):
        parts = []
        for a in range(tn // LANES):
            ya = y[:, a * LANES:(a + 1) * LANES]
            r = lax.rsqrt(jnp.sum(ya * ya, axis=-1, keepdims=True) + EPS)
            parts.append(ya * (r * s) if s != 1.0 else ya * r)
        return jnp.concatenate(parts, axis=1)

    for j in range(nqkv + nz + 1):
        cols = slice(j * tn, (j + 1) * tn)
        pre = _dot(xn, w_ref[:, cols])
        if j == nqkv + nz:
            ba_ref[...] = pre[:, :2 * LANES]
        elif j >= nqkv:
            o_ref[:, cols] = pre
        else:
            cw = cw_ref[:, cols]
            pad = pad_ref.at[j % 2]
            if per_seq:
                p3 = pre.reshape(tm // SUBLANES, SUBLANES, tn)
                pad[:, :SUBLANES, :] = st_ref[:, :, cols]
                pad[:, SUBLANES:, :] = p3
                tail_ref[:, :, cols] = p3
                conv = cw[3:4][None] * p3
                for t in range(3):
                    conv = conv + cw[t:t + 1][None] * pad[:, SUBLANES - 3 + t:2 * SUBLANES - 3 + t, :]
            else:
                tail = pre[tm - SUBLANES:]
                pad[:SUBLANES, :] = carry_ref[:, cols]
                pad[SUBLANES:, :] = pre
                tail_ref[0, :, cols] = tail
                carry_ref[:, cols] = tail
                conv = cw[3:4] * pre
                for t in range(3):
                    conv = conv + cw[t:t + 1] * pad[pl.ds(SUBLANES - 3 + t, tm), :]
            y = _silu(conv).reshape(tm, tn)
            if j < nq:
                y = l2n(y, qscale)
            elif j < 2 * nq:
                y = l2n(y, 1.0)
            o_ref[:, cols] = y


def _gdn_in(x, g, w_all, cw, *, state=None, seq_len, a_qk, a_qkv, a_vd, qscale, tm, tn=512):
    m, d = x.shape
    per_seq = state is not None
    nq, nqkv, nz = a_qk // tn, a_qkv // tn, a_vd // tn
    groups = tm // SUBLANES if per_seq else 1
    in_specs = [pl.BlockSpec((tm, d), lambda i: (i, 0)),
                _resident((1, d)), _resident(w_all.shape), _resident((SUBLANES, a_qkv))]
    args = [x, g.reshape(1, d), w_all, cw]
    if per_seq:
        in_specs.append(pl.BlockSpec((groups, SUBLANES, a_qkv), lambda i: (i, 0, 0)))
        args.append(state)
    if per_seq:
        scratch = [pltpu.VMEM((2, groups, 2 * SUBLANES, tn), F32)]
    else:
        scratch = [pltpu.VMEM((2, SUBLANES + tm, tn), F32), pltpu.VMEM((SUBLANES, a_qkv), F32)]
    return pl.pallas_call(
        functools.partial(_gdn_in_kernel, per_seq=per_seq, tiles_per_seq=max(seq_len // tm, 1),
                          tn=tn, nq=nq, nqkv=nqkv, nz=nz, qscale=qscale),
        grid=(m // tm,),
        in_specs=in_specs,
        out_specs=[pl.BlockSpec((tm, a_qkv + a_vd), lambda i: (i, 0)),
                   pl.BlockSpec((tm, 2 * LANES), lambda i: (i, 0)),
                   pl.BlockSpec((groups, SUBLANES, a_qkv), lambda i: (i, 0, 0))],
        out_shape=[jax.ShapeDtypeStruct((m, a_qkv + a_vd), F32),
                   jax.ShapeDtypeStruct((m, 2 * LANES), F32),
                   jax.ShapeDtypeStruct((m // tm * groups, SUBLANES, a_qkv), F32)],
        scratch_shapes=scratch,
        compiler_params=_cparams("arbitrary"),
        name="gdn_in",
    )(*args)


def _split(a):
    hi = a.astype(BF16)
    return hi, (a - hi.astype(F32)).astype(BF16)


def _dot3(a, b, nt=False):
    f = _dot_nt if nt else _dot
    return f(a[0], b[0]) + (f(a[0], b[1]) + f(a[1], b[0]))


def _unit_lower_inverses(lms, row, col, eye):
    def same(s):
        return (row >> s) == (col >> s)

    def dot1(a, b):
        return _dot(a.astype(BF16), b.astype(BF16))

    nd = [jnp.where(same(3), -lm, 0.0) for lm in lms]
    nd2 = [dot1(a, a) for a in nd]
    nd4 = [dot1(a, a) for a in nd2]
    x = [dot1(eye + a, eye + b) for a, b in zip(nd, nd2)]
    x = [dot1(a, eye + b) for a, b in zip(x, nd4)]
    for s in (3, 4, 5):
        mask = same(s + 1) & jnp.logical_not(same(s))
        y = [dot1(a, jnp.where(mask, lm, 0.0)) for a, lm in zip(x, lms)]
        x = [a - dot1(b, a) for a, b in zip(x, y)]
    return x


def _gdn_chunk_kernel(*refs, heads, dk, dv, has_s0):
    q_ref, k_ref, v_ref, z_ref, ba_ref, na_ref, dt_ref, gain_ref = refs[:8]
    pos = 8
    s0_ref = None
    if has_s0:
        s0_ref = refs[pos]
        pos += 1
    o_ref, sout_ref, s_ref = refs[pos:pos + 3]
    n = pl.program_id(1)

    @pl.when(n == 0)
    def _():
        if has_s0:
            s_ref[...] = s0_ref[...]
        else:
            s_ref[...] = jnp.zeros_like(s_ref)

    bt, cr = q_ref.shape[0], q_ref.shape[1]

    def pad(a):
        if cr == CHUNK:
            return a
        return jnp.concatenate([a, jnp.zeros((CHUNK - cr, a.shape[1]), a.dtype)], axis=0)

    row = lax.broadcasted_iota(jnp.int32, (CHUNK, CHUNK), 0)
    col = lax.broadcasted_iota(jnp.int32, (CHUNK, CHUNK), 1)
    lower = row >= col
    strict = row > col
    eye = (row == col).astype(F32)
    tri = lower.astype(F32)
    gain = gain_ref[...]

    qh, kh, vh, bcol, egcol, eglcol, eglast, decay = [], [], [], [], [], [], [], []
    for b in range(bt):
        ba = ba_ref[b]
        beta = pad(jax.nn.sigmoid(ba[:, :LANES]))
        g = pad(na_ref[...] * jax.nn.softplus(ba[:, LANES:] + dt_ref[...]))
        q, k, v = pad(q_ref[b]), pad(k_ref[b]), pad(v_ref[b])
        gc = _dot_hi(tri, g)
        gct = gc.T
        glast = gc[CHUNK - 1:CHUNK]
        eg = jnp.exp(gc)
        egl = jnp.exp(glast - gc)
        egt = jnp.exp(glast)
        for h in range(heads):
            qh.append(q[:, h * dk:(h + 1) * dk])
            kh.append(k[:, h * dk:(h + 1) * dk])
            vh.append(v[:, h * dv:(h + 1) * dv])
            bcol.append(beta[:, h:h + 1])
            egcol.append(eg[:, h:h + 1])
            eglcol.append(egl[:, h:h + 1])
            eglast.append(egt[:, h:h + 1])
            decay.append(jnp.where(lower, jnp.exp(gc[:, h:h + 1] - gct[h:h + 1, :]), 0.0))
    ps = range(bt * heads)
    kb = [kh[i] * bcol[i] for i in ps]
    kk = [_dot3(_split(kb[i]), _split(kh[i]), nt=True) for i in ps]
    lm = [jnp.where(strict, kk[i] * decay[i], 0.0) for i in ps]
    t = _unit_lower_inverses(lm, row, col, eye)
    rhs = [jnp.concatenate([vh[i] * bcol[i], kb[i] * egcol[i]], axis=1) for i in ps]
    sol = [_dot(t[i].astype(BF16), rhs[i].astype(BF16)) for i in ps]
    kh16 = [kh[i].astype(BF16) for i in ps]
    attn = [(_dot_nt(qh[i].astype(BF16), kh16[i]) * decay[i]).astype(BF16) for i in ps]
    qe16 = [(qh[i] * egcol[i]).astype(BF16) for i in ps]
    kd16 = [(kh[i] * eglcol[i]).astype(BF16) for i in ps]
    sh = [s_ref[i // heads, i % heads] for i in ps]
    sh16 = [a.astype(BF16) for a in sh]
    v_new = [sol[i][:, :dv] - _dot(sol[i][:, dv:].astype(BF16), sh16[i]) for i in ps]
    vn16 = [a.astype(BF16) for a in v_new]
    o = [_dot(qe16[i], sh16[i]) + _dot(attn[i], vn16[i]) for i in ps]
    for i in ps:
        s_ref[i // heads, i % heads] = sh[i] * eglast[i] + _dot_tn(kd16[i], vn16[i])
    for i in ps:
        b, h = i // heads, i % heads
        oh = o[i][:cr]
        zh = z_ref[b, :, h * dv:(h + 1) * dv]
        on = oh * lax.rsqrt(jnp.mean(oh * oh, axis=-1, keepdims=True) + EPS) * gain * _silu(zh)
        o_ref[b, :, h * dv:(h + 1) * dv] = on.astype(o_ref.dtype)

    @pl.when(n == pl.num_programs(1) - 1)
    def _():
        sout_ref[...] = s_ref[...]


def _gdn_chunk(qkvz, ba, neg_a, dt_bias, gain, *, s0, batch, seq_len, heads, dk, dv, bt=4):
    m = qkvz.shape[0]
    cr = min(CHUNK, seq_len)
    nc = seq_len // cr
    hd = heads * dk
    bt = math.gcd(batch, bt)
    has_s0 = s0 is not None
    qkvz = qkvz.reshape(batch, seq_len, -1)
    ba = ba.reshape(batch, seq_len, -1)

    def blk(c):
        return pl.BlockSpec((bt, cr, hd), lambda b, n: (b, n, c))

    in_specs = [blk(0), blk(1), blk(2), blk(3),
                pl.BlockSpec((bt, cr, 2 * LANES), lambda b, n: (b, n, 0)),
                pl.BlockSpec((1, LANES), lambda b, n: (0, 0)),
                pl.BlockSpec((1, LANES), lambda b, n: (0, 0)),
                pl.BlockSpec((1, dv), lambda b, n: (0, 0))]
    args = [qkvz, qkvz, qkvz, qkvz, ba, neg_a, dt_bias, gain.reshape(1, dv)]
    if has_s0:
        in_specs.append(pl.BlockSpec((bt, heads, dk, dv), lambda b, n: (b, 0, 0, 0)))
        args.append(s0)
    o, s_fin = pl.pallas_call(
        functools.partial(_gdn_chunk_kernel, heads=heads, dk=dk, dv=dv, has_s0=has_s0),
        grid=(batch // bt, nc),
        in_specs=in_specs,
        out_specs=[pl.BlockSpec((bt, cr, hd), lambda b, n: (b, n, 0)),
                   pl.BlockSpec((bt, heads, dk, dv), lambda b, n: (b, 0, 0, 0))],
        out_shape=[jax.ShapeDtypeStruct((batch, seq_len, hd), BF16),
                   jax.ShapeDtypeStruct((batch, heads, dk, dv), F32)],
        scratch_shapes=[pltpu.VMEM((bt, heads, dk, dv), F32)],
        compiler_params=_cparams("parallel", "arbitrary"),
        name="gdn_chunk",
    )(*args)
    return o.reshape(m, hd), s_fin


def _lambda(lv_ref, lam_init):
    lv = lv_ref[...]
    a = jnp.sum(lv[0:1] * lv[1:2], axis=-1, keepdims=True)
    b = jnp.sum(lv[2:3] * lv[3:4], axis=-1, keepdims=True)
    return jnp.exp(a) - jnp.exp(b) + lam_init


def _flash_kernel(qt_ref, kt_ref, q_ref, k_ref, v_ref, lv_ref, sub_ref, o_ref, qs_ref, m_ref, acc_ref,
                  *, dh, lam_init):
    step = pl.program_id(2)
    qi = qt_ref[step]
    ki = kt_ref[step]
    tq = q_ref.shape[1]
    tk = k_ref.shape[0]
    vd = v_ref.shape[0]

    @pl.when(ki == 0)
    def _():
        q = q_ref[...]
        feat = lax.broadcasted_iota(jnp.int32, q.shape, 0)
        zero = jnp.zeros_like(q)
        qs_ref[:, :tq] = jnp.where(feat < dh, q, zero)
        qs_ref[:, tq:] = jnp.where(feat >= dh, q, zero)
        m_ref[...] = jnp.full_like(m_ref, -jnp.inf)
        acc_ref[...] = jnp.zeros_like(acc_ref)

    def accumulate(masked):
        s = _dot(k_ref[...], qs_ref[...])
        if masked:
            kpos = ki * tk + lax.broadcasted_iota(jnp.int32, s.shape, 0)
            qpos = qi * tq + (lax.broadcasted_iota(jnp.int32, s.shape, 1) & (tq - 1))
            s = jnp.where(kpos <= qpos, s, -jnp.inf)
        m_old = m_ref[...]
        m_new = jnp.maximum(m_old, jnp.max(s, axis=0, keepdims=True))
        alpha = jnp.exp2(m_old - m_new)
        p = jnp.exp2(s - m_new).astype(BF16)
        v_ones = jnp.concatenate([v_ref[...], jnp.ones((acc_ref.shape[0] - vd, tk), BF16)], axis=0)
        acc_ref[...] = alpha * acc_ref[...] + _dot(v_ones, p)
        m_ref[...] = m_new

    def accumulate_diagonal():
        hk, hq = tk // 2, tq // 2
        v_ones = jnp.concatenate([v_ref[...], jnp.ones((acc_ref.shape[0] - vd, tk), BF16)], axis=0)
        s = _dot(k_ref[:hk, :], qs_ref[...])
        kpos = lax.broadcasted_iota(jnp.int32, s.shape, 0)
        qpos = lax.broadcasted_iota(jnp.int32, s.shape, 1) & (tq - 1)
        s = jnp.where(kpos <= qpos, s, -jnp.inf)
        m_old = m_ref[...]
        m_new = jnp.maximum(m_old, jnp.max(s, axis=0, keepdims=True))
        p = jnp.exp2(s - m_new).astype(BF16)
        acc_ref[...] = jnp.exp2(m_old - m_new) * acc_ref[...] + _dot(v_ones[:, :hk], p)
        m_ref[...] = m_new
        k2 = k_ref[hk:, :]
        v2 = v_ones[:, hk:]
        for base in (hq, tq + hq):
            cols = slice(base, base + hq)
            s = _dot(k2, qs_ref[:, cols])
            kpos = lax.broadcasted_iota(jnp.int32, s.shape, 0)
            qpos = lax.broadcasted_iota(jnp.int32, s.shape, 1)
            s = jnp.where(kpos <= qpos, s, -jnp.inf)
            m_old = m_ref[:, cols]
            m_new = jnp.maximum(m_old, jnp.max(s, axis=0, keepdims=True))
            p = jnp.exp2(s - m_new).astype(BF16)
            acc_ref[:, cols] = jnp.exp2(m_old - m_new) * acc_ref[:, cols] + _dot(v2, p)
            m_ref[:, cols] = m_new

    below_diagonal = (ki + 1) * tk - 1 <= qi * tq

    @pl.when(below_diagonal)
    def _():
        accumulate(False)

    @pl.when(jnp.logical_not(below_diagonal))
    def _():
        if tq == tk:
            accumulate_diagonal()
        else:
            accumulate(True)

    @pl.when((ki + 1) * tk == (qi + 1) * tq)
    def _():
        lam = _lambda(lv_ref, lam_init)
        a = acc_ref[:vd] / acc_ref[vd:vd + 1]
        o = a[:, :tq] - lam * a[:, tq:]
        r = lax.rsqrt(jnp.mean(o * o, axis=0, keepdims=True) + EPS)
        o = o * r * (sub_ref[...] * (1.0 - lam_init))
        o_ref[...] = o.T.astype(o_ref.dtype)


def _flash_prompt(q_t, k, v_t, lam_vecs, subln, *, batch, seq_len, dh, vd, lam_init, tq=1024, tk=1024):
    m, width = k.shape
    pairs = width // (2 * dh)
    tq = min(tq, seq_len)
    tk = min(tk, tq)
    nq, nk = seq_len // tq, seq_len // tk
    assert tq & (tq - 1) == 0 and tq % tk == 0
    steps = [(i, j) for i in range(nq) for j in range((i + 1) * tq // tk)]
    q_tab = jnp.asarray([s[0] for s in steps], jnp.int32)
    k_tab = jnp.asarray([s[1] for s in steps], jnp.int32)
    grid_spec = pltpu.PrefetchScalarGridSpec(
        num_scalar_prefetch=2,
        grid=(batch, pairs, len(steps)),
        in_specs=[pl.BlockSpec((2 * dh, tq), lambda b, h, s, qt, kt: (h, b * nq + qt[s])),
                  pl.BlockSpec((tk, 2 * dh), lambda b, h, s, qt, kt: (b * nk + kt[s], h)),
                  pl.BlockSpec((vd, tk), lambda b, h, s, qt, kt: (h, b * nk + kt[s])),
                  pl.BlockSpec(lam_vecs.shape, lambda b, h, s, qt, kt: (0, 0)),
                  pl.BlockSpec((vd, 1), lambda b, h, s, qt, kt: (0, 0))],
        out_specs=pl.BlockSpec((tq, vd), lambda b, h, s, qt, kt: (b * nq + qt[s], h)),
        scratch_shapes=[pltpu.VMEM((2 * dh, 2 * tq), BF16),
                        pltpu.VMEM((1, 2 * tq), F32),
                        pltpu.VMEM((vd + 2 * SUBLANES, 2 * tq), F32)])
    return pl.pallas_call(
        functools.partial(_flash_kernel, dh=dh, lam_init=lam_init),
        grid_spec=grid_spec,
        out_shape=jax.ShapeDtypeStruct((m, pairs * vd), BF16),
        compiler_params=_cparams("parallel", "parallel", "arbitrary"),
        name="diff_flash",
    )(q_tab, k_tab, q_t, k, v_t, lam_vecs, subln.reshape(vd, 1))


def _paged_kernel(*refs, nh, dh, vd, lam_init, group):
    q_ref, kn_ref, vn_ref = refs[1:4]
    kc_refs = refs[4:4 + group]
    vc_refs = refs[4 + group:4 + 2 * group]
    lv_ref, sub_ref, o_ref, qbd_ref, m_ref, l_ref, acc_ref = refs[4 + 2 * group:]
    p = pl.program_id(1)
    t = q_ref.shape[1]
    width = q_ref.shape[2]
    rows = nh * t
    page = kc_refs[0].shape[2]
    nvh = width // vd

    def accum(kt16, v16, mask):
        s = _dot(qbd_ref[...], kt16)
        if mask is not None:
            s = jnp.where(mask, s, -jnp.inf)
        m_old = m_ref[...]
        m_new = jnp.maximum(m_old, jnp.max(s, axis=-1, keepdims=True))
        alpha = jnp.exp(m_old - m_new)
        pr = jnp.exp(s - m_new)
        l_ref[...] = alpha * l_ref[...] + jnp.sum(pr, axis=-1, keepdims=True)
        acc_ref[...] = alpha * acc_ref[...] + _dot(pr.astype(BF16), v16)
        m_ref[...] = m_new

    @pl.when(p == 0)
    def _():
        q = q_ref[0].astype(F32)
        q3 = jnp.broadcast_to(q[None], (nh, t, width))
        hd = lax.broadcasted_iota(jnp.int32, (nh, t, width), 0)
        ln = lax.broadcasted_iota(jnp.int32, (nh, t, width), 2)
        qbd = jnp.where((ln >= hd * dh) & (ln < (hd + 1) * dh), q3, 0.0)
        qbd_ref[...] = qbd.reshape(rows, width).astype(BF16)
        m_ref[...] = jnp.full_like(m_ref, -jnp.inf)
        l_ref[...] = jnp.zeros_like(l_ref)
        acc_ref[...] = jnp.zeros_like(acc_ref)
        zpad = jnp.zeros((page - t, width), F32)
        kt16 = jnp.concatenate([kn_ref[0], zpad], axis=0).T.astype(BF16)
        v16 = jnp.concatenate([vn_ref[0], zpad], axis=0).astype(BF16)
        r = lax.broadcasted_iota(jnp.int32, (rows, page), 0)
        c = lax.broadcasted_iota(jnp.int32, (rows, page), 1)
        accum(kt16, v16, c <= (r & (t - 1)))

    @pl.when(p > 0)
    def _():
        v = jnp.concatenate(
            [jnp.concatenate([vc[0, pl.ds(h, page, stride=nvh), :] for h in range(nvh)], axis=1).astype(BF16)
             for vc in vc_refs], axis=0)
        kt = jnp.concatenate([kc[0].astype(BF16) for kc in kc_refs], axis=1)
        accum(kt, v, None)

    @pl.when(p == pl.num_programs(1) - 1)
    def _():
        lam = _lambda(lv_ref, lam_init)
        r = lax.broadcasted_iota(jnp.int32, (rows, 1), 0)
        odd = ((r // t) & 1) == 1
        wgt = jnp.where(odd, -lam, 1.0) / l_ref[...]
        a3 = (acc_ref[...] * wgt).reshape(nh, t, width)
        hd = lax.broadcasted_iota(jnp.int32, (nh, t, width), 0)
        ln = lax.broadcasted_iota(jnp.int32, (nh, t, width), 2)
        pair = hd >> 1
        o = jnp.sum(jnp.where((ln >= pair * vd) & (ln < (pair + 1) * vd), a3, 0.0), axis=0)
        sub = sub_ref[...]
        for h in range(width // vd):
            oh = o[:, h * vd:(h + 1) * vd]
            o_ref[0, :, h * vd:(h + 1) * vd] = (_rms(oh, sub) * (1.0 - lam_init)).astype(o_ref.dtype)


def _paged_attention(q, k_new, v_new, cache_k, cache_v, page_table, lam_vecs, subln, *, dh, vd, lam_init, group=8):
    b, t, width = q.shape
    npg = page_table.shape[1]
    page = cache_k.shape[2]
    nh = width // dh
    nvh = width // vd
    assert t & (t - 1) == 0 and t <= page and nh == 2 * nvh and cache_v.shape[1] == page * nvh
    group = math.gcd(npg, group)

    def pidx(i):
        return lambda bb, p, pt: (pt[bb * npg + jnp.maximum(p - 1, 0) * group + i], 0, 0)

    grid_spec = pltpu.PrefetchScalarGridSpec(
        num_scalar_prefetch=1,
        grid=(b, npg // group + 1),
        in_specs=[pl.BlockSpec((1, t, width), lambda bb, p, pt: (bb, 0, 0)),
                  pl.BlockSpec((1, t, width), lambda bb, p, pt: (bb, 0, 0)),
                  pl.BlockSpec((1, t, width), lambda bb, p, pt: (bb, 0, 0))]
        + [pl.BlockSpec((1, width, page), pidx(i)) for i in range(group)]
        + [pl.BlockSpec((1, page * nvh, vd), pidx(i)) for i in range(group)]
        + [pl.BlockSpec(lam_vecs.shape, lambda bb, p, pt: (0, 0)),
           pl.BlockSpec((1, vd), lambda bb, p, pt: (0, 0))],
        out_specs=pl.BlockSpec((1, t, width), lambda bb, p, pt: (bb, 0, 0)),
        scratch_shapes=[pltpu.VMEM((nh * t, width), BF16),
                        pltpu.VMEM((nh * t, 1), F32),
                        pltpu.VMEM((nh * t, 1), F32),
                        pltpu.VMEM((nh * t, width), F32)])
    return pl.pallas_call(
        functools.partial(_paged_kernel, nh=nh, dh=dh, vd=vd, lam_init=lam_init, group=group),
        grid_spec=grid_spec,
        out_shape=jax.ShapeDtypeStruct((b, t, width), BF16),
        compiler_params=_cparams("parallel", "arbitrary"),
        name="diff_paged",
    )(page_table.reshape(-1), q, k_new, v_new, *([cache_k] * group), *([cache_v] * group),
      lam_vecs, subln.reshape(1, vd))


def _rot_weight(w, dh):
    k, n = w.shape
    w4 = w.reshape(k, n // dh, 2, dh // 2)
    return jnp.stack([-w4[:, :, 1], w4[:, :, 0]], axis=2).reshape(k, n)


def _rope_tables(pos, dh):
    half = dh // 2
    inv = 1.0 / (ROPE_THETA ** (jnp.arange(half, dtype=F32) / half))
    ang = pos.astype(F32)[:, None] * inv[None, :]
    reps = LANES // half
    return jnp.tile(jnp.cos(ang), (1, reps)), jnp.tile(jnp.sin(ang), (1, reps))


def _pad_rows(a, rows, front):
    pad = [(0, 0)] * a.ndim
    pad[-2] = (rows - a.shape[-2], 0) if front else (0, rows - a.shape[-2])
    return jnp.pad(a, pad)


def _prep_weights(p):
    n_a, d, a_in = p["a_w_in"].shape
    heads = p["a_A_log"].shape[1]
    dv = p["a_o_gain"].shape[1]
    a_vd = heads * dv
    a_qkv = a_in - a_vd - 2 * heads
    dh = p["b_lambda"].shape[-1]
    w = {}
    w_in = p["a_w_in"]
    tn = 512
    zpad = jnp.zeros((n_a, d, LANES - heads), F32)
    w["a_w_all"] = jnp.concatenate(
        [w_in[:, :, :a_qkv + a_vd], w_in[:, :, a_qkv + a_vd:a_qkv + a_vd + heads], zpad,
         w_in[:, :, a_qkv + a_vd + heads:], zpad, jnp.zeros((n_a, d, tn - 2 * LANES), F32)], axis=2).astype(BF16)
    w["a_cw"] = _pad_rows(p["a_conv_w"], SUBLANES, front=False)
    hp = jnp.zeros((n_a, LANES - heads), F32)
    w["a_neg_a"] = jnp.concatenate([-jnp.exp(p["a_A_log"].astype(F32)), hp], axis=1)[:, None, :]
    w["a_dt"] = jnp.concatenate([p["a_dt_bias"].astype(F32), hp], axis=1)[:, None, :]
    w["a_w_out"] = p["a_w_out"].astype(BF16)
    kq = p["w_kv"].shape[1] - (p["b_w_out"].shape[1])
    w["w_k"] = p["w_kv"][:, :kq].astype(BF16)
    w["w_k_rot"] = _rot_weight(p["w_kv"][:, :kq], dh).astype(BF16)
    w["w_v"] = p["w_kv"][:, kq:].astype(BF16)
    w["b_w_q"] = p["b_w_q"].astype(BF16)
    w["b_w_q_rot"] = jnp.stack([_rot_weight(p["b_w_q"][j], dh) for j in range(p["b_w_q"].shape[0])]).astype(BF16)
    w["b_w_out"] = p["b_w_out"].astype(BF16)
    w["w_v_t"] = w["w_v"].T
    w["b_w_q_t"] = jnp.swapaxes(w["b_w_q"], 1, 2)
    w["b_w_q_rot_t"] = jnp.swapaxes(w["b_w_q_rot"], 1, 2)
    f = p["f_w_down"].shape[1]
    w["f_wu"] = p["f_w_up"].astype(BF16)
    w["f_cw"] = _pad_rows(p["f_conv_w"], SUBLANES, front=False)
    w["f_wd"] = p["f_w_down"].astype(BF16)
    return w


def _trunk(x, pos, p, w, *, delta0, dconv0, fconv0, cache_k, cache_v, page_table):
    b, l, d = x.shape
    m = b * l
    sample = page_table is not None
    depth = p["f_norm"].shape[0]
    n_a = p["a_norm"].shape[0]
    heads = p["a_A_log"].shape[1]
    dv = p["a_o_gain"].shape[1]
    dk = (p["a_w_in"].shape[2] - 2 * heads - 2 * heads * dv) // (2 * heads)
    a_vd = heads * dv
    a_qk = heads * dk
    a_qkv = 2 * a_qk + a_vd
    dh = p["b_lambda"].shape[-1]
    vd = p["b_subln"].shape[-1]
    f = p["f_w_down"].shape[1]
    tm = m if sample else min(512, l)
    assert m % tm == 0 and (sample or l % tm == 0)

    h = x.reshape(m, d)
    cos, sin = _rope_tables(pos, dh)
    if sample:
        cos, sin = jnp.tile(cos, (b, 1)), jnp.tile(sin, (b, 1))

    def tails_to_state(tails, rows):
        if sample:
            return tails[:, SUBLANES - rows:, :]
        per = tails.shape[0] // b
        return tails.reshape(b, per, SUBLANES, -1)[:, per - 1, SUBLANES - rows:, :]

    deltas, dconvs, fconvs = [], [], []
    k_new = v_new = k16 = v16_t = None
    for layer in range(depth):
        if layer < n_a:
            st = _pad_rows(dconv0[layer], SUBLANES, front=True) if sample else None
            qkvz, ba, tails = _gdn_in(h, p["a_norm"][layer], w["a_w_all"][layer], w["a_cw"][layer], state=st,
                                      seq_len=l, a_qk=a_qk, a_qkv=a_qkv, a_vd=a_vd, qscale=dk ** -0.5,
                                      tm=tm if sample else min(tm, 256))
            dconvs.append(tails_to_state(tails, p["a_conv_w"].shape[1] - 1))
            o, s_fin = _gdn_chunk(qkvz, ba, w["a_neg_a"][layer], w["a_dt"][layer], p["a_o_gain"][layer],
                                  s0=delta0[layer] if sample else None, batch=b, seq_len=l,
                                  heads=heads, dk=dk, dv=dv)
            deltas.append(s_fin)
            w_o = w["a_w_out"][layer]
        else:
            if layer == n_a:
                k_new, k16 = _norm_proj(h, p["kv_norm"], w["w_k"], w_rot=w["w_k_rot"], cos=cos, sin=sin,
                                        out_dtypes=(F32, BF16), tm=tm)
                (v_new,) = _norm_proj(h, p["kv_norm"], w["w_v"], out_dtypes=(F32,), tm=tm)
                if not sample:
                    v16_t = _norm_proj_t(h, p["kv_norm"], w["w_v_t"], tm=tm)
            j = layer - n_a
            lam_init = 0.8 - 0.6 * math.exp(-0.3 * layer)
            if sample:
                (q16,) = _norm_proj(h, p["b_norm"][j], w["b_w_q"][j], w_rot=w["b_w_q_rot"][j], cos=cos, sin=sin,
                                    scale=dh ** -0.5, out_dtypes=(BF16,), tm=tm)
                width = q16.shape[1]
                o = _paged_attention(q16.reshape(b, l, width), k_new.reshape(b, l, width), v_new.reshape(b, l, width),
                                     cache_k, cache_v, page_table, p["b_lambda"][j], p["b_subln"][j],
                                     dh=dh, vd=vd, lam_init=lam_init).reshape(m, width)
            else:
                q16_t = _norm_proj_t(h, p["b_norm"][j], w["b_w_q_t"][j], wt_rot=w["b_w_q_rot_t"][j],
                                     cos_t=cos.T, sin_t=sin.T, scale=dh ** -0.5 * math.log2(math.e), tm=tm)
                o = _flash_prompt(q16_t, k16, v16_t, p["b_lambda"][j], p["b_subln"][j], batch=b, seq_len=l,
                                  dh=dh, vd=vd, lam_init=lam_init)
            w_o = w["b_w_out"][j]
        st = _pad_rows(fconv0[layer], SUBLANES, front=True) if sample else None
        h, tails = _conv_ffn(h, o, w_o, p["f_norm"][layer], w["f_wu"][layer], w["f_cw"][layer],
                             w["f_wd"][layer], state=st, final_g=p["final_norm"] if layer == depth - 1 else None,
                             seq_len=l, tm=tm)
        fconvs.append(tails_to_state(tails, p["f_conv_w"].shape[1] - 1))
    nkh = k_new.shape[1] // dh
    return (h.reshape(b, l, d), jnp.stack(deltas), jnp.stack(dconvs), jnp.stack(fconvs),
            k_new.reshape(b, l, nkh, dh), v_new.reshape(b, l, v_new.shape[1] // vd, vd))


def kernel(x_prompt, x_sample, state_delta, state_dconv, state_fconv, cache_k, cache_v, page_table, a_norm, a_w_in, a_conv_w, a_A_log, a_dt_bias, a_o_gain, a_w_out, kv_norm, w_kv, b_norm, b_w_q, b_lambda, b_subln, b_w_out, f_norm, f_w_up, f_conv_w, f_w_down, final_norm):
    p = dict(a_norm=a_norm, a_w_in=a_w_in, a_conv_w=a_conv_w, a_A_log=a_A_log, a_dt_bias=a_dt_bias,
             a_o_gain=a_o_gain, a_w_out=a_w_out, kv_norm=kv_norm, w_kv=w_kv, b_norm=b_norm, b_w_q=b_w_q,
             b_lambda=b_lambda, b_subln=b_subln, b_w_out=b_w_out, f_norm=f_norm, f_w_up=f_w_up,
             f_conv_w=f_conv_w, f_w_down=f_w_down, final_norm=final_norm)
    w = _prep_weights(p)
    lp = x_prompt.shape[1]
    prompt = _trunk(x_prompt, jnp.arange(lp, dtype=jnp.int32), p, w, delta0=None, dconv0=None, fconv0=None,
                    cache_k=None, cache_v=None, page_table=None)
    ls = x_sample.shape[1]
    past_len = page_table.shape[1] * cache_k.shape[1]
    pool, page = cache_k.shape[:2]
    sample = _trunk(x_sample, past_len + jnp.arange(ls, dtype=jnp.int32), p, w, delta0=state_delta,
                    dconv0=state_dconv, fconv0=state_fconv,
                    cache_k=cache_k.transpose(0, 2, 3, 1).reshape(pool, -1, page),
                    cache_v=cache_v.reshape(pool, page * cache_v.shape[2], cache_v.shape[3]),
                    page_table=page_table)
    return (prompt[0], sample[0]) + prompt[1:] + sample[1:]
```

```python
import functools
import math

import jax
import jax.numpy as jnp
from jax import lax
from jax.experimental import pallas as pl
from jax.experimental.pallas import tpu as pltpu

F32 = jnp.float32
BF16 = jnp.bfloat16
EPS = 1e-6
ROPE_THETA = 10000.0
LANES = 128
SUBLANES = 8
CHUNK = 64
VMEM_LIMIT = 48 * 1024 * 1024
HI = lax.Precision.HIGHEST


def _cparams(*sem):
    return pltpu.CompilerParams(dimension_semantics=sem, vmem_limit_bytes=VMEM_LIMIT)


def _dot(a, b):
    return jnp.dot(a, b, preferred_element_type=F32)


def _dot_nt(a, b, precision=None):
    return lax.dot_general(a, b, (((1,), (1,)), ((), ())), precision=precision,
                           preferred_element_type=F32)


def _dot_tn(a, b, precision=None):
    return lax.dot_general(a, b, (((0,), (0,)), ((), ())), precision=precision,
                           preferred_element_type=F32)


def _dot_hi(a, b):
    return jnp.dot(a, b, precision=HI, preferred_element_type=F32)


def _rms(x, g):
    r = lax.rsqrt(jnp.mean(x * x, axis=-1, keepdims=True) + EPS)
    return x * r * g


def _silu(x):
    return x * jax.nn.sigmoid(x)


def _proj_kernel(*refs, rope, scale, n_out, feature_major):
    if rope:
        x_ref, g_ref, w_ref, wr_ref, cos_ref, sin_ref = refs[:6]
        rest = refs[6:]
    else:
        x_ref, g_ref, w_ref = refs[:3]
        rest = refs[3:]
    outs, xn_ref = rest[:n_out], rest[-1]

    @pl.when(pl.program_id(1) == 0)
    def _():
        xn_ref[...] = _rms(x_ref[...], g_ref[...]).astype(BF16)

    xn = xn_ref[...]
    y = _dot(xn, w_ref[...])
    if rope:
        yr = _dot(xn, wr_ref[...])
        reps = y.shape[1] // LANES
        cos = jnp.concatenate([cos_ref[...]] * reps, axis=1)
        sin = jnp.concatenate([sin_ref[...]] * reps, axis=1)
        y = y * cos + yr * sin
    if scale != 1.0:
        y = y * scale
    for o in outs:
        o[...] = y.astype(o.dtype)
    if feature_major:
        rest[n_out][...] = y.T


def _norm_proj(x, g, w, *, w_rot=None, cos=None, sin=None, scale=1.0, out_dtypes=(F32,), t_seq=None, tm, tn=1024):
    m, d = x.shape
    n = w.shape[1]
    tn = min(tn, n)
    rope = w_rot is not None
    out_specs = [pl.BlockSpec((tm, tn), lambda i, j: (i, j)) for _ in out_dtypes]
    out_shape = [jax.ShapeDtypeStruct((m, n), dt) for dt in out_dtypes]
    if t_seq is not None:
        tps = t_seq // tm
        out_specs.append(pl.BlockSpec((None, tn, tm), lambda i, j: (i // tps, j, i % tps)))
        out_shape.append(jax.ShapeDtypeStruct((m // t_seq, n, t_seq), F32))
    in_specs = [pl.BlockSpec((tm, d), lambda i, j: (i, 0)),
                pl.BlockSpec((1, d), lambda i, j: (0, 0)),
                pl.BlockSpec((d, tn), lambda i, j: (0, j))]
    args = [x, g.reshape(1, d), w]
    if rope:
        pt = cos.shape[0] // tm
        in_specs += [pl.BlockSpec((d, tn), lambda i, j: (0, j)),
                     pl.BlockSpec((tm, LANES), lambda i, j: (i % pt, 0)),
                     pl.BlockSpec((tm, LANES), lambda i, j: (i % pt, 0))]
        args += [w_rot, cos, sin]
    res = pl.pallas_call(
        functools.partial(_proj_kernel, rope=rope, scale=scale, n_out=len(out_dtypes),
                          feature_major=t_seq is not None),
        grid=(m // tm, n // tn),
        in_specs=in_specs,
        out_specs=out_specs,
        out_shape=out_shape,
        scratch_shapes=[pltpu.VMEM((tm, d), BF16)],
        compiler_params=_cparams("parallel", "arbitrary"),
        name="norm_proj_rope" if rope else "norm_proj",
    )(*args)
    return res


def _proj_t_kernel(*refs, rope, scale):
    if rope:
        x_ref, g_ref, wt_ref, wrt_ref, cos_ref, sin_ref, o_ref, xn_ref = refs
    else:
        x_ref, g_ref, wt_ref, o_ref, xn_ref = refs

    @pl.when(pl.program_id(1) == 0)
    def _():
        xn_ref[...] = _rms(x_ref[...], g_ref[...]).astype(BF16)

    xn = xn_ref[...]
    y = _dot_nt(wt_ref[...], xn)
    if rope:
        yr = _dot_nt(wrt_ref[...], xn)
        reps = y.shape[0] // LANES
        cos = jnp.concatenate([cos_ref[...]] * reps, axis=0)
        sin = jnp.concatenate([sin_ref[...]] * reps, axis=0)
        y = y * cos + yr * sin
    if scale != 1.0:
        y = y * scale
    o_ref[...] = y.astype(o_ref.dtype)


def _norm_proj_t(x, g, wt, *, wt_rot=None, cos_t=None, sin_t=None, scale=1.0, tm, tn=1024):
    m, d = x.shape
    n = wt.shape[0]
    tn = min(tn, n)
    rope = wt_rot is not None
    in_specs = [pl.BlockSpec((tm, d), lambda i, j: (i, 0)),
                pl.BlockSpec((1, d), lambda i, j: (0, 0)),
                pl.BlockSpec((tn, d), lambda i, j: (j, 0))]
    args = [x, g.reshape(1, d), wt]
    if rope:
        pt = cos_t.shape[1] // tm
        in_specs += [pl.BlockSpec((tn, d), lambda i, j: (j, 0)),
                     pl.BlockSpec((LANES, tm), lambda i, j: (0, i % pt)),
                     pl.BlockSpec((LANES, tm), lambda i, j: (0, i % pt))]
        args += [wt_rot, cos_t, sin_t]
    return pl.pallas_call(
        functools.partial(_proj_t_kernel, rope=rope, scale=scale),
        grid=(m // tm, n // tn),
        in_specs=in_specs,
        out_specs=pl.BlockSpec((tn, tm), lambda i, j: (j, i)),
        out_shape=jax.ShapeDtypeStruct((n, m), BF16),
        scratch_shapes=[pltpu.VMEM((tm, d), BF16)],
        compiler_params=_cparams("parallel", "arbitrary"),
        name="norm_proj_t_rope" if rope else "norm_proj_t",
    )(*args)


def _resident(shape):
    return pl.BlockSpec(shape, lambda *_: (0,) * len(shape), pipeline_mode=pl.Buffered(1))


def _ffn_kernel(*refs, per_seq, tiles_per_seq, tf, final):
    x_ref, mix_ref, wo_ref, g_ref, wu_ref, cw_ref, wd_ref = refs[:7]
    pos = 7
    st_ref = fg_ref = carry_ref = None
    if per_seq:
        st_ref = refs[pos]
        pos += 1
    if final:
        fg_ref = refs[pos]
        pos += 1
    o_ref, tail_ref, act_ref, pad_ref = refs[pos:pos + 4]
    if not per_seq:
        carry_ref = refs[pos + 4]
    i = pl.program_id(0)
    tm = x_ref.shape[0]
    f = wd_ref.shape[0]
    nf = f // tf
    x = x_ref[...] + _dot(mix_ref[...], wo_ref[...])
    xn = _rms(x, g_ref[...]).astype(BF16)

    if not per_seq:
        @pl.when((i % tiles_per_seq) == 0)
        def _():
            carry_ref[...] = jnp.zeros_like(carry_ref)

    for j in range(nf):
        cols = slice(j * tf, (j + 1) * tf)
        gate = _dot(xn, wu_ref[:, cols])
        val = _dot(xn, wu_ref[:, f + j * tf:f + (j + 1) * tf])
        cw = cw_ref[:, cols]
        pad = pad_ref.at[j % 2]
        if per_seq:
            g3 = gate.reshape(tm // SUBLANES, SUBLANES, tf)
            pad[:, :SUBLANES, :] = st_ref[:, :, cols]
            pad[:, SUBLANES:, :] = g3
            tail_ref[:, :, cols] = g3
            conv = (cw[2:3][None] * g3 + cw[1:2][None] * pad[:, SUBLANES - 1:2 * SUBLANES - 1, :]
                    + cw[0:1][None] * pad[:, SUBLANES - 2:2 * SUBLANES - 2, :])
        else:
            tail = gate[tm - SUBLANES:]
            pad[:SUBLANES, :] = carry_ref[:, cols]
            pad[SUBLANES:, :] = gate
            tail_ref[0, :, cols] = tail
            carry_ref[:, cols] = tail
            conv = (cw[2:3] * gate + cw[1:2] * pad[pl.ds(SUBLANES - 1, tm), :]
                    + cw[0:1] * pad[pl.ds(SUBLANES - 2, tm), :])
        act_ref[:, cols] = (_silu(conv).reshape(tm, tf) * val).astype(BF16)

    y = x + _dot(act_ref[...], wd_ref[...])
    if final:
        y = _rms(y, fg_ref[...])
    o_ref[...] = y


def _conv_ffn(x, mix, w_o, g, wu, cw, wd, *, state=None, final_g=None, seq_len, tm, tf=256):
    m, d = x.shape
    f = wd.shape[0]
    per_seq = state is not None
    final = final_g is not None
    groups = tm // SUBLANES if per_seq else 1
    in_specs = [pl.BlockSpec((tm, d), lambda i: (i, 0)),
                pl.BlockSpec((tm, mix.shape[1]), lambda i: (i, 0)), _resident(w_o.shape),
                _resident((1, d)), _resident((d, 2 * f)), _resident((SUBLANES, f)), _resident((f, d))]
    args = [x, mix, w_o, g.reshape(1, d), wu, cw, wd]
    if per_seq:
        in_specs.append(pl.BlockSpec((groups, SUBLANES, f), lambda i: (i, 0, 0)))
        args.append(state)
    if final:
        in_specs.append(_resident((1, d)))
        args.append(final_g.reshape(1, d))
    if per_seq:
        scratch = [pltpu.VMEM((tm, f), BF16), pltpu.VMEM((2, groups, 2 * SUBLANES, tf), F32)]
    else:
        scratch = [pltpu.VMEM((tm, f), BF16), pltpu.VMEM((2, SUBLANES + tm, tf), F32), pltpu.VMEM((SUBLANES, f), F32)]
    return pl.pallas_call(
        functools.partial(_ffn_kernel, per_seq=per_seq, tiles_per_seq=max(seq_len // tm, 1), tf=tf, final=final),
        grid=(m // tm,),
        in_specs=in_specs,
        out_specs=[pl.BlockSpec((tm, d), lambda i: (i, 0)),
                   pl.BlockSpec((groups, SUBLANES, f), lambda i: (i, 0, 0))],
        out_shape=[jax.ShapeDtypeStruct((m, d), F32),
                   jax.ShapeDtypeStruct((m // tm * groups, SUBLANES, f), F32)],
        scratch_shapes=scratch,
        compiler_params=_cparams("arbitrary"),
        name="conv_ffn",
    )(*args)


def _gdn_in_kernel(*refs, per_seq, tiles_per_seq, tn, nq, nqkv, nz, qscale):
    x_ref, g_ref, w_ref, cw_ref = refs[:4]
    pos = 4
    st_ref = carry_ref = None
    if per_seq:
        st_ref = refs[pos]
        pos += 1
    o_ref, ba_ref, tail_ref, pad_ref = refs[pos:pos + 4]
    if not per_seq:
        carry_ref = refs[pos + 4]
    i = pl.program_id(0)
    tm = x_ref.shape[0]
    xn = _rms(x_ref[...], g_ref[...]).astype(BF16)

    if not per_seq:
        @pl.when((i % tiles_per_seq) == 0)
        def _():
            carry_ref[...] = jnp.zeros_like(carry_ref)

    def l2n(y, s):
        parts = []
        for a in range(tn // LANES):
            ya = y[:, a * LANES:(a + 1) * LANES]
            r = lax.rsqrt(jnp.sum(ya * ya, axis=-1, keepdims=True) + EPS)
            parts.append(ya * (r * s) if s != 1.0 else ya * r)
        return jnp.concatenate(parts, axis=1)

    for j in range(nqkv + nz + 1):
        cols = slice(j * tn, (j + 1) * tn)
        pre = _dot(xn, w_ref[:, cols])
        if j == nqkv + nz:
            ba_ref[...] = pre[:, :2 * LANES]
        elif j >= nqkv:
            o_ref[:, cols] = pre
        else:
            cw = cw_ref[:, cols]
            pad = pad_ref.at[j % 2]
            if per_seq:
                p3 = pre.reshape(tm // SUBLANES, SUBLANES, tn)
                pad[:, :SUBLANES, :] = st_ref[:, :, cols]
                pad[:, SUBLANES:, :] = p3
                tail_ref[:, :, cols] = p3
                conv = cw[3:4][None] * p3
                for t in range(3):
                    conv = conv + cw[t:t + 1][None] * pad[:, SUBLANES - 3 + t:2 * SUBLANES - 3 + t, :]
            else:
                tail = pre[tm - SUBLANES:]
                pad[:SUBLANES, :] = carry_ref[:, cols]
                pad[SUBLANES:, :] = pre
                tail_ref[0, :, cols] = tail
                carry_ref[:, cols] = tail
                conv = cw[3:4] * pre
                for t in range(3):
                    conv = conv + cw[t:t + 1] * pad[pl.ds(SUBLANES - 3 + t, tm), :]
            y = _silu(conv).reshape(tm, tn)
            if j < nq:
                y = l2n(y, qscale)
            elif j < 2 * nq:
                y = l2n(y, 1.0)
            o_ref[:, cols] = y


def _gdn_in(x, g, w_all, cw, *, state=None, seq_len, a_qk, a_qkv, a_vd, qscale, tm, tn=512):
    m, d = x.shape
    per_seq = state is not None
    nq, nqkv, nz = a_qk // tn, a_qkv // tn, a_vd // tn
    groups = tm // SUBLANES if per_seq else 1
    in_specs = [pl.BlockSpec((tm, d), lambda i: (i, 0)),
                _resident((1, d)), _resident(w_all.shape), _resident((SUBLANES, a_qkv))]
    args = [x, g.reshape(1, d), w_all, cw]
    if per_seq:
        in_specs.append(pl.BlockSpec((groups, SUBLANES, a_qkv), lambda i: (i, 0, 0)))
        args.append(state)
    if per_seq:
        scratch = [pltpu.VMEM((2, groups, 2 * SUBLANES, tn), F32)]
    else:
        scratch = [pltpu.VMEM((2, SUBLANES + tm, tn), F32), pltpu.VMEM((SUBLANES, a_qkv), F32)]
    return pl.pallas_call(
        functools.partial(_gdn_in_kernel, per_seq=per_seq, tiles_per_seq=max(seq_len // tm, 1),
                          tn=tn, nq=nq, nqkv=nqkv, nz=nz, qscale=qscale),
        grid=(m // tm,),
        in_specs=in_specs,
        out_specs=[pl.BlockSpec((tm, a_qkv + a_vd), lambda i: (i, 0)),
                   pl.BlockSpec((tm, 2 * LANES), lambda i: (i, 0)),
                   pl.BlockSpec((groups, SUBLANES, a_qkv), lambda i: (i, 0, 0))],
        out_shape=[jax.ShapeDtypeStruct((m, a_qkv + a_vd), F32),
                   jax.ShapeDtypeStruct((m, 2 * LANES), F32),
                   jax.ShapeDtypeStruct((m // tm * groups, SUBLANES, a_qkv), F32)],
        scratch_shapes=scratch,
        compiler_params=_cparams("arbitrary"),
        name="gdn_in",
    )(*args)


def _split(a):
    hi = a.astype(BF16)
    return hi, (a - hi.astype(F32)).astype(BF16)


def _dot3(a, b, nt=False):
    f = _dot_nt if nt else _dot
    return f(a[0], b[0]) + (f(a[0], b[1]) + f(a[1], b[0]))


def _unit_lower_inverses(lms, row, col, eye):
    size = lms[0].shape[0]
    def same(s):
        return (row >> s) == (col >> s)

    def dot1(a, b):
        return _dot(a.astype(BF16), b.astype(BF16))

    nd = [jnp.where(same(3), -lm, 0.0) for lm in lms]
    nd2 = [dot1(a, a) for a in nd]
    nd4 = [dot1(a, a) for a in nd2]
    x = [dot1(eye + a, eye + b) for a, b in zip(nd, nd2)]
    x = [dot1(a, eye + b) for a, b in zip(x, nd4)]
    for s in range(3, size.bit_length() - 1):
        mask = same(s + 1) & jnp.logical_not(same(s))
        y = [dot1(a, jnp.where(mask, lm, 0.0)) for a, lm in zip(x, lms)]
        x = [a - dot1(b, a) for a, b in zip(x, y)]
    return x


def _gdn_chunk_kernel(*refs, heads, dk, dv, has_s0):
    q_ref, k_ref, v_ref, z_ref, ba_ref, na_ref, dt_ref, gain_ref = refs[:8]
    pos = 8
    s0_ref = None
    if has_s0:
        s0_ref = refs[pos]
        pos += 1
    o_ref, sout_ref, s_ref = refs[pos:pos + 3]
    n = pl.program_id(1)

    @pl.when(n == 0)
    def _():
        if has_s0:
            s_ref[...] = s0_ref[...]
        else:
            s_ref[...] = jnp.zeros_like(s_ref)

    bt, cr = q_ref.shape[0], q_ref.shape[1]
    c = max(cr, 2 * SUBLANES)

    def pad(a):
        if cr == c:
            return a
        return jnp.concatenate([a, jnp.zeros((c - cr, a.shape[1]), a.dtype)], axis=0)

    row = lax.broadcasted_iota(jnp.int32, (c, c), 0)
    col = lax.broadcasted_iota(jnp.int32, (c, c), 1)
    lower = row >= col
    strict = row > col
    eye = (row == col).astype(F32)
    tri = lower.astype(F32)
    gain = gain_ref[...]

    qh, kh, vh, bcol, egcol, eglcol, eglast, decay = [], [], [], [], [], [], [], []
    for b in range(bt):
        ba = ba_ref[b]
        beta = pad(jax.nn.sigmoid(ba[:, :LANES]))
        g = pad(na_ref[...] * jax.nn.softplus(ba[:, LANES:] + dt_ref[...]))
        q, k, v = pad(q_ref[b]), pad(k_ref[b]), pad(v_ref[b])
        gc = _dot_hi(tri, g)
        gct = gc.T
        glast = gc[c - 1:c]
        eg = jnp.exp(gc)
        egl = jnp.exp(glast - gc)
        egt = jnp.exp(glast)
        for h in range(heads):
            qh.append(q[:, h * dk:(h + 1) * dk])
            kh.append(k[:, h * dk:(h + 1) * dk])
            vh.append(v[:, h * dv:(h + 1) * dv])
            bcol.append(beta[:, h:h + 1])
            egcol.append(eg[:, h:h + 1])
            eglcol.append(egl[:, h:h + 1])
            eglast.append(egt[:, h:h + 1])
            decay.append(jnp.where(lower, jnp.exp(gc[:, h:h + 1] - gct[h:h + 1, :]), 0.0))
    ps = range(bt * heads)
    kb = [kh[i] * bcol[i] for i in ps]
    kk = [_dot3(_split(kb[i]), _split(kh[i]), nt=True) for i in ps]
    lm = [jnp.where(strict, kk[i] * decay[i], 0.0) for i in ps]
    t = _unit_lower_inverses(lm, row, col, eye)
    rhs = [jnp.concatenate([vh[i] * bcol[i], kb[i] * egcol[i]], axis=1) for i in ps]
    sol = [_dot(t[i].astype(BF16), rhs[i].astype(BF16)) for i in ps]
    kh16 = [kh[i].astype(BF16) for i in ps]
    attn = [(_dot_nt(qh[i].astype(BF16), kh16[i]) * decay[i]).astype(BF16) for i in ps]
    qe16 = [(qh[i] * egcol[i]).astype(BF16) for i in ps]
    kd16 = [(kh[i] * eglcol[i]).astype(BF16) for i in ps]
    sh = [s_ref[i // heads, i % heads] for i in ps]
    sh16 = [a.astype(BF16) for a in sh]
    v_new = [sol[i][:, :dv] - _dot(sol[i][:, dv:].astype(BF16), sh16[i]) for i in ps]
    vn16 = [a.astype(BF16) for a in v_new]
    o = [_dot(qe16[i], sh16[i]) + _dot(attn[i], vn16[i]) for i in ps]
    for i in ps:
        s_ref[i // heads, i % heads] = sh[i] * eglast[i] + _dot_tn(kd16[i], vn16[i])
    for i in ps:
        b, h = i // heads, i % heads
        oh = o[i][:cr]
        zh = z_ref[b, :, h * dv:(h + 1) * dv]
        on = oh * lax.rsqrt(jnp.mean(oh * oh, axis=-1, keepdims=True) + EPS) * gain * _silu(zh)
        o_ref[b, :, h * dv:(h + 1) * dv] = on.astype(o_ref.dtype)

    @pl.when(n == pl.num_programs(1) - 1)
    def _():
        sout_ref[...] = s_ref[...]


def _gdn_chunk(qkvz, ba, neg_a, dt_bias, gain, *, s0, batch, seq_len, heads, dk, dv, bt=4):
    m = qkvz.shape[0]
    cr = min(CHUNK, seq_len)
    nc = seq_len // cr
    hd = heads * dk
    bt = math.gcd(batch, bt)
    has_s0 = s0 is not None
    qkvz = qkvz.reshape(batch, seq_len, -1)
    ba = ba.reshape(batch, seq_len, -1)

    def blk(c):
        return pl.BlockSpec((bt, cr, hd), lambda b, n: (b, n, c))

    in_specs = [blk(0), blk(1), blk(2), blk(3),
                pl.BlockSpec((bt, cr, 2 * LANES), lambda b, n: (b, n, 0)),
                pl.BlockSpec((1, LANES), lambda b, n: (0, 0)),
                pl.BlockSpec((1, LANES), lambda b, n: (0, 0)),
                pl.BlockSpec((1, dv), lambda b, n: (0, 0))]
    args = [qkvz, qkvz, qkvz, qkvz, ba, neg_a, dt_bias, gain.reshape(1, dv)]
    if has_s0:
        in_specs.append(pl.BlockSpec((bt, heads, dk, dv), lambda b, n: (b, 0, 0, 0)))
        args.append(s0)
    o, s_fin = pl.pallas_call(
        functools.partial(_gdn_chunk_kernel, heads=heads, dk=dk, dv=dv, has_s0=has_s0),
        grid=(batch // bt, nc),
        in_specs=in_specs,
        out_specs=[pl.BlockSpec((bt, cr, hd), lambda b, n: (b, n, 0)),
                   pl.BlockSpec((bt, heads, dk, dv), lambda b, n: (b, 0, 0, 0))],
        out_shape=[jax.ShapeDtypeStruct((batch, seq_len, hd), BF16),
                   jax.ShapeDtypeStruct((batch, heads, dk, dv), F32)],
        scratch_shapes=[pltpu.VMEM((bt, heads, dk, dv), F32)],
        compiler_params=_cparams("parallel", "arbitrary"),
        name="gdn_chunk",
    )(*args)
    return o.reshape(m, hd), s_fin


def _lambda(lv_ref, lam_init):
    lv = lv_ref[...]
    a = jnp.sum(lv[0:1] * lv[1:2], axis=-1, keepdims=True)
    b = jnp.sum(lv[2:3] * lv[3:4], axis=-1, keepdims=True)
    return jnp.exp(a) - jnp.exp(b) + lam_init


def _flash_kernel(qt_ref, kt_ref, q_ref, k_ref, v_ref, lv_ref, sub_ref, o_ref, qs_ref, m_ref, acc_ref,
                  *, dh, lam_init):
    step = pl.program_id(2)
    qi = qt_ref[step]
    ki = kt_ref[step]
    tq = q_ref.shape[1]
    tk = k_ref.shape[0]
    vd = v_ref.shape[0]

    @pl.when(ki == 0)
    def _():
        q = q_ref[...]
        feat = lax.broadcasted_iota(jnp.int32, q.shape, 0)
        zero = jnp.zeros_like(q)
        qs_ref[:, :tq] = jnp.where(feat < dh, q, zero)
        qs_ref[:, tq:] = jnp.where(feat >= dh, q, zero)
        m_ref[...] = jnp.full_like(m_ref, -jnp.inf)
        acc_ref[...] = jnp.zeros_like(acc_ref)

    def accumulate(masked):
        s = _dot(k_ref[...], qs_ref[...])
        if masked:
            kpos = ki * tk + lax.broadcasted_iota(jnp.int32, s.shape, 0)
            qpos = qi * tq + (lax.broadcasted_iota(jnp.int32, s.shape, 1) & (tq - 1))
            s = jnp.where(kpos <= qpos, s, -jnp.inf)
        m_old = m_ref[...]
        m_new = jnp.maximum(m_old, jnp.max(s, axis=0, keepdims=True))
        alpha = jnp.exp2(m_old - m_new)
        p = jnp.exp2(s - m_new).astype(BF16)
        v_ones = jnp.concatenate([v_ref[...], jnp.ones((acc_ref.shape[0] - vd, tk), BF16)], axis=0)
        acc_ref[...] = alpha * acc_ref[...] + _dot(v_ones, p)
        m_ref[...] = m_new

    def accumulate_diagonal():
        hk, hq = tk // 2, tq // 2
        v_ones = jnp.concatenate([v_ref[...], jnp.ones((acc_ref.shape[0] - vd, tk), BF16)], axis=0)
        s = _dot(k_ref[:hk, :], qs_ref[...])
        kpos = lax.broadcasted_iota(jnp.int32, s.shape, 0)
        qpos = lax.broadcasted_iota(jnp.int32, s.shape, 1) & (tq - 1)
        s = jnp.where(kpos <= qpos, s, -jnp.inf)
        m_old = m_ref[...]
        m_new = jnp.maximum(m_old, jnp.max(s, axis=0, keepdims=True))
        p = jnp.exp2(s - m_new).astype(BF16)
        acc_ref[...] = jnp.exp2(m_old - m_new) * acc_ref[...] + _dot(v_ones[:, :hk], p)
        m_ref[...] = m_new
        k2 = k_ref[hk:, :]
        v2 = v_ones[:, hk:]
        for base in (hq, tq + hq):
            cols = slice(base, base + hq)
            s = _dot(k2, qs_ref[:, cols])
            kpos = lax.broadcasted_iota(jnp.int32, s.shape, 0)
            qpos = lax.broadcasted_iota(jnp.int32, s.shape, 1)
            s = jnp.where(kpos <= qpos, s, -jnp.inf)
            m_old = m_ref[:, cols]
            m_new = jnp.maximum(m_old, jnp.max(s, axis=0, keepdims=True))
            p = jnp.exp2(s - m_new).astype(BF16)
            acc_ref[:, cols] = jnp.exp2(m_old - m_new) * acc_ref[:, cols] + _dot(v2, p)
            m_ref[:, cols] = m_new

    below_diagonal = (ki + 1) * tk - 1 <= qi * tq

    @pl.when(below_diagonal)
    def _():
        accumulate(False)

    @pl.when(jnp.logical_not(below_diagonal))
    def _():
        if tq == tk:
            accumulate_diagonal()
        else:
            accumulate(True)

    @pl.when((ki + 1) * tk == (qi + 1) * tq)
    def _():
        lam = _lambda(lv_ref, lam_init)
        a = acc_ref[:vd] / acc_ref[vd:vd + 1]
        o = a[:, :tq] - lam * a[:, tq:]
        r = lax.rsqrt(jnp.mean(o * o, axis=0, keepdims=True) + EPS)
        o = o * r * (sub_ref[...] * (1.0 - lam_init))
        o_ref[...] = o.T.astype(o_ref.dtype)


def _flash_prompt(q_t, k, v_t, lam_vecs, subln, *, batch, seq_len, dh, vd, lam_init, tq=1024, tk=1024):
    m, width = k.shape
    pairs = width // (2 * dh)
    tq = min(tq, seq_len)
    tk = min(tk, tq)
    nq, nk = seq_len // tq, seq_len // tk
    assert tq & (tq - 1) == 0 and tq % tk == 0
    steps = [(i, j) for i in range(nq) for j in range((i + 1) * tq // tk)]
    q_tab = jnp.asarray([s[0] for s in steps], jnp.int32)
    k_tab = jnp.asarray([s[1] for s in steps], jnp.int32)
    grid_spec = pltpu.PrefetchScalarGridSpec(
        num_scalar_prefetch=2,
        grid=(batch, pairs, len(steps)),
        in_specs=[pl.BlockSpec((2 * dh, tq), lambda b, h, s, qt, kt: (h, b * nq + qt[s])),
                  pl.BlockSpec((tk, 2 * dh), lambda b, h, s, qt, kt: (b * nk + kt[s], h)),
                  pl.BlockSpec((vd, tk), lambda b, h, s, qt, kt: (h, b * nk + kt[s])),
                  pl.BlockSpec(lam_vecs.shape, lambda b, h, s, qt, kt: (0, 0)),
                  pl.BlockSpec((vd, 1), lambda b, h, s, qt, kt: (0, 0))],
        out_specs=pl.BlockSpec((tq, vd), lambda b, h, s, qt, kt: (b * nq + qt[s], h)),
        scratch_shapes=[pltpu.VMEM((2 * dh, 2 * tq), BF16),
                        pltpu.VMEM((1, 2 * tq), F32),
                        pltpu.VMEM((vd + 2 * SUBLANES, 2 * tq), F32)])
    return pl.pallas_call(
        functools.partial(_flash_kernel, dh=dh, lam_init=lam_init),
        grid_spec=grid_spec,
        out_shape=jax.ShapeDtypeStruct((m, pairs * vd), BF16),
        compiler_params=_cparams("parallel", "parallel", "arbitrary"),
        name="diff_flash",
    )(q_tab, k_tab, q_t, k, v_t, lam_vecs, subln.reshape(vd, 1))


def _paged_kernel(*refs, nh, dh, vd, lam_init, group):
    q_ref, kn_ref, vn_ref = refs[1:4]
    kc_refs = refs[4:4 + group]
    vc_refs = refs[4 + group:4 + 2 * group]
    lv_ref, sub_ref, o_ref, qbd_ref, m_ref, l_ref, acc_ref = refs[4 + 2 * group:]
    p = pl.program_id(1)
    t = q_ref.shape[1]
    width = q_ref.shape[2]
    rows = nh * t
    page = kc_refs[0].shape[2]
    nvh = width // vd

    def accum(kt16, v16, mask):
        s = _dot(qbd_ref[...], kt16)
        if mask is not None:
            s = jnp.where(mask, s, -jnp.inf)
        m_old = m_ref[...]
        m_new = jnp.maximum(m_old, jnp.max(s, axis=-1, keepdims=True))
        alpha = jnp.exp(m_old - m_new)
        pr = jnp.exp(s - m_new)
        l_ref[...] = alpha * l_ref[...] + jnp.sum(pr, axis=-1, keepdims=True)
        acc_ref[...] = alpha * acc_ref[...] + _dot(pr.astype(BF16), v16)
        m_ref[...] = m_new

    @pl.when(p == 0)
    def _():
        q = q_ref[0].astype(F32)
        q3 = jnp.broadcast_to(q[None], (nh, t, width))
        hd = lax.broadcasted_iota(jnp.int32, (nh, t, width), 0)
        ln = lax.broadcasted_iota(jnp.int32, (nh, t, width), 2)
        qbd = jnp.where((ln >= hd * dh) & (ln < (hd + 1) * dh), q3, 0.0)
        qbd_ref[...] = qbd.reshape(rows, width).astype(BF16)
        m_ref[...] = jnp.full_like(m_ref, -jnp.inf)
        l_ref[...] = jnp.zeros_like(l_ref)
        acc_ref[...] = jnp.zeros_like(acc_ref)
        zpad = jnp.zeros((page - t, width), F32)
        kt16 = jnp.concatenate([kn_ref[0], zpad], axis=0).T.astype(BF16)
        v16 = jnp.concatenate([vn_ref[0], zpad], axis=0).astype(BF16)
        r = lax.broadcasted_iota(jnp.int32, (rows, page), 0)
        c = lax.broadcasted_iota(jnp.int32, (rows, page), 1)
        accum(kt16, v16, c <= (r & (t - 1)))

    @pl.when(p > 0)
    def _():
        v = jnp.concatenate(
            [jnp.concatenate([vc[0, pl.ds(h, page, stride=nvh), :] for h in range(nvh)], axis=1).astype(BF16)
             for vc in vc_refs], axis=0)
        kt = jnp.concatenate([kc[0].astype(BF16) for kc in kc_refs], axis=1)
        accum(kt, v, None)

    @pl.when(p == pl.num_programs(1) - 1)
    def _():
        lam = _lambda(lv_ref, lam_init)
        r = lax.broadcasted_iota(jnp.int32, (rows, 1), 0)
        odd = ((r // t) & 1) == 1
        wgt = jnp.where(odd, -lam, 1.0) / l_ref[...]
        a3 = (acc_ref[...] * wgt).reshape(nh, t, width)
        hd = lax.broadcasted_iota(jnp.int32, (nh, t, width), 0)
        ln = lax.broadcasted_iota(jnp.int32, (nh, t, width), 2)
        pair = hd >> 1
        o = jnp.sum(jnp.where((ln >= pair * vd) & (ln < (pair + 1) * vd), a3, 0.0), axis=0)
        sub = sub_ref[...]
        for h in range(width // vd):
            oh = o[:, h * vd:(h + 1) * vd]
            o_ref[0, :, h * vd:(h + 1) * vd] = (_rms(oh, sub) * (1.0 - lam_init)).astype(o_ref.dtype)


def _paged_attention(q, k_new, v_new, cache_k, cache_v, page_table, lam_vecs, subln, *, dh, vd, lam_init, group=8):
    b, t, width = q.shape
    npg = page_table.shape[1]
    page = cache_k.shape[2]
    nh = width // dh
    nvh = width // vd
    assert t & (t - 1) == 0 and t <= page and nh == 2 * nvh and cache_v.shape[1] == page * nvh
    group = math.gcd(npg, group)

    def pidx(i):
        return lambda bb, p, pt: (pt[bb * npg + jnp.maximum(p - 1, 0) * group + i], 0, 0)

    grid_spec = pltpu.PrefetchScalarGridSpec(
        num_scalar_prefetch=1,
        grid=(b, npg // group + 1),
        in_specs=[pl.BlockSpec((1, t, width), lambda bb, p, pt: (bb, 0, 0)),
                  pl.BlockSpec((1, t, width), lambda bb, p, pt: (bb, 0, 0)),
                  pl.BlockSpec((1, t, width), lambda bb, p, pt: (bb, 0, 0))]
        + [pl.BlockSpec((1, width, page), pidx(i)) for i in range(group)]
        + [pl.BlockSpec((1, page * nvh, vd), pidx(i)) for i in range(group)]
        + [pl.BlockSpec(lam_vecs.shape, lambda bb, p, pt: (0, 0)),
           pl.BlockSpec((1, vd), lambda bb, p, pt: (0, 0))],
        out_specs=pl.BlockSpec((1, t, width), lambda bb, p, pt: (bb, 0, 0)),
        scratch_shapes=[pltpu.VMEM((nh * t, width), BF16),
                        pltpu.VMEM((nh * t, 1), F32),
                        pltpu.VMEM((nh * t, 1), F32),
                        pltpu.VMEM((nh * t, width), F32)])
    return pl.pallas_call(
        functools.partial(_paged_kernel, nh=nh, dh=dh, vd=vd, lam_init=lam_init, group=group),
        grid_spec=grid_spec,
        out_shape=jax.ShapeDtypeStruct((b, t, width), BF16),
        compiler_params=_cparams("parallel", "arbitrary"),
        name="diff_paged",
    )(page_table.reshape(-1), q, k_new, v_new, *([cache_k] * group), *([cache_v] * group),
      lam_vecs, subln.reshape(1, vd))


def _rot_weight(w, dh):
    k, n = w.shape
    w4 = w.reshape(k, n // dh, 2, dh // 2)
    return jnp.stack([-w4[:, :, 1], w4[:, :, 0]], axis=2).reshape(k, n)


def _rope_tables(pos, dh):
    half = dh // 2
    inv = 1.0 / (ROPE_THETA ** (jnp.arange(half, dtype=F32) / half))
    ang = pos.astype(F32)[:, None] * inv[None, :]
    reps = LANES // half
    return jnp.tile(jnp.cos(ang), (1, reps)), jnp.tile(jnp.sin(ang), (1, reps))


def _pad_rows(a, rows, front):
    pad = [(0, 0)] * a.ndim
    pad[-2] = (rows - a.shape[-2], 0) if front else (0, rows - a.shape[-2])
    return jnp.pad(a, pad)


def _prep_weights(p):
    n_a, d, a_in = p["a_w_in"].shape
    heads = p["a_A_log"].shape[1]
    dv = p["a_o_gain"].shape[1]
    a_vd = heads * dv
    a_qkv = a_in - a_vd - 2 * heads
    dh = p["b_lambda"].shape[-1]
    w = {}
    w_in = p["a_w_in"]
    tn = 512
    zpad = jnp.zeros((n_a, d, LANES - heads), F32)
    w["a_w_all"] = jnp.concatenate(
        [w_in[:, :, :a_qkv + a_vd], w_in[:, :, a_qkv + a_vd:a_qkv + a_vd + heads], zpad,
         w_in[:, :, a_qkv + a_vd + heads:], zpad, jnp.zeros((n_a, d, tn - 2 * LANES), F32)], axis=2).astype(BF16)
    w["a_cw"] = _pad_rows(p["a_conv_w"], SUBLANES, front=False)
    hp = jnp.zeros((n_a, LANES - heads), F32)
    w["a_neg_a"] = jnp.concatenate([-jnp.exp(p["a_A_log"].astype(F32)), hp], axis=1)[:, None, :]
    w["a_dt"] = jnp.concatenate([p["a_dt_bias"].astype(F32), hp], axis=1)[:, None, :]
    w["a_w_out"] = p["a_w_out"].astype(BF16)
    kq = p["w_kv"].shape[1] - (p["b_w_out"].shape[1])
    w["w_k"] = p["w_kv"][:, :kq].astype(BF16)
    w["w_k_rot"] = _rot_weight(p["w_kv"][:, :kq], dh).astype(BF16)
    w["w_v"] = p["w_kv"][:, kq:].astype(BF16)
    w["b_w_q"] = p["b_w_q"].astype(BF16)
    w["b_w_q_rot"] = jnp.stack([_rot_weight(p["b_w_q"][j], dh) for j in range(p["b_w_q"].shape[0])]).astype(BF16)
    w["b_w_out"] = p["b_w_out"].astype(BF16)
    w["w_v_t"] = w["w_v"].T
    w["b_w_q_t"] = jnp.swapaxes(w["b_w_q"], 1, 2)
    w["b_w_q_rot_t"] = jnp.swapaxes(w["b_w_q_rot"], 1, 2)
    f = p["f_w_down"].shape[1]
    w["f_wu"] = p["f_w_up"].astype(BF16)
    w["f_cw"] = _pad_rows(p["f_conv_w"], SUBLANES, front=False)
    w["f_wd"] = p["f_w_down"].astype(BF16)
    return w


def _trunk(x, pos, p, w, *, delta0, dconv0, fconv0, cache_k, cache_v, page_table):
    b, l, d = x.shape
    m = b * l
    sample = page_table is not None
    depth = p["f_norm"].shape[0]
    n_a = p["a_norm"].shape[0]
    heads = p["a_A_log"].shape[1]
    dv = p["a_o_gain"].shape[1]
    dk = (p["a_w_in"].shape[2] - 2 * heads - 2 * heads * dv) // (2 * heads)
    a_vd = heads * dv
    a_qk = heads * dk
    a_qkv = 2 * a_qk + a_vd
    dh = p["b_lambda"].shape[-1]
    vd = p["b_subln"].shape[-1]
    f = p["f_w_down"].shape[1]
    tm = m if sample else min(512, l)
    assert m % tm == 0 and (sample or l % tm == 0)

    h = x.reshape(m, d)
    cos, sin = _rope_tables(pos, dh)
    if sample:
        cos, sin = jnp.tile(cos, (b, 1)), jnp.tile(sin, (b, 1))

    def tails_to_state(tails, rows):
        if sample:
            return tails[:, SUBLANES - rows:, :]
        per = tails.shape[0] // b
        return tails.reshape(b, per, SUBLANES, -1)[:, per - 1, SUBLANES - rows:, :]

    deltas, dconvs, fconvs = [], [], []
    k_new = v_new = k16 = v16_t = None
    for layer in range(depth):
        if layer < n_a:
            st = _pad_rows(dconv0[layer], SUBLANES, front=True) if sample else None
            qkvz, ba, tails = _gdn_in(h, p["a_norm"][layer], w["a_w_all"][layer], w["a_cw"][layer], state=st,
                                      seq_len=l, a_qk=a_qk, a_qkv=a_qkv, a_vd=a_vd, qscale=dk ** -0.5,
                                      tm=tm if sample else min(tm, 256))
            dconvs.append(tails_to_state(tails, p["a_conv_w"].shape[1] - 1))
            o, s_fin = _gdn_chunk(qkvz, ba, w["a_neg_a"][layer], w["a_dt"][layer], p["a_o_gain"][layer],
                                  s0=delta0[layer] if sample else None, batch=b, seq_len=l,
                                  heads=heads, dk=dk, dv=dv)
            deltas.append(s_fin)
            w_o = w["a_w_out"][layer]
        else:
            if layer == n_a:
                if sample:
                    (k_new,) = _norm_proj(h, p["kv_norm"], w["w_k"], w_rot=w["w_k_rot"], cos=cos, sin=sin,
                                          out_dtypes=(F32,), tm=tm)
                    k_out = k_new.reshape(b, l, -1, dh)
                else:
                    k16, k_fm = _norm_proj(h, p["kv_norm"], w["w_k"], w_rot=w["w_k_rot"], cos=cos, sin=sin,
                                           out_dtypes=(BF16,), t_seq=l, tm=tm)
                    k_out = k_fm.reshape(b, -1, dh, l).transpose(0, 3, 1, 2)
                (v_new,) = _norm_proj(h, p["kv_norm"], w["w_v"], out_dtypes=(F32,), tm=tm)
                if not sample:
                    v16_t = _norm_proj_t(h, p["kv_norm"], w["w_v_t"], tm=tm)
            j = layer - n_a
            lam_init = 0.8 - 0.6 * math.exp(-0.3 * layer)
            if sample:
                (q16,) = _norm_proj(h, p["b_norm"][j], w["b_w_q"][j], w_rot=w["b_w_q_rot"][j], cos=cos, sin=sin,
                                    scale=dh ** -0.5, out_dtypes=(BF16,), tm=tm)
                width = q16.shape[1]
                o = _paged_attention(q16.reshape(b, l, width), k_new.reshape(b, l, width), v_new.reshape(b, l, width),
                                     cache_k, cache_v, page_table, p["b_lambda"][j], p["b_subln"][j],
                                     dh=dh, vd=vd, lam_init=lam_init).reshape(m, width)
            else:
                q16_t = _norm_proj_t(h, p["b_norm"][j], w["b_w_q_t"][j], wt_rot=w["b_w_q_rot_t"][j],
                                     cos_t=cos.T, sin_t=sin.T, scale=dh ** -0.5 * math.log2(math.e), tm=tm)
                o = _flash_prompt(q16_t, k16, v16_t, p["b_lambda"][j], p["b_subln"][j], batch=b, seq_len=l,
                                  dh=dh, vd=vd, lam_init=lam_init)
            w_o = w["b_w_out"][j]
        st = _pad_rows(fconv0[layer], SUBLANES, front=True) if sample else None
        h, tails = _conv_ffn(h, o, w_o, p["f_norm"][layer], w["f_wu"][layer], w["f_cw"][layer],
                             w["f_wd"][layer], state=st, final_g=p["final_norm"] if layer == depth - 1 else None,
                             seq_len=l, tm=tm)
        fconvs.append(tails_to_state(tails, p["f_conv_w"].shape[1] - 1))
    return (h.reshape(b, l, d), jnp.stack(deltas), jnp.stack(dconvs), jnp.stack(fconvs),
            k_out, v_new.reshape(b, l, v_new.shape[1] // vd, vd))


def kernel(x_prompt, x_sample, state_delta, state_dconv, state_fconv, cache_k, cache_v, page_table, a_norm, a_w_in, a_conv_w, a_A_log, a_dt_bias, a_o_gain, a_w_out, kv_norm, w_kv, b_norm, b_w_q, b_lambda, b_subln, b_w_out, f_norm, f_w_up, f_conv_w, f_w_down, final_norm):
    p = dict(a_norm=a_norm, a_w_in=a_w_in, a_conv_w=a_conv_w, a_A_log=a_A_log, a_dt_bias=a_dt_bias,
             a_o_gain=a_o_gain, a_w_out=a_w_out, kv_norm=kv_norm, w_kv=w_kv, b_norm=b_norm, b_w_q=b_w_q,
             b_lambda=b_lambda, b_subln=b_subln, b_w_out=b_w_out, f_norm=f_norm, f_w_up=f_w_up,
             f_conv_w=f_conv_w, f_w_down=f_w_down, final_norm=final_norm)
    w = _prep_weights(p)
    lp = x_prompt.shape[1]
    prompt = _trunk(x_prompt, jnp.arange(lp, dtype=jnp.int32), p, w, delta0=None, dconv0=None, fconv0=None,
                    cache_k=None, cache_v=None, page_table=None)
    ls = x_sample.shape[1]
    past_len = page_table.shape[1] * cache_k.shape[1]
    pool, page = cache_k.shape[:2]
    sample = _trunk(x_sample, past_len + jnp.arange(ls, dtype=jnp.int32), p, w, delta0=state_delta,
                    dconv0=state_dconv, fconv0=state_fconv,
                    cache_k=cache_k.transpose(0, 2, 3, 1).reshape(pool, -1, page),
                    cache_v=cache_v.reshape(pool, page * cache_v.shape[2], cache_v.shape[3]),
                    page_table=page_table)
    return (prompt[0], sample[0]) + prompt[1:] + sample[1:]
```

```python
import functools
import math

import jax
import jax.numpy as jnp
from jax import lax
from jax.experimental import pallas as pl
from jax.experimental.pallas import tpu as pltpu

F32 = jnp.float32
BF16 = jnp.bfloat16
EPS = 1e-6
ROPE_THETA = 10000.0
LANES = 128
SUBLANES = 8
CHUNK = 64
VMEM_LIMIT = 48 * 1024 * 1024
HI = lax.Precision.HIGHEST


def _cparams(*sem):
    return pltpu.CompilerParams(dimension_semantics=sem, vmem_limit_bytes=VMEM_LIMIT)


def _dot(a, b):
    return jnp.dot(a, b, preferred_element_type=F32)


def _dot_nt(a, b, precision=None):
    return lax.dot_general(a, b, (((1,), (1,)), ((), ())), precision=precision,
                           preferred_element_type=F32)


def _dot_tn(a, b, precision=None):
    return lax.dot_general(a, b, (((0,), (0,)), ((), ())), precision=precision,
                           preferred_element_type=F32)


def _dot_hi(a, b):
    return jnp.dot(a, b, precision=HI, preferred_element_type=F32)


def _rms(x, g):
    r = lax.rsqrt(jnp.mean(x * x, axis=-1, keepdims=True) + EPS)
    return x * r * g


def _silu(x):
    return x * jax.nn.sigmoid(x)


def _proj_kernel(*refs, rope, scale, n_out, feature_major):
    if rope:
        x_ref, g_ref, w_ref, wr_ref, cos_ref, sin_ref = refs[:6]
        rest = refs[6:]
    else:
        x_ref, g_ref, w_ref = refs[:3]
        rest = refs[3:]
    outs, xn_ref = rest[:n_out], rest[-1]

    @pl.when(pl.program_id(1) == 0)
    def _():
        xn_ref[...] = _rms(x_ref[...], g_ref[...]).astype(BF16)

    xn = xn_ref[...]
    y = _dot(xn, w_ref[...])
    if rope:
        yr = _dot(xn, wr_ref[...])
        reps = y.shape[1] // LANES
        cos = jnp.concatenate([cos_ref[...]] * reps, axis=1)
        sin = jnp.concatenate([sin_ref[...]] * reps, axis=1)
        y = y * cos + yr * sin
    if scale != 1.0:
        y = y * scale
    for o in outs:
        o[...] = y.astype(o.dtype)
    if feature_major:
        o_t = rest[n_out]
        o_t[...] = y.T.astype(o_t.dtype)


def _norm_proj(x, g, w, *, w_rot=None, cos=None, sin=None, scale=1.0, out_dtypes=(F32,), t_seq=None, t_dtype=F32,
               tm, tn=1024):
    m, d = x.shape
    n = w.shape[1]
    tn = min(tn, n)
    rope = w_rot is not None
    out_specs = [pl.BlockSpec((tm, tn), lambda i, j: (i, j)) for _ in out_dtypes]
    out_shape = [jax.ShapeDtypeStruct((m, n), dt) for dt in out_dtypes]
    if t_seq is not None:
        tps = t_seq // tm
        out_specs.append(pl.BlockSpec((None, tn, tm), lambda i, j: (i // tps, j, i % tps)))
        out_shape.append(jax.ShapeDtypeStruct((m // t_seq, n, t_seq), t_dtype))
    in_specs = [pl.BlockSpec((tm, d), lambda i, j: (i, 0)),
                pl.BlockSpec((1, d), lambda i, j: (0, 0)),
                pl.BlockSpec((d, tn), lambda i, j: (0, j))]
    args = [x, g.reshape(1, d), w]
    if rope:
        pt = cos.shape[0] // tm
        in_specs += [pl.BlockSpec((d, tn), lambda i, j: (0, j)),
                     pl.BlockSpec((tm, LANES), lambda i, j: (i % pt, 0)),
                     pl.BlockSpec((tm, LANES), lambda i, j: (i % pt, 0))]
        args += [w_rot, cos, sin]
    res = pl.pallas_call(
        functools.partial(_proj_kernel, rope=rope, scale=scale, n_out=len(out_dtypes),
                          feature_major=t_seq is not None),
        grid=(m // tm, n // tn),
        in_specs=in_specs,
        out_specs=out_specs,
        out_shape=out_shape,
        scratch_shapes=[pltpu.VMEM((tm, d), BF16)],
        compiler_params=_cparams("parallel", "arbitrary"),
        name="norm_proj_rope" if rope else "norm_proj",
    )(*args)
    return res


def _proj_t_kernel(*refs, rope, scale):
    if rope:
        x_ref, g_ref, wt_ref, wrt_ref, cos_ref, sin_ref, o_ref, xn_ref = refs
    else:
        x_ref, g_ref, wt_ref, o_ref, xn_ref = refs

    @pl.when(pl.program_id(1) == 0)
    def _():
        xn_ref[...] = _rms(x_ref[...], g_ref[...]).astype(BF16)

    xn = xn_ref[...]
    y = _dot_nt(wt_ref[...], xn)
    if rope:
        yr = _dot_nt(wrt_ref[...], xn)
        reps = y.shape[0] // LANES
        cos = jnp.concatenate([cos_ref[...]] * reps, axis=0)
        sin = jnp.concatenate([sin_ref[...]] * reps, axis=0)
        y = y * cos + yr * sin
    if scale != 1.0:
        y = y * scale
    o_ref[...] = y.astype(o_ref.dtype)


def _norm_proj_t(x, g, wt, *, wt_rot=None, cos_t=None, sin_t=None, scale=1.0, tm, tn=1024):
    m, d = x.shape
    n = wt.shape[0]
    tn = min(tn, n)
    rope = wt_rot is not None
    in_specs = [pl.BlockSpec((tm, d), lambda i, j: (i, 0)),
                pl.BlockSpec((1, d), lambda i, j: (0, 0)),
                pl.BlockSpec((tn, d), lambda i, j: (j, 0))]
    args = [x, g.reshape(1, d), wt]
    if rope:
        pt = cos_t.shape[1] // tm
        in_specs += [pl.BlockSpec((tn, d), lambda i, j: (j, 0)),
                     pl.BlockSpec((LANES, tm), lambda i, j: (0, i % pt)),
                     pl.BlockSpec((LANES, tm), lambda i, j: (0, i % pt))]
        args += [wt_rot, cos_t, sin_t]
    return pl.pallas_call(
        functools.partial(_proj_t_kernel, rope=rope, scale=scale),
        grid=(m // tm, n // tn),
        in_specs=in_specs,
        out_specs=pl.BlockSpec((tn, tm), lambda i, j: (j, i)),
        out_shape=jax.ShapeDtypeStruct((n, m), BF16),
        scratch_shapes=[pltpu.VMEM((tm, d), BF16)],
        compiler_params=_cparams("parallel", "arbitrary"),
        name="norm_proj_t_rope" if rope else "norm_proj_t",
    )(*args)


def _resident(shape):
    return pl.BlockSpec(shape, lambda *_: (0,) * len(shape), pipeline_mode=pl.Buffered(1))


def _ffn_kernel(*refs, per_seq, tiles_per_seq, tf, final):
    x_ref, mix_ref, wo_ref, g_ref, wu_ref, cw_ref, wd_ref = refs[:7]
    pos = 7
    st_ref = fg_ref = carry_ref = None
    if per_seq:
        st_ref = refs[pos]
        pos += 1
    if final:
        fg_ref = refs[pos]
        pos += 1
    o_ref, tail_ref, act_ref, pad_ref = refs[pos:pos + 4]
    if not per_seq:
        carry_ref = refs[pos + 4]
    i = pl.program_id(0)
    tm = x_ref.shape[0]
    f = wd_ref.shape[0]
    nf = f // tf
    x = x_ref[...] + _dot(mix_ref[...], wo_ref[...])
    xn = _rms(x, g_ref[...]).astype(BF16)

    if not per_seq:
        @pl.when((i % tiles_per_seq) == 0)
        def _():
            carry_ref[...] = jnp.zeros_like(carry_ref)

    for j in range(nf):
        cols = slice(j * tf, (j + 1) * tf)
        gate = _dot(xn, wu_ref[:, cols])
        val = _dot(xn, wu_ref[:, f + j * tf:f + (j + 1) * tf])
        cw = cw_ref[:, cols]
        pad = pad_ref.at[j % 2]
        if per_seq:
            g3 = gate.reshape(tm // SUBLANES, SUBLANES, tf)
            pad[:, :SUBLANES, :] = st_ref[:, :, cols]
            pad[:, SUBLANES:, :] = g3
            tail_ref[:, :, cols] = g3
            conv = (cw[2:3][None] * g3 + cw[1:2][None] * pad[:, SUBLANES - 1:2 * SUBLANES - 1, :]
                    + cw[0:1][None] * pad[:, SUBLANES - 2:2 * SUBLANES - 2, :])
        else:
            tail = gate[tm - SUBLANES:]
            pad[:SUBLANES, :] = carry_ref[:, cols]
            pad[SUBLANES:, :] = gate
            tail_ref[0, :, cols] = tail
            carry_ref[:, cols] = tail
            conv = (cw[2:3] * gate + cw[1:2] * pad[pl.ds(SUBLANES - 1, tm), :]
                    + cw[0:1] * pad[pl.ds(SUBLANES - 2, tm), :])
        act_ref[:, cols] = (_silu(conv).reshape(tm, tf) * val).astype(BF16)

    y = x + _dot(act_ref[...], wd_ref[...])
    if final:
        y = _rms(y, fg_ref[...])
    o_ref[...] = y


def _conv_ffn(x, mix, w_o, g, wu, cw, wd, *, state=None, final_g=None, seq_len, tm, tf=256):
    m, d = x.shape
    f = wd.shape[0]
    per_seq = state is not None
    final = final_g is not None
    groups = tm // SUBLANES if per_seq else 1
    in_specs = [pl.BlockSpec((tm, d), lambda i: (i, 0)),
                pl.BlockSpec((tm, mix.shape[1]), lambda i: (i, 0)), _resident(w_o.shape),
                _resident((1, d)), _resident((d, 2 * f)), _resident((SUBLANES, f)), _resident((f, d))]
    args = [x, mix, w_o, g.reshape(1, d), wu, cw, wd]
    if per_seq:
        in_specs.append(pl.BlockSpec((groups, SUBLANES, f), lambda i: (i, 0, 0)))
        args.append(state)
    if final:
        in_specs.append(_resident((1, d)))
        args.append(final_g.reshape(1, d))
    if per_seq:
        scratch = [pltpu.VMEM((tm, f), BF16), pltpu.VMEM((2, groups, 2 * SUBLANES, tf), F32)]
    else:
        scratch = [pltpu.VMEM((tm, f), BF16), pltpu.VMEM((2, SUBLANES + tm, tf), F32), pltpu.VMEM((SUBLANES, f), F32)]
    return pl.pallas_call(
        functools.partial(_ffn_kernel, per_seq=per_seq, tiles_per_seq=max(seq_len // tm, 1), tf=tf, final=final),
        grid=(m // tm,),
        in_specs=in_specs,
        out_specs=[pl.BlockSpec((tm, d), lambda i: (i, 0)),
                   pl.BlockSpec((groups, SUBLANES, f), lambda i: (i, 0, 0))],
        out_shape=[jax.ShapeDtypeStruct((m, d), F32),
                   jax.ShapeDtypeStruct((m // tm * groups, SUBLANES, f), F32)],
        scratch_shapes=scratch,
        compiler_params=_cparams("arbitrary"),
        name="conv_ffn",
    )(*args)


def _gdn_in_kernel(*refs, per_seq, tiles_per_seq, tn, nq, nqkv, nz, qscale):
    x_ref, g_ref, w_ref, cw_ref = refs[:4]
    pos = 4
    st_ref = carry_ref = None
    if per_seq:
        st_ref = refs[pos]
        pos += 1
    o_ref, ba_ref, tail_ref, pad_ref = refs[pos:pos + 4]
    if not per_seq:
        carry_ref = refs[pos + 4]
    i = pl.program_id(0)
    tm = x_ref.shape[0]
    xn = _rms(x_ref[...], g_ref[...]).astype(BF16)

    if not per_seq:
        @pl.when((i % tiles_per_seq) == 0)
        def _():
            carry_ref[...] = jnp.zeros_like(carry_ref)

    def l2n(y, s):
        parts = []
        for a in range(tn // LANES):
            ya = y[:, a * LANES:(a + 1) * LANES]
            r = lax.rsqrt(jnp.sum(ya * ya, axis=-1, keepdims=True) + EPS)
            parts.append(ya * (r * s) if s != 1.0 else ya * r)
        return jnp.concatenate(parts, axis=1)

    for j in range(nqkv + nz + 1):
        cols = slice(j * tn, (j + 1) * tn)
        pre = _dot(xn, w_ref[:, cols])
        if j == nqkv + nz:
            ba_ref[...] = pre[:, :2 * LANES]
        elif j >= nqkv:
            o_ref[:, cols] = pre
        else:
            cw = cw_ref[:, cols]
            pad = pad_ref.at[j % 2]
            if per_seq:
                p3 = pre.reshape(tm // SUBLANES, SUBLANES, tn)
                pad[:, :SUBLANES, :] = st_ref[:, :, cols]
                pad[:, SUBLANES:, :] = p3
                tail_ref[:, :, cols] = p3
                conv = cw[3:4][None] * p3
                for t in range(3):
                    conv = conv + cw[t:t + 1][None] * pad[:, SUBLANES - 3 + t:2 * SUBLANES - 3 + t, :]
            else:
                tail = pre[tm - SUBLANES:]
                pad[:SUBLANES, :] = carry_ref[:, cols]
                pad[SUBLANES:, :] = pre
                tail_ref[0, :, cols] = tail
                carry_ref[:, cols] = tail
                conv = cw[3:4] * pre
                for t in range(3):
                    conv = conv + cw[t:t + 1] * pad[pl.ds(SUBLANES - 3 + t, tm), :]
            y = _silu(conv).reshape(tm, tn)
            if j < nq:
                y = l2n(y, qscale)
            elif j < 2 * nq:
                y = l2n(y, 1.0)
            o_ref[:, cols] = y


def _gdn_in(x, g, w_all, cw, *, state=None, seq_len, a_qk, a_qkv, a_vd, qscale, tm, tn=512):
    m, d = x.shape
    per_seq = state is not None
    nq, nqkv, nz = a_qk // tn, a_qkv // tn, a_vd // tn
    groups = tm // SUBLANES if per_seq else 1
    in_specs = [pl.BlockSpec((tm, d), lambda i: (i, 0)),
                _resident((1, d)), _resident(w_all.shape), _resident((SUBLANES, a_qkv))]
    args = [x, g.reshape(1, d), w_all, cw]
    if per_seq:
        in_specs.append(pl.BlockSpec((groups, SUBLANES, a_qkv), lambda i: (i, 0, 0)))
        args.append(state)
    if per_seq:
        scratch = [pltpu.VMEM((2, groups, 2 * SUBLANES, tn), F32)]
    else:
        scratch = [pltpu.VMEM((2, SUBLANES + tm, tn), F32), pltpu.VMEM((SUBLANES, a_qkv), F32)]
    return pl.pallas_call(
        functools.partial(_gdn_in_kernel, per_seq=per_seq, tiles_per_seq=max(seq_len // tm, 1),
                          tn=tn, nq=nq, nqkv=nqkv, nz=nz, qscale=qscale),
        grid=(m // tm,),
        in_specs=in_specs,
        out_specs=[pl.BlockSpec((tm, a_qkv + a_vd), lambda i: (i, 0)),
                   pl.BlockSpec((tm, 2 * LANES), lambda i: (i, 0)),
                   pl.BlockSpec((groups, SUBLANES, a_qkv), lambda i: (i, 0, 0))],
        out_shape=[jax.ShapeDtypeStruct((m, a_qkv + a_vd), F32),
                   jax.ShapeDtypeStruct((m, 2 * LANES), F32),
                   jax.ShapeDtypeStruct((m // tm * groups, SUBLANES, a_qkv), F32)],
        scratch_shapes=scratch,
        compiler_params=_cparams("arbitrary"),
        name="gdn_in",
    )(*args)


def _split(a):
    hi = a.astype(BF16)
    return hi, (a - hi.astype(F32)).astype(BF16)


def _dot3(a, b, nt=False):
    f = _dot_nt if nt else _dot
    return f(a[0], b[0]) + (f(a[0], b[1]) + f(a[1], b[0]))


def _unit_lower_inverses(lms, row, col, eye):
    size = lms[0].shape[0]
    def same(s):
        return (row >> s) == (col >> s)

    def dot1(a, b):
        return _dot(a.astype(BF16), b.astype(BF16))

    nd = [jnp.where(same(3), -lm, 0.0) for lm in lms]
    nd2 = [dot1(a, a) for a in nd]
    nd4 = [dot1(a, a) for a in nd2]
    x = [dot1(eye + a, eye + b) for a, b in zip(nd, nd2)]
    x = [dot1(a, eye + b) for a, b in zip(x, nd4)]
    for s in range(3, size.bit_length() - 1):
        mask = same(s + 1) & jnp.logical_not(same(s))
        y = [dot1(a, jnp.where(mask, lm, 0.0)) for a, lm in zip(x, lms)]
        x = [a - dot1(b, a) for a, b in zip(x, y)]
    return x


def _gdn_chunk_kernel(*refs, heads, dk, dv, has_s0):
    q_ref, k_ref, v_ref, z_ref, ba_ref, na_ref, dt_ref, gain_ref = refs[:8]
    pos = 8
    s0_ref = None
    if has_s0:
        s0_ref = refs[pos]
        pos += 1
    o_ref, sout_ref, s_ref = refs[pos:pos + 3]
    n = pl.program_id(1)

    @pl.when(n == 0)
    def _():
        if has_s0:
            s_ref[...] = s0_ref[...]
        else:
            s_ref[...] = jnp.zeros_like(s_ref)

    bt, cr = q_ref.shape[0], q_ref.shape[1]
    c = max(cr, 2 * SUBLANES)

    def pad(a):
        if cr == c:
            return a
        return jnp.concatenate([a, jnp.zeros((c - cr, a.shape[1]), a.dtype)], axis=0)

    row = lax.broadcasted_iota(jnp.int32, (c, c), 0)
    col = lax.broadcasted_iota(jnp.int32, (c, c), 1)
    lower = row >= col
    strict = row > col
    eye = (row == col).astype(F32)
    tri = lower.astype(F32)
    gain = gain_ref[...]

    qh, kh, vh, bcol, egcol, eglcol, eglast, decay = [], [], [], [], [], [], [], []
    for b in range(bt):
        ba = ba_ref[b]
        beta = pad(jax.nn.sigmoid(ba[:, :LANES]))
        g = pad(na_ref[...] * jax.nn.softplus(ba[:, LANES:] + dt_ref[...]))
        q, k, v = pad(q_ref[b]), pad(k_ref[b]), pad(v_ref[b])
        gc = _dot_hi(tri, g)
        gct = gc.T
        glast = gc[c - 1:c]
        eg = jnp.exp(gc)
        egl = jnp.exp(glast - gc)
        egt = jnp.exp(glast)
        for h in range(heads):
            qh.append(q[:, h * dk:(h + 1) * dk])
            kh.append(k[:, h * dk:(h + 1) * dk])
            vh.append(v[:, h * dv:(h + 1) * dv])
            bcol.append(beta[:, h:h + 1])
            egcol.append(eg[:, h:h + 1])
            eglcol.append(egl[:, h:h + 1])
            eglast.append(egt[:, h:h + 1])
            decay.append(jnp.where(lower, jnp.exp(gc[:, h:h + 1] - gct[h:h + 1, :]), 0.0))
    ps = range(bt * heads)
    kb = [kh[i] * bcol[i] for i in ps]
    kk = [_dot3(_split(kb[i]), _split(kh[i]), nt=True) for i in ps]
    lm = [jnp.where(strict, kk[i] * decay[i], 0.0) for i in ps]
    t = _unit_lower_inverses(lm, row, col, eye)
    rhs = [jnp.concatenate([vh[i] * bcol[i], kb[i] * egcol[i]], axis=1) for i in ps]
    sol = [_dot(t[i].astype(BF16), rhs[i].astype(BF16)) for i in ps]
    kh16 = [kh[i].astype(BF16) for i in ps]
    attn = [(_dot_nt(qh[i].astype(BF16), kh16[i]) * decay[i]).astype(BF16) for i in ps]
    qe16 = [(qh[i] * egcol[i]).astype(BF16) for i in ps]
    kd16 = [(kh[i] * eglcol[i]).astype(BF16) for i in ps]
    sh = [s_ref[i // heads, i % heads] for i in ps]
    sh16 = [a.astype(BF16) for a in sh]
    v_new = [sol[i][:, :dv] - _dot(sol[i][:, dv:].astype(BF16), sh16[i]) for i in ps]
    vn16 = [a.astype(BF16) for a in v_new]
    o = [_dot(qe16[i], sh16[i]) + _dot(attn[i], vn16[i]) for i in ps]
    for i in ps:
        s_ref[i // heads, i % heads] = sh[i] * eglast[i] + _dot_tn(kd16[i], vn16[i])
    for i in ps:
        b, h = i // heads, i % heads
        oh = o[i][:cr]
        zh = z_ref[b, :, h * dv:(h + 1) * dv]
        on = oh * lax.rsqrt(jnp.mean(oh * oh, axis=-1, keepdims=True) + EPS) * gain * _silu(zh)
        o_ref[b, :, h * dv:(h + 1) * dv] = on.astype(o_ref.dtype)

    @pl.when(n == pl.num_programs(1) - 1)
    def _():
        sout_ref[...] = s_ref[...]


def _gdn_chunk(qkvz, ba, neg_a, dt_bias, gain, *, s0, batch, seq_len, heads, dk, dv, bt=4):
    m = qkvz.shape[0]
    cr = min(CHUNK, seq_len)
    nc = seq_len // cr
    hd = heads * dk
    bt = math.gcd(batch, bt)
    has_s0 = s0 is not None
    qkvz = qkvz.reshape(batch, seq_len, -1)
    ba = ba.reshape(batch, seq_len, -1)

    def blk(c):
        return pl.BlockSpec((bt, cr, hd), lambda b, n: (b, n, c))

    in_specs = [blk(0), blk(1), blk(2), blk(3),
                pl.BlockSpec((bt, cr, 2 * LANES), lambda b, n: (b, n, 0)),
                pl.BlockSpec((1, LANES), lambda b, n: (0, 0)),
                pl.BlockSpec((1, LANES), lambda b, n: (0, 0)),
                pl.BlockSpec((1, dv), lambda b, n: (0, 0))]
    args = [qkvz, qkvz, qkvz, qkvz, ba, neg_a, dt_bias, gain.reshape(1, dv)]
    if has_s0:
        in_specs.append(pl.BlockSpec((bt, heads, dk, dv), lambda b, n: (b, 0, 0, 0)))
        args.append(s0)
    o, s_fin = pl.pallas_call(
        functools.partial(_gdn_chunk_kernel, heads=heads, dk=dk, dv=dv, has_s0=has_s0),
        grid=(batch // bt, nc),
        in_specs=in_specs,
        out_specs=[pl.BlockSpec((bt, cr, hd), lambda b, n: (b, n, 0)),
                   pl.BlockSpec((bt, heads, dk, dv), lambda b, n: (b, 0, 0, 0))],
        out_shape=[jax.ShapeDtypeStruct((batch, seq_len, hd), BF16),
                   jax.ShapeDtypeStruct((batch, heads, dk, dv), F32)],
        scratch_shapes=[pltpu.VMEM((bt, heads, dk, dv), F32)],
        compiler_params=_cparams("parallel", "arbitrary"),
        name="gdn_chunk",
    )(*args)
    return o.reshape(m, hd), s_fin


def _lambda(lv_ref, lam_init):
    lv = lv_ref[...]
    a = jnp.sum(lv[0:1] * lv[1:2], axis=-1, keepdims=True)
    b = jnp.sum(lv[2:3] * lv[3:4], axis=-1, keepdims=True)
    return jnp.exp(a) - jnp.exp(b) + lam_init


def _flash_kernel(qt_ref, kt_ref, q_ref, k_ref, v_ref, lv_ref, sub_ref, o_ref, qs_ref, m_ref, acc_ref,
                  *, dh, lam_init):
    step = pl.program_id(2)
    qi = qt_ref[step]
    ki = kt_ref[step]
    tq = q_ref.shape[1]
    tk = k_ref.shape[0]
    vd = v_ref.shape[0]

    @pl.when(ki == 0)
    def _():
        q = q_ref[...]
        feat = lax.broadcasted_iota(jnp.int32, q.shape, 0)
        zero = jnp.zeros_like(q)
        qs_ref[:, :tq] = jnp.where(feat < dh, q, zero)
        qs_ref[:, tq:] = jnp.where(feat >= dh, q, zero)
        m_ref[...] = jnp.full_like(m_ref, -jnp.inf)
        acc_ref[...] = jnp.zeros_like(acc_ref)

    def accumulate(masked):
        s = _dot(k_ref[...], qs_ref[...])
        if masked:
            kpos = ki * tk + lax.broadcasted_iota(jnp.int32, s.shape, 0)
            qpos = qi * tq + (lax.broadcasted_iota(jnp.int32, s.shape, 1) & (tq - 1))
            s = jnp.where(kpos <= qpos, s, -jnp.inf)
        m_old = m_ref[...]
        m_new = jnp.maximum(m_old, jnp.max(s, axis=0, keepdims=True))
        alpha = jnp.exp2(m_old - m_new)
        p = jnp.exp2(s - m_new).astype(BF16)
        v_ones = jnp.concatenate([v_ref[...], jnp.ones((acc_ref.shape[0] - vd, tk), BF16)], axis=0)
        acc_ref[...] = alpha * acc_ref[...] + _dot(v_ones, p)
        m_ref[...] = m_new

    def accumulate_diagonal():
        hk, hq = tk // 2, tq // 2
        v_ones = jnp.concatenate([v_ref[...], jnp.ones((acc_ref.shape[0] - vd, tk), BF16)], axis=0)
        s = _dot(k_ref[:hk, :], qs_ref[...])
        kpos = lax.broadcasted_iota(jnp.int32, s.shape, 0)
        qpos = lax.broadcasted_iota(jnp.int32, s.shape, 1) & (tq - 1)
        s = jnp.where(kpos <= qpos, s, -jnp.inf)
        m_old = m_ref[...]
        m_new = jnp.maximum(m_old, jnp.max(s, axis=0, keepdims=True))
        p = jnp.exp2(s - m_new).astype(BF16)
        acc_ref[...] = jnp.exp2(m_old - m_new) * acc_ref[...] + _dot(v_ones[:, :hk], p)
        m_ref[...] = m_new
        k2 = k_ref[hk:, :]
        v2 = v_ones[:, hk:]
        for base in (hq, tq + hq):
            cols = slice(base, base + hq)
            s = _dot(k2, qs_ref[:, cols])
            kpos = lax.broadcasted_iota(jnp.int32, s.shape, 0)
            qpos = lax.broadcasted_iota(jnp.int32, s.shape, 1)
            s = jnp.where(kpos <= qpos, s, -jnp.inf)
            m_old = m_ref[:, cols]
            m_new = jnp.maximum(m_old, jnp.max(s, axis=0, keepdims=True))
            p = jnp.exp2(s - m_new).astype(BF16)
            acc_ref[:, cols] = jnp.exp2(m_old - m_new) * acc_ref[:, cols] + _dot(v2, p)
            m_ref[:, cols] = m_new

    below_diagonal = (ki + 1) * tk - 1 <= qi * tq

    @pl.when(below_diagonal)
    def _():
        accumulate(False)

    @pl.when(jnp.logical_not(below_diagonal))
    def _():
        if tq == tk:
            accumulate_diagonal()
        else:
            accumulate(True)

    @pl.when((ki + 1) * tk == (qi + 1) * tq)
    def _():
        lam = _lambda(lv_ref, lam_init)
        a = acc_ref[:vd] / acc_ref[vd:vd + 1]
        o = a[:, :tq] - lam * a[:, tq:]
        r = lax.rsqrt(jnp.mean(o * o, axis=0, keepdims=True) + EPS)
        o = o * r * (sub_ref[...] * (1.0 - lam_init))
        o_ref[...] = o.T.astype(o_ref.dtype)


def _flash_prompt(q_t, k, v_t, lam_vecs, subln, *, batch, seq_len, dh, vd, lam_init, tq=1024, tk=1024):
    m, width = k.shape
    pairs = width // (2 * dh)
    tq = min(tq, seq_len)
    tk = min(tk, tq)
    nq, nk = seq_len // tq, seq_len // tk
    assert tq & (tq - 1) == 0 and tq % tk == 0
    steps = [(i, j) for i in range(nq) for j in range((i + 1) * tq // tk)]
    q_tab = jnp.asarray([s[0] for s in steps], jnp.int32)
    k_tab = jnp.asarray([s[1] for s in steps], jnp.int32)
    grid_spec = pltpu.PrefetchScalarGridSpec(
        num_scalar_prefetch=2,
        grid=(batch, pairs, len(steps)),
        in_specs=[pl.BlockSpec((2 * dh, tq), lambda b, h, s, qt, kt: (h, b * nq + qt[s])),
                  pl.BlockSpec((tk, 2 * dh), lambda b, h, s, qt, kt: (b * nk + kt[s], h)),
                  pl.BlockSpec((None, vd, tk), lambda b, h, s, qt, kt: (b, h, kt[s])),
                  pl.BlockSpec(lam_vecs.shape, lambda b, h, s, qt, kt: (0, 0)),
                  pl.BlockSpec((vd, 1), lambda b, h, s, qt, kt: (0, 0))],
        out_specs=pl.BlockSpec((tq, vd), lambda b, h, s, qt, kt: (b * nq + qt[s], h)),
        scratch_shapes=[pltpu.VMEM((2 * dh, 2 * tq), BF16),
                        pltpu.VMEM((1, 2 * tq), F32),
                        pltpu.VMEM((vd + 2 * SUBLANES, 2 * tq), F32)])
    return pl.pallas_call(
        functools.partial(_flash_kernel, dh=dh, lam_init=lam_init),
        grid_spec=grid_spec,
        out_shape=jax.ShapeDtypeStruct((m, pairs * vd), BF16),
        compiler_params=_cparams("parallel", "parallel", "arbitrary"),
        name="diff_flash",
    )(q_tab, k_tab, q_t, k, v_t, lam_vecs, subln.reshape(vd, 1))


def _paged_kernel(*refs, nh, dh, vd, lam_init, group):
    q_ref, kn_ref, vn_ref = refs[1:4]
    kc_refs = refs[4:4 + group]
    vc_refs = refs[4 + group:4 + 2 * group]
    lv_ref, sub_ref, o_ref, qbd_ref, m_ref, l_ref, acc_ref = refs[4 + 2 * group:]
    p = pl.program_id(1)
    t = q_ref.shape[1]
    width = q_ref.shape[2]
    rows = nh * t
    page = kc_refs[0].shape[2]
    nvh = width // vd

    def accum(kt16, v16, mask):
        s = _dot(qbd_ref[...], kt16)
        if mask is not None:
            s = jnp.where(mask, s, -jnp.inf)
        m_old = m_ref[...]
        m_new = jnp.maximum(m_old, jnp.max(s, axis=-1, keepdims=True))
        alpha = jnp.exp(m_old - m_new)
        pr = jnp.exp(s - m_new)
        l_ref[...] = alpha * l_ref[...] + jnp.sum(pr, axis=-1, keepdims=True)
        acc_ref[...] = alpha * acc_ref[...] + _dot(pr.astype(BF16), v16)
        m_ref[...] = m_new

    @pl.when(p == 0)
    def _():
        q = q_ref[0].astype(F32)
        q3 = jnp.broadcast_to(q[None], (nh, t, width))
        hd = lax.broadcasted_iota(jnp.int32, (nh, t, width), 0)
        ln = lax.broadcasted_iota(jnp.int32, (nh, t, width), 2)
        qbd = jnp.where((ln >= hd * dh) & (ln < (hd + 1) * dh), q3, 0.0)
        qbd_ref[...] = qbd.reshape(rows, width).astype(BF16)
        m_ref[...] = jnp.full_like(m_ref, -jnp.inf)
        l_ref[...] = jnp.zeros_like(l_ref)
        acc_ref[...] = jnp.zeros_like(acc_ref)
        zpad = jnp.zeros((page - t, width), F32)
        kt16 = jnp.concatenate([kn_ref[0], zpad], axis=0).T.astype(BF16)
        v16 = jnp.concatenate([vn_ref[0], zpad], axis=0).astype(BF16)
        r = lax.broadcasted_iota(jnp.int32, (rows, page), 0)
        c = lax.broadcasted_iota(jnp.int32, (rows, page), 1)
        accum(kt16, v16, c <= (r & (t - 1)))

    @pl.when(p > 0)
    def _():
        v = jnp.concatenate(
            [jnp.concatenate([vc[0, pl.ds(h, page, stride=nvh), :] for h in range(nvh)], axis=1).astype(BF16)
             for vc in vc_refs], axis=0)
        kt = jnp.concatenate([kc[0].astype(BF16) for kc in kc_refs], axis=1)
        accum(kt, v, None)

    @pl.when(p == pl.num_programs(1) - 1)
    def _():
        lam = _lambda(lv_ref, lam_init)
        r = lax.broadcasted_iota(jnp.int32, (rows, 1), 0)
        odd = ((r // t) & 1) == 1
        wgt = jnp.where(odd, -lam, 1.0) / l_ref[...]
        a3 = (acc_ref[...] * wgt).reshape(nh, t, width)
        hd = lax.broadcasted_iota(jnp.int32, (nh, t, width), 0)
        ln = lax.broadcasted_iota(jnp.int32, (nh, t, width), 2)
        pair = hd >> 1
        o = jnp.sum(jnp.where((ln >= pair * vd) & (ln < (pair + 1) * vd), a3, 0.0), axis=0)
        sub = sub_ref[...]
        for h in range(width // vd):
            oh = o[:, h * vd:(h + 1) * vd]
            o_ref[0, :, h * vd:(h + 1) * vd] = (_rms(oh, sub) * (1.0 - lam_init)).astype(o_ref.dtype)


def _paged_attention(q, k_new, v_new, cache_k, cache_v, page_table, lam_vecs, subln, *, dh, vd, lam_init, group=8):
    b, t, width = q.shape
    npg = page_table.shape[1]
    page = cache_k.shape[2]
    nh = width // dh
    nvh = width // vd
    assert t & (t - 1) == 0 and t <= page and nh == 2 * nvh and cache_v.shape[1] == page * nvh
    group = math.gcd(npg, group)

    def pidx(i):
        return lambda bb, p, pt: (pt[bb * npg + jnp.maximum(p - 1, 0) * group + i], 0, 0)

    grid_spec = pltpu.PrefetchScalarGridSpec(
        num_scalar_prefetch=1,
        grid=(b, npg // group + 1),
        in_specs=[pl.BlockSpec((1, t, width), lambda bb, p, pt: (bb, 0, 0)),
                  pl.BlockSpec((1, t, width), lambda bb, p, pt: (bb, 0, 0)),
                  pl.BlockSpec((1, t, width), lambda bb, p, pt: (bb, 0, 0))]
        + [pl.BlockSpec((1, width, page), pidx(i)) for i in range(group)]
        + [pl.BlockSpec((1, page * nvh, vd), pidx(i)) for i in range(group)]
        + [pl.BlockSpec(lam_vecs.shape, lambda bb, p, pt: (0, 0)),
           pl.BlockSpec((1, vd), lambda bb, p, pt: (0, 0))],
        out_specs=pl.BlockSpec((1, t, width), lambda bb, p, pt: (bb, 0, 0)),
        scratch_shapes=[pltpu.VMEM((nh * t, width), BF16),
                        pltpu.VMEM((nh * t, 1), F32),
                        pltpu.VMEM((nh * t, 1), F32),
                        pltpu.VMEM((nh * t, width), F32)])
    return pl.pallas_call(
        functools.partial(_paged_kernel, nh=nh, dh=dh, vd=vd, lam_init=lam_init, group=group),
        grid_spec=grid_spec,
        out_shape=jax.ShapeDtypeStruct((b, t, width), BF16),
        compiler_params=_cparams("parallel", "arbitrary"),
        name="diff_paged",
    )(page_table.reshape(-1), q, k_new, v_new, *([cache_k] * group), *([cache_v] * group),
      lam_vecs, subln.reshape(1, vd))


def _rot_weight(w, dh):
    k, n = w.shape
    w4 = w.reshape(k, n // dh, 2, dh // 2)
    return jnp.stack([-w4[:, :, 1], w4[:, :, 0]], axis=2).reshape(k, n)


def _rope_tables(pos, dh):
    half = dh // 2
    inv = 1.0 / (ROPE_THETA ** (jnp.arange(half, dtype=F32) / half))
    ang = pos.astype(F32)[:, None] * inv[None, :]
    reps = LANES // half
    return jnp.tile(jnp.cos(ang), (1, reps)), jnp.tile(jnp.sin(ang), (1, reps))


def _pad_rows(a, rows, front):
    pad = [(0, 0)] * a.ndim
    pad[-2] = (rows - a.shape[-2], 0) if front else (0, rows - a.shape[-2])
    return jnp.pad(a, pad)


def _prep_weights(p):
    n_a, d, a_in = p["a_w_in"].shape
    heads = p["a_A_log"].shape[1]
    dv = p["a_o_gain"].shape[1]
    a_vd = heads * dv
    a_qkv = a_in - a_vd - 2 * heads
    dh = p["b_lambda"].shape[-1]
    w = {}
    w_in = p["a_w_in"]
    tn = 512
    zpad = jnp.zeros((n_a, d, LANES - heads), F32)
    w["a_w_all"] = jnp.concatenate(
        [w_in[:, :, :a_qkv + a_vd], w_in[:, :, a_qkv + a_vd:a_qkv + a_vd + heads], zpad,
         w_in[:, :, a_qkv + a_vd + heads:], zpad, jnp.zeros((n_a, d, tn - 2 * LANES), F32)], axis=2).astype(BF16)
    w["a_cw"] = _pad_rows(p["a_conv_w"], SUBLANES, front=False)
    hp = jnp.zeros((n_a, LANES - heads), F32)
    w["a_neg_a"] = jnp.concatenate([-jnp.exp(p["a_A_log"].astype(F32)), hp], axis=1)[:, None, :]
    w["a_dt"] = jnp.concatenate([p["a_dt_bias"].astype(F32), hp], axis=1)[:, None, :]
    w["a_w_out"] = p["a_w_out"].astype(BF16)
    kq = p["w_kv"].shape[1] - (p["b_w_out"].shape[1])
    w["w_k"] = p["w_kv"][:, :kq].astype(BF16)
    w["w_k_rot"] = _rot_weight(p["w_kv"][:, :kq], dh).astype(BF16)
    w["w_v"] = p["w_kv"][:, kq:].astype(BF16)
    w["b_w_q"] = p["b_w_q"].astype(BF16)
    w["b_w_q_rot"] = jnp.stack([_rot_weight(p["b_w_q"][j], dh) for j in range(p["b_w_q"].shape[0])]).astype(BF16)
    w["b_w_out"] = p["b_w_out"].astype(BF16)
    w["b_w_q_t"] = jnp.swapaxes(w["b_w_q"], 1, 2)
    w["b_w_q_rot_t"] = jnp.swapaxes(w["b_w_q_rot"], 1, 2)
    f = p["f_w_down"].shape[1]
    w["f_wu"] = p["f_w_up"].astype(BF16)
    w["f_cw"] = _pad_rows(p["f_conv_w"], SUBLANES, front=False)
    w["f_wd"] = p["f_w_down"].astype(BF16)
    return w


def _trunk(x, pos, p, w, *, delta0, dconv0, fconv0, cache_k, cache_v, page_table):
    b, l, d = x.shape
    m = b * l
    sample = page_table is not None
    depth = p["f_norm"].shape[0]
    n_a = p["a_norm"].shape[0]
    heads = p["a_A_log"].shape[1]
    dv = p["a_o_gain"].shape[1]
    dk = (p["a_w_in"].shape[2] - 2 * heads - 2 * heads * dv) // (2 * heads)
    a_vd = heads * dv
    a_qk = heads * dk
    a_qkv = 2 * a_qk + a_vd
    dh = p["b_lambda"].shape[-1]
    vd = p["b_subln"].shape[-1]
    f = p["f_w_down"].shape[1]
    tm = m if sample else min(512, l)
    assert m % tm == 0 and (sample or l % tm == 0)

    h = x.reshape(m, d)
    cos, sin = _rope_tables(pos, dh)
    if sample:
        cos, sin = jnp.tile(cos, (b, 1)), jnp.tile(sin, (b, 1))

    def tails_to_state(tails, rows):
        if sample:
            return tails[:, SUBLANES - rows:, :]
        per = tails.shape[0] // b
        return tails.reshape(b, per, SUBLANES, -1)[:, per - 1, SUBLANES - rows:, :]

    deltas, dconvs, fconvs = [], [], []
    k_new = v_new = k16 = v16_t = None
    for layer in range(depth):
        if layer < n_a:
            st = _pad_rows(dconv0[layer], SUBLANES, front=True) if sample else None
            qkvz, ba, tails = _gdn_in(h, p["a_norm"][layer], w["a_w_all"][layer], w["a_cw"][layer], state=st,
                                      seq_len=l, a_qk=a_qk, a_qkv=a_qkv, a_vd=a_vd, qscale=dk ** -0.5,
                                      tm=tm if sample else min(tm, 256))
            dconvs.append(tails_to_state(tails, p["a_conv_w"].shape[1] - 1))
            o, s_fin = _gdn_chunk(qkvz, ba, w["a_neg_a"][layer], w["a_dt"][layer], p["a_o_gain"][layer],
                                  s0=delta0[layer] if sample else None, batch=b, seq_len=l,
                                  heads=heads, dk=dk, dv=dv)
            deltas.append(s_fin)
            w_o = w["a_w_out"][layer]
        else:
            if layer == n_a:
                if sample:
                    (k_new,) = _norm_proj(h, p["kv_norm"], w["w_k"], w_rot=w["w_k_rot"], cos=cos, sin=sin,
                                          out_dtypes=(F32,), tm=tm)
                    k_out = k_new.reshape(b, l, -1, dh)
                else:
                    k16, k_fm = _norm_proj(h, p["kv_norm"], w["w_k"], w_rot=w["w_k_rot"], cos=cos, sin=sin,
                                           out_dtypes=(BF16,), t_seq=l, tm=tm)
                    k_out = k_fm.reshape(b, -1, dh, l).transpose(0, 3, 1, 2)
                if sample:
                    (v_new,) = _norm_proj(h, p["kv_norm"], w["w_v"], out_dtypes=(F32,), tm=tm)
                else:
                    v_new, v16_t = _norm_proj(h, p["kv_norm"], w["w_v"], out_dtypes=(F32,), t_seq=l, t_dtype=BF16,
                                              tm=tm)
            j = layer - n_a
            lam_init = 0.8 - 0.6 * math.exp(-0.3 * layer)
            if sample:
                (q16,) = _norm_proj(h, p["b_norm"][j], w["b_w_q"][j], w_rot=w["b_w_q_rot"][j], cos=cos, sin=sin,
                                    scale=dh ** -0.5, out_dtypes=(BF16,), tm=tm)
                width = q16.shape[1]
                o = _paged_attention(q16.reshape(b, l, width), k_new.reshape(b, l, width), v_new.reshape(b, l, width),
                                     cache_k, cache_v, page_table, p["b_lambda"][j], p["b_subln"][j],
                                     dh=dh, vd=vd, lam_init=lam_init).reshape(m, width)
            else:
                q16_t = _norm_proj_t(h, p["b_norm"][j], w["b_w_q_t"][j], wt_rot=w["b_w_q_rot_t"][j],
                                     cos_t=cos.T, sin_t=sin.T, scale=dh ** -0.5 * math.log2(math.e), tm=tm)
                o = _flash_prompt(q16_t, k16, v16_t, p["b_lambda"][j], p["b_subln"][j], batch=b, seq_len=l,
                                  dh=dh, vd=vd, lam_init=lam_init)
            w_o = w["b_w_out"][j]
        st = _pad_rows(fconv0[layer], SUBLANES, front=True) if sample else None
        h, tails = _conv_ffn(h, o, w_o, p["f_norm"][layer], w["f_wu"][layer], w["f_cw"][layer],
                             w["f_wd"][layer], state=st, final_g=p["final_norm"] if layer == depth - 1 else None,
                             seq_len=l, tm=tm)
        fconvs.append(tails_to_state(tails, p["f_conv_w"].shape[1] - 1))
    return (h.reshape(b, l, d), jnp.stack(deltas), jnp.stack(dconvs), jnp.stack(fconvs),
            k_out, v_new.reshape(b, l, v_new.shape[1] // vd, vd))


def kernel(x_prompt, x_sample, state_delta, state_dconv, state_fconv, cache_k, cache_v, page_table, a_norm, a_w_in, a_conv_w, a_A_log, a_dt_bias, a_o_gain, a_w_out, kv_norm, w_kv, b_norm, b_w_q, b_lambda, b_subln, b_w_out, f_norm, f_w_up, f_conv_w, f_w_down, final_norm):
    p = dict(a_norm=a_norm, a_w_in=a_w_in, a_conv_w=a_conv_w, a_A_log=a_A_log, a_dt_bias=a_dt_bias,
             a_o_gain=a_o_gain, a_w_out=a_w_out, kv_norm=kv_norm, w_kv=w_kv, b_norm=b_norm, b_w_q=b_w_q,
             b_lambda=b_lambda, b_subln=b_subln, b_w_out=b_w_out, f_norm=f_norm, f_w_up=f_w_up,
             f_conv_w=f_conv_w, f_w_down=f_w_down, final_norm=final_norm)
    w = _prep_weights(p)
    lp = x_prompt.shape[1]
    prompt = _trunk(x_prompt, jnp.arange(lp, dtype=jnp.int32), p, w, delta0=None, dconv0=None, fconv0=None,
                    cache_k=None, cache_v=None, page_table=None)
    ls = x_sample.shape[1]
    past_len = page_table.shape[1] * cache_k.shape[1]
    pool, page = cache_k.shape[:2]
    sample = _trunk(x_sample, past_len + jnp.arange(ls, dtype=jnp.int32), p, w, delta0=state_delta,
                    dconv0=state_dconv, fconv0=state_fconv,
                    cache_k=cache_k.transpose(0, 2, 3, 1).reshape(pool, -1, page),
                    cache_v=cache_v.reshape(pool, page * cache_v.shape[2], cache_v.shape[3]),
                    page_table=page_table)
    return (prompt[0], sample[0]) + prompt[1:] + sample[1:]
```

```python
import functools
import math

import jax
import jax.numpy as jnp
from jax import lax
from jax.experimental import pallas as pl
from jax.experimental.pallas import tpu as pltpu

F32 = jnp.float32
BF16 = jnp.bfloat16
EPS = 1e-6
ROPE_THETA = 10000.0
LANES = 128
SUBLANES = 8
CHUNK = 64
VMEM_LIMIT = 48 * 1024 * 1024
HI = lax.Precision.HIGHEST


def _cparams(*sem, flags=None):
    return pltpu.CompilerParams(dimension_semantics=sem, vmem_limit_bytes=VMEM_LIMIT, flags=flags)


def _dot(a, b):
    return jnp.dot(a, b, preferred_element_type=F32)


def _dot_nt(a, b, precision=None):
    return lax.dot_general(a, b, (((1,), (1,)), ((), ())), precision=precision,
                           preferred_element_type=F32)


def _dot_tn(a, b, precision=None):
    return lax.dot_general(a, b, (((0,), (0,)), ((), ())), precision=precision,
                           preferred_element_type=F32)


def _dot_hi(a, b):
    return jnp.dot(a, b, precision=HI, preferred_element_type=F32)


def _rms(x, g):
    r = lax.rsqrt(jnp.mean(x * x, axis=-1, keepdims=True) + EPS)
    return x * r * g


def _silu(x):
    return x * jax.nn.sigmoid(x)


def _proj_kernel(*refs, rope, scale, n_out, feature_major):
    if rope:
        x_ref, g_ref, w_ref, wr_ref, cos_ref, sin_ref = refs[:6]
        rest = refs[6:]
    else:
        x_ref, g_ref, w_ref = refs[:3]
        rest = refs[3:]
    outs, xn_ref = rest[:n_out], rest[-1]

    @pl.when(pl.program_id(1) == 0)
    def _():
        xn_ref[...] = _rms(x_ref[...], g_ref[...]).astype(BF16)

    xn = xn_ref[...]
    y = _dot(xn, w_ref[...])
    if rope:
        yr = _dot(xn, wr_ref[...])
        reps = y.shape[1] // LANES
        cos = jnp.concatenate([cos_ref[...]] * reps, axis=1)
        sin = jnp.concatenate([sin_ref[...]] * reps, axis=1)
        y = y * cos + yr * sin
    if scale != 1.0:
        y = y * scale
    for o in outs:
        o[...] = y.astype(o.dtype)
    if feature_major:
        o_t = rest[n_out]
        o_t[...] = y.T.astype(o_t.dtype)


def _norm_proj(x, g, w, *, w_rot=None, cos=None, sin=None, scale=1.0, out_dtypes=(F32,), t_seq=None, t_dtype=F32,
               tm, tn=1024):
    m, d = x.shape
    n = w.shape[1]
    tn = min(tn, n)
    rope = w_rot is not None
    out_specs = [pl.BlockSpec((tm, tn), lambda i, j: (i, j)) for _ in out_dtypes]
    out_shape = [jax.ShapeDtypeStruct((m, n), dt) for dt in out_dtypes]
    if t_seq is not None:
        tps = t_seq // tm
        out_specs.append(pl.BlockSpec((None, tn, tm), lambda i, j: (i // tps, j, i % tps)))
        out_shape.append(jax.ShapeDtypeStruct((m // t_seq, n, t_seq), t_dtype))
    in_specs = [pl.BlockSpec((tm, d), lambda i, j: (i, 0)),
                pl.BlockSpec((1, d), lambda i, j: (0, 0)),
                pl.BlockSpec((d, tn), lambda i, j: (0, j))]
    args = [x, g.reshape(1, d), w]
    if rope:
        pt = cos.shape[0] // tm
        in_specs += [pl.BlockSpec((d, tn), lambda i, j: (0, j)),
                     pl.BlockSpec((tm, LANES), lambda i, j: (i % pt, 0)),
                     pl.BlockSpec((tm, LANES), lambda i, j: (i % pt, 0))]
        args += [w_rot, cos, sin]
    res = pl.pallas_call(
        functools.partial(_proj_kernel, rope=rope, scale=scale, n_out=len(out_dtypes),
                          feature_major=t_seq is not None),
        grid=(m // tm, n // tn),
        in_specs=in_specs,
        out_specs=out_specs,
        out_shape=out_shape,
        scratch_shapes=[pltpu.VMEM((tm, d), BF16)],
        compiler_params=_cparams("parallel", "arbitrary"),
        name="norm_proj_rope" if rope else "norm_proj",
    )(*args)
    return res


def _proj_t_kernel(*refs, rope, scale):
    if rope:
        x_ref, g_ref, wt_ref, wrt_ref, cos_ref, sin_ref, o_ref, xn_ref = refs
    else:
        x_ref, g_ref, wt_ref, o_ref, xn_ref = refs

    @pl.when(pl.program_id(1) == 0)
    def _():
        xn_ref[...] = _rms(x_ref[...], g_ref[...]).astype(BF16)

    xn = xn_ref[...]
    y = _dot_nt(wt_ref[...], xn)
    if rope:
        yr = _dot_nt(wrt_ref[...], xn)
        reps = y.shape[0] // LANES
        cos = jnp.concatenate([cos_ref[...]] * reps, axis=0)
        sin = jnp.concatenate([sin_ref[...]] * reps, axis=0)
        y = y * cos + yr * sin
    if scale != 1.0:
        y = y * scale
    o_ref[...] = y.astype(o_ref.dtype)


def _norm_proj_t(x, g, wt, *, wt_rot=None, cos_t=None, sin_t=None, scale=1.0, tm, tn=1024):
    m, d = x.shape
    n = wt.shape[0]
    tn = min(tn, n)
    rope = wt_rot is not None
    in_specs = [pl.BlockSpec((tm, d), lambda i, j: (i, 0)),
                pl.BlockSpec((1, d), lambda i, j: (0, 0)),
                pl.BlockSpec((tn, d), lambda i, j: (j, 0))]
    args = [x, g.reshape(1, d), wt]
    if rope:
        pt = cos_t.shape[1] // tm
        in_specs += [pl.BlockSpec((tn, d), lambda i, j: (j, 0)),
                     pl.BlockSpec((LANES, tm), lambda i, j: (0, i % pt)),
                     pl.BlockSpec((LANES, tm), lambda i, j: (0, i % pt))]
        args += [wt_rot, cos_t, sin_t]
    return pl.pallas_call(
        functools.partial(_proj_t_kernel, rope=rope, scale=scale),
        grid=(m // tm, n // tn),
        in_specs=in_specs,
        out_specs=pl.BlockSpec((tn, tm), lambda i, j: (j, i)),
        out_shape=jax.ShapeDtypeStruct((n, m), BF16),
        scratch_shapes=[pltpu.VMEM((tm, d), BF16)],
        compiler_params=_cparams("parallel", "arbitrary"),
        name="norm_proj_t_rope" if rope else "norm_proj_t",
    )(*args)


def _resident(shape):
    return pl.BlockSpec(shape, lambda *_: (0,) * len(shape), pipeline_mode=pl.Buffered(1))


def _ffn_kernel(*refs, per_seq, tiles_per_seq, tf, final):
    x_ref, mix_ref, wo_ref, g_ref, wu_ref, cw_ref, wd_ref = refs[:7]
    pos = 7
    st_ref = fg_ref = carry_ref = None
    if per_seq:
        st_ref = refs[pos]
        pos += 1
    if final:
        fg_ref = refs[pos]
        pos += 1
    o_ref, tail_ref, act_ref, pad_ref = refs[pos:pos + 4]
    if not per_seq:
        carry_ref = refs[pos + 4]
    i = pl.program_id(0)
    tm = x_ref.shape[0]
    f = wd_ref.shape[0]
    nf = f // tf
    x = x_ref[...] + _dot(mix_ref[...], wo_ref[...])
    xn = _rms(x, g_ref[...]).astype(BF16)

    if not per_seq:
        @pl.when((i % tiles_per_seq) == 0)
        def _():
            carry_ref[...] = jnp.zeros_like(carry_ref)

    for j in range(nf):
        cols = slice(j * tf, (j + 1) * tf)
        gate = _dot(xn, wu_ref[:, cols])
        val = _dot(xn, wu_ref[:, f + j * tf:f + (j + 1) * tf])
        cw = cw_ref[:, cols]
        pad = pad_ref.at[j % 2]
        if per_seq:
            g3 = gate.reshape(tm // SUBLANES, SUBLANES, tf)
            pad[:, :SUBLANES, :] = st_ref[:, :, cols]
            pad[:, SUBLANES:, :] = g3
            tail_ref[:, :, cols] = g3
            conv = (cw[2:3][None] * g3 + cw[1:2][None] * pad[:, SUBLANES - 1:2 * SUBLANES - 1, :]
                    + cw[0:1][None] * pad[:, SUBLANES - 2:2 * SUBLANES - 2, :])
        else:
            tail = gate[tm - SUBLANES:]
            pad[:SUBLANES, :] = carry_ref[:, cols]
            pad[SUBLANES:, :] = gate
            tail_ref[0, :, cols] = tail
            carry_ref[:, cols] = tail
            conv = (cw[2:3] * gate + cw[1:2] * pad[pl.ds(SUBLANES - 1, tm), :]
                    + cw[0:1] * pad[pl.ds(SUBLANES - 2, tm), :])
        act_ref[:, cols] = (_silu(conv).reshape(tm, tf) * val).astype(BF16)

    y = x + _dot(act_ref[...], wd_ref[...])
    if final:
        y = _rms(y, fg_ref[...])
    o_ref[...] = y


def _conv_ffn(x, mix, w_o, g, wu, cw, wd, *, state=None, final_g=None, seq_len, tm, tf=256):
    m, d = x.shape
    f = wd.shape[0]
    per_seq = state is not None
    final = final_g is not None
    groups = tm // SUBLANES if per_seq else 1
    in_specs = [pl.BlockSpec((tm, d), lambda i: (i, 0)),
                pl.BlockSpec((tm, mix.shape[1]), lambda i: (i, 0)), _resident(w_o.shape),
                _resident((1, d)), _resident((d, 2 * f)), _resident((SUBLANES, f)), _resident((f, d))]
    args = [x, mix, w_o, g.reshape(1, d), wu, cw, wd]
    if per_seq:
        in_specs.append(pl.BlockSpec((groups, SUBLANES, f), lambda i: (i, 0, 0)))
        args.append(state)
    if final:
        in_specs.append(_resident((1, d)))
        args.append(final_g.reshape(1, d))
    if per_seq:
        scratch = [pltpu.VMEM((tm, f), BF16), pltpu.VMEM((2, groups, 2 * SUBLANES, tf), F32)]
    else:
        scratch = [pltpu.VMEM((tm, f), BF16), pltpu.VMEM((2, SUBLANES + tm, tf), F32), pltpu.VMEM((SUBLANES, f), F32)]
    return pl.pallas_call(
        functools.partial(_ffn_kernel, per_seq=per_seq, tiles_per_seq=max(seq_len // tm, 1), tf=tf, final=final),
        grid=(m // tm,),
        in_specs=in_specs,
        out_specs=[pl.BlockSpec((tm, d), lambda i: (i, 0)),
                   pl.BlockSpec((groups, SUBLANES, f), lambda i: (i, 0, 0))],
        out_shape=[jax.ShapeDtypeStruct((m, d), F32),
                   jax.ShapeDtypeStruct((m // tm * groups, SUBLANES, f), F32)],
        scratch_shapes=scratch,
        compiler_params=_cparams("arbitrary"),
        name="conv_ffn",
    )(*args)


def _gdn_in_kernel(*refs, per_seq, tiles_per_seq, tn, nq, nqkv, nz, qscale):
    x_ref, g_ref, w_ref, cw_ref = refs[:4]
    pos = 4
    st_ref = carry_ref = None
    if per_seq:
        st_ref = refs[pos]
        pos += 1
    o_ref, ba_ref, tail_ref, pad_ref = refs[pos:pos + 4]
    if not per_seq:
        carry_ref = refs[pos + 4]
    i = pl.program_id(0)
    tm = x_ref.shape[0]
    xn = _rms(x_ref[...], g_ref[...]).astype(BF16)

    if not per_seq:
        @pl.when((i % tiles_per_seq) == 0)
        def _():
            carry_ref[...] = jnp.zeros_like(carry_ref)

    def l2n(y, s):
        parts = []
        for a in range(tn // LANES):
            ya = y[:, a * LANES:(a + 1) * LANES]
            r = lax.rsqrt(jnp.sum(ya * ya, axis=-1, keepdims=True) + EPS)
            parts.append(ya * (r * s) if s != 1.0 else ya * r)
        return jnp.concatenate(parts, axis=1)

    for j in range(nqkv + nz + 1):
        cols = slice(j * tn, (j + 1) * tn)
        pre = _dot(xn, w_ref[:, cols])
        if j == nqkv + nz:
            ba_ref[...] = pre[:, :2 * LANES]
        elif j >= nqkv:
            o_ref[:, cols] = pre
        else:
            cw = cw_ref[:, cols]
            pad = pad_ref.at[j % 2]
            if per_seq:
                p3 = pre.reshape(tm // SUBLANES, SUBLANES, tn)
                pad[:, :SUBLANES, :] = st_ref[:, :, cols]
                pad[:, SUBLANES:, :] = p3
                tail_ref[:, :, cols] = p3
                conv = cw[3:4][None] * p3
                for t in range(3):
                    conv = conv + cw[t:t + 1][None] * pad[:, SUBLANES - 3 + t:2 * SUBLANES - 3 + t, :]
            else:
                tail = pre[tm - SUBLANES:]
                pad[:SUBLANES, :] = carry_ref[:, cols]
                pad[SUBLANES:, :] = pre
                tail_ref[0, :, cols] = tail
                carry_ref[:, cols] = tail
                conv = cw[3:4] * pre
                for t in range(3):
                    conv = conv + cw[t:t + 1] * pad[pl.ds(SUBLANES - 3 + t, tm), :]
            y = _silu(conv).reshape(tm, tn)
            if j < nq:
                y = l2n(y, qscale)
            elif j < 2 * nq:
                y = l2n(y, 1.0)
            o_ref[:, cols] = y


def _gdn_in(x, g, w_all, cw, *, state=None, seq_len, a_qk, a_qkv, a_vd, qscale, tm, tn=256):
    m, d = x.shape
    per_seq = state is not None
    nq, nqkv, nz = a_qk // tn, a_qkv // tn, a_vd // tn
    groups = tm // SUBLANES if per_seq else 1
    in_specs = [pl.BlockSpec((tm, d), lambda i: (i, 0)),
                _resident((1, d)), _resident(w_all.shape), _resident((SUBLANES, a_qkv))]
    args = [x, g.reshape(1, d), w_all, cw]
    if per_seq:
        in_specs.append(pl.BlockSpec((groups, SUBLANES, a_qkv), lambda i: (i, 0, 0)))
        args.append(state)
    if per_seq:
        scratch = [pltpu.VMEM((2, groups, 2 * SUBLANES, tn), F32)]
    else:
        scratch = [pltpu.VMEM((2, SUBLANES + tm, tn), F32), pltpu.VMEM((SUBLANES, a_qkv), F32)]
    return pl.pallas_call(
        functools.partial(_gdn_in_kernel, per_seq=per_seq, tiles_per_seq=max(seq_len // tm, 1),
                          tn=tn, nq=nq, nqkv=nqkv, nz=nz, qscale=qscale),
        grid=(m // tm,),
        in_specs=in_specs,
        out_specs=[pl.BlockSpec((tm, a_qkv + a_vd), lambda i: (i, 0)),
                   pl.BlockSpec((tm, 2 * LANES), lambda i: (i, 0)),
                   pl.BlockSpec((groups, SUBLANES, a_qkv), lambda i: (i, 0, 0))],
        out_shape=[jax.ShapeDtypeStruct((m, a_qkv + a_vd), F32),
                   jax.ShapeDtypeStruct((m, 2 * LANES), F32),
                   jax.ShapeDtypeStruct((m // tm * groups, SUBLANES, a_qkv), F32)],
        scratch_shapes=scratch,
        compiler_params=_cparams("arbitrary"),
        name="gdn_in",
    )(*args)


def _split(a):
    hi = a.astype(BF16)
    return hi, (a - hi.astype(F32)).astype(BF16)


def _dot3(a, b, nt=False):
    f = _dot_nt if nt else _dot
    return f(a[0], b[0]) + (f(a[0], b[1]) + f(a[1], b[0]))


def _unit_lower_inverses(lms, row, col, eye):
    size = lms[0].shape[0]
    def same(s):
        return (row >> s) == (col >> s)

    def dot1(a, b):
        return _dot(a.astype(BF16), b.astype(BF16))

    nd = [jnp.where(same(3), -lm, 0.0) for lm in lms]
    nd2 = [dot1(a, a) for a in nd]
    nd4 = [dot1(a, a) for a in nd2]
    x = [dot1(eye + a, eye + b) for a, b in zip(nd, nd2)]
    x = [dot1(a, eye + b) for a, b in zip(x, nd4)]
    for s in range(3, size.bit_length() - 1):
        mask = same(s + 1) & jnp.logical_not(same(s))
        y = [dot1(a, jnp.where(mask, lm, 0.0)) for a, lm in zip(x, lms)]
        x = [a - dot1(b, a) for a, b in zip(x, y)]
    return x


def _gdn_chunk_kernel(*refs, heads, dk, dv, has_s0):
    q_ref, k_ref, v_ref, z_ref, ba_ref, na_ref, dt_ref, gain_ref = refs[:8]
    pos = 8
    s0_ref = None
    if has_s0:
        s0_ref = refs[pos]
        pos += 1
    o_ref, sout_ref, s_ref = refs[pos:pos + 3]
    n = pl.program_id(1)

    @pl.when(n == 0)
    def _():
        if has_s0:
            s_ref[...] = s0_ref[...]
        else:
            s_ref[...] = jnp.zeros_like(s_ref)

    bt, cr = q_ref.shape[0], q_ref.shape[1]
    c = max(cr, 2 * SUBLANES)

    def pad(a):
        if cr == c:
            return a
        return jnp.concatenate([a, jnp.zeros((c - cr, a.shape[1]), a.dtype)], axis=0)

    row = lax.broadcasted_iota(jnp.int32, (c, c), 0)
    col = lax.broadcasted_iota(jnp.int32, (c, c), 1)
    lower = row >= col
    strict = row > col
    eye = (row == col).astype(F32)
    tri = lower.astype(F32)
    gain = gain_ref[...]

    qh, kh, vh, bcol, egcol, eglcol, eglast, decay = [], [], [], [], [], [], [], []
    for b in range(bt):
        ba = ba_ref[b]
        beta = pad(jax.nn.sigmoid(ba[:, :LANES]))
        g = pad(na_ref[...] * jax.nn.softplus(ba[:, LANES:] + dt_ref[...]))
        q, k, v = pad(q_ref[b]), pad(k_ref[b]), pad(v_ref[b])
        gc = _dot_hi(tri, g)
        gct = gc.T
        glast = gc[c - 1:c]
        eg = jnp.exp(gc)
        egl = jnp.exp(glast - gc)
        egt = jnp.exp(glast)
        for h in range(heads):
            qh.append(q[:, h * dk:(h + 1) * dk])
            kh.append(k[:, h * dk:(h + 1) * dk])
            vh.append(v[:, h * dv:(h + 1) * dv])
            bcol.append(beta[:, h:h + 1])
            egcol.append(eg[:, h:h + 1])
            eglcol.append(egl[:, h:h + 1])
            eglast.append(egt[:, h:h + 1])
            decay.append(jnp.where(lower, jnp.exp(gc[:, h:h + 1] - gct[h:h + 1, :]), 0.0))
    ps = range(bt * heads)
    kb = [kh[i] * bcol[i] for i in ps]
    kh16 = [kh[i].astype(BF16) for i in ps]
    kk = [_dot_nt(kb[i].astype(BF16), kh16[i]) for i in ps]
    lm = [jnp.where(strict, kk[i] * decay[i], 0.0) for i in ps]
    t = _unit_lower_inverses(lm, row, col, eye)
    rhs = [jnp.concatenate([vh[i] * bcol[i], kb[i] * egcol[i]], axis=1) for i in ps]
    sol = [_dot(t[i].astype(BF16), rhs[i].astype(BF16)) for i in ps]
    attn = [(_dot_nt(qh[i].astype(BF16), kh16[i]) * decay[i]).astype(BF16) for i in ps]
    qe16 = [(qh[i] * egcol[i]).astype(BF16) for i in ps]
    kd16 = [(kh[i] * eglcol[i]).astype(BF16) for i in ps]
    sh = [s_ref[i // heads, i % heads] for i in ps]
    sh16 = [a.astype(BF16) for a in sh]
    v_new = [sol[i][:, :dv] - _dot(sol[i][:, dv:].astype(BF16), sh16[i]) for i in ps]
    vn16 = [a.astype(BF16) for a in v_new]
    o = [_dot(qe16[i], sh16[i]) + _dot(attn[i], vn16[i]) for i in ps]
    for i in ps:
        s_ref[i // heads, i % heads] = sh[i] * eglast[i] + _dot_tn(kd16[i], vn16[i])
    for i in ps:
        b, h = i // heads, i % heads
        oh = o[i][:cr]
        zh = z_ref[b, :, h * dv:(h + 1) * dv]
        on = oh * lax.rsqrt(jnp.mean(oh * oh, axis=-1, keepdims=True) + EPS) * gain * _silu(zh)
        o_ref[b, :, h * dv:(h + 1) * dv] = on.astype(o_ref.dtype)

    @pl.when(n == pl.num_programs(1) - 1)
    def _():
        sout_ref[...] = s_ref[...]


def _gdn_chunk(qkvz, ba, neg_a, dt_bias, gain, *, s0, batch, seq_len, heads, dk, dv, bt=4):
    m = qkvz.shape[0]
    cr = min(CHUNK, seq_len)
    nc = seq_len // cr
    hd = heads * dk
    bt = math.gcd(batch, bt)
    has_s0 = s0 is not None
    qkvz = qkvz.reshape(batch, seq_len, -1)
    ba = ba.reshape(batch, seq_len, -1)

    def blk(c):
        return pl.BlockSpec((bt, cr, hd), lambda b, n: (b, n, c))

    in_specs = [blk(0), blk(1), blk(2), blk(3),
                pl.BlockSpec((bt, cr, 2 * LANES), lambda b, n: (b, n, 0)),
                pl.BlockSpec((1, LANES), lambda b, n: (0, 0)),
                pl.BlockSpec((1, LANES), lambda b, n: (0, 0)),
                pl.BlockSpec((1, dv), lambda b, n: (0, 0))]
    args = [qkvz, qkvz, qkvz, qkvz, ba, neg_a, dt_bias, gain.reshape(1, dv)]
    if has_s0:
        in_specs.append(pl.BlockSpec((bt, heads, dk, dv), lambda b, n: (b, 0, 0, 0)))
        args.append(s0)
    o, s_fin = pl.pallas_call(
        functools.partial(_gdn_chunk_kernel, heads=heads, dk=dk, dv=dv, has_s0=has_s0),
        grid=(batch // bt, nc),
        in_specs=in_specs,
        out_specs=[pl.BlockSpec((bt, cr, hd), lambda b, n: (b, n, 0)),
                   pl.BlockSpec((bt, heads, dk, dv), lambda b, n: (b, 0, 0, 0))],
        out_shape=[jax.ShapeDtypeStruct((batch, seq_len, hd), BF16),
                   jax.ShapeDtypeStruct((batch, heads, dk, dv), F32)],
        scratch_shapes=[pltpu.VMEM((bt, heads, dk, dv), F32)],
        compiler_params=_cparams("parallel", "arbitrary"),
        name="gdn_chunk",
    )(*args)
    return o.reshape(m, hd), s_fin


def _lambda(lv_ref, lam_init):
    lv = lv_ref[...]
    a = jnp.sum(lv[0:1] * lv[1:2], axis=-1, keepdims=True)
    b = jnp.sum(lv[2:3] * lv[3:4], axis=-1, keepdims=True)
    return jnp.exp(a) - jnp.exp(b) + lam_init


def _flash_kernel(qt_ref, kt_ref, q_ref, k_ref, v_ref, lv_ref, sub_ref, o_ref, qs_ref, m_ref, acc_ref,
                  *, dh, lam_init):
    step = pl.program_id(2)
    qi = qt_ref[step]
    ki = kt_ref[step]
    tq = q_ref.shape[1]
    tk = k_ref.shape[0]
    vd = v_ref.shape[0]

    @pl.when(ki == 0)
    def _():
        q = q_ref[...]
        feat = lax.broadcasted_iota(jnp.int32, q.shape, 0)
        zero = jnp.zeros_like(q)
        qs_ref[:, :tq] = jnp.where(feat < dh, q, zero)
        qs_ref[:, tq:] = jnp.where(feat >= dh, q, zero)
        m_ref[...] = jnp.full_like(m_ref, -jnp.inf)
        acc_ref[...] = jnp.zeros_like(acc_ref)

    def accumulate(masked):
        s = _dot(k_ref[...], qs_ref[...])
        if masked:
            kpos = ki * tk + lax.broadcasted_iota(jnp.int32, s.shape, 0)
            qpos = qi * tq + (lax.broadcasted_iota(jnp.int32, s.shape, 1) & (tq - 1))
            s = jnp.where(kpos <= qpos, s, -jnp.inf)
        m_old = m_ref[...]
        m_new = jnp.maximum(m_old, jnp.max(s, axis=0, keepdims=True))
        alpha = jnp.exp2(m_old - m_new)
        p = jnp.exp2(s - m_new).astype(BF16)
        v_ones = jnp.concatenate([v_ref[...], jnp.ones((acc_ref.shape[0] - vd, tk), BF16)], axis=0)
        acc_ref[...] = alpha * acc_ref[...] + _dot(v_ones, p)
        m_ref[...] = m_new

    def accumulate_diagonal():
        hk, hq = tk // 2, tq // 2
        v_ones = jnp.concatenate([v_ref[...], jnp.ones((acc_ref.shape[0] - vd, tk), BF16)], axis=0)
        s = _dot(k_ref[:hk, :], qs_ref[...])
        kpos = lax.broadcasted_iota(jnp.int32, s.shape, 0)
        qpos = lax.broadcasted_iota(jnp.int32, s.shape, 1) & (tq - 1)
        s = jnp.where(kpos <= qpos, s, -jnp.inf)
        m_old = m_ref[...]
        m_new = jnp.maximum(m_old, jnp.max(s, axis=0, keepdims=True))
        p = jnp.exp2(s - m_new).astype(BF16)
        acc_ref[...] = jnp.exp2(m_old - m_new) * acc_ref[...] + _dot(v_ones[:, :hk], p)
        m_ref[...] = m_new
        k2 = k_ref[hk:, :]
        v2 = v_ones[:, hk:]
        for base in (hq, tq + hq):
            cols = slice(base, base + hq)
            s = _dot(k2, qs_ref[:, cols])
            kpos = lax.broadcasted_iota(jnp.int32, s.shape, 0)
            qpos = lax.broadcasted_iota(jnp.int32, s.shape, 1)
            s = jnp.where(kpos <= qpos, s, -jnp.inf)
            m_old = m_ref[:, cols]
            m_new = jnp.maximum(m_old, jnp.max(s, axis=0, keepdims=True))
            p = jnp.exp2(s - m_new).astype(BF16)
            acc_ref[:, cols] = jnp.exp2(m_old - m_new) * acc_ref[:, cols] + _dot(v2, p)
            m_ref[:, cols] = m_new

    below_diagonal = (ki + 1) * tk - 1 <= qi * tq

    @pl.when(below_diagonal)
    def _():
        accumulate(False)

    @pl.when(jnp.logical_not(below_diagonal))
    def _():
        if tq == tk:
            accumulate_diagonal()
        else:
            accumulate(True)

    @pl.when((ki + 1) * tk == (qi + 1) * tq)
    def _():
        lam = _lambda(lv_ref, lam_init)
        a = acc_ref[:vd] / acc_ref[vd:vd + 1]
        o = a[:, :tq] - lam * a[:, tq:]
        r = lax.rsqrt(jnp.mean(o * o, axis=0, keepdims=True) + EPS)
        o = o * r * (sub_ref[...] * (1.0 - lam_init))
        o_ref[...] = o.T.astype(o_ref.dtype)


def _flash_prompt(q_t, k, v_t, lam_vecs, subln, *, batch, seq_len, dh, vd, lam_init, tq=1024, tk=1024):
    m, width = k.shape
    pairs = width // (2 * dh)
    tq = min(tq, seq_len)
    tk = min(tk, tq)
    nq, nk = seq_len // tq, seq_len // tk
    assert tq & (tq - 1) == 0 and tq % tk == 0
    steps = [(i, j) for i in range(nq) for j in range((i + 1) * tq // tk)]
    q_tab = jnp.asarray([s[0] for s in steps], jnp.int32)
    k_tab = jnp.asarray([s[1] for s in steps], jnp.int32)
    grid_spec = pltpu.PrefetchScalarGridSpec(
        num_scalar_prefetch=2,
        grid=(batch, pairs, len(steps)),
        in_specs=[pl.BlockSpec((2 * dh, tq), lambda b, h, s, qt, kt: (h, b * nq + qt[s])),
                  pl.BlockSpec((tk, 2 * dh), lambda b, h, s, qt, kt: (b * nk + kt[s], h)),
                  pl.BlockSpec((None, vd, tk), lambda b, h, s, qt, kt: (b, h, kt[s])),
                  pl.BlockSpec(lam_vecs.shape, lambda b, h, s, qt, kt: (0, 0)),
                  pl.BlockSpec((vd, 1), lambda b, h, s, qt, kt: (0, 0))],
        out_specs=pl.BlockSpec((tq, vd), lambda b, h, s, qt, kt: (b * nq + qt[s], h)),
        scratch_shapes=[pltpu.VMEM((2 * dh, 2 * tq), BF16),
                        pltpu.VMEM((1, 2 * tq), F32),
                        pltpu.VMEM((vd + 2 * SUBLANES, 2 * tq), F32)])
    return pl.pallas_call(
        functools.partial(_flash_kernel, dh=dh, lam_init=lam_init),
        grid_spec=grid_spec,
        out_shape=jax.ShapeDtypeStruct((m, pairs * vd), BF16),
        compiler_params=_cparams("parallel", "parallel", "arbitrary"),
        name="diff_flash",
    )(q_tab, k_tab, q_t, k, v_t, lam_vecs, subln.reshape(vd, 1))


def _paged_kernel(*refs, nh, dh, vd, lam_init, group):
    q_ref, kn_ref, vn_ref = refs[1:4]
    kc_refs = refs[4:4 + group]
    vc_refs = refs[4 + group:4 + 2 * group]
    lv_ref, sub_ref, o_ref, qbd_ref, m_ref, l_ref, acc_ref = refs[4 + 2 * group:]
    p = pl.program_id(1)
    t = q_ref.shape[1]
    width = q_ref.shape[2]
    rows = nh * t
    page = kc_refs[0].shape[2]
    nvh = width // vd

    def accum(kt16, v16, mask):
        s = _dot(qbd_ref[...], kt16)
        if mask is not None:
            s = jnp.where(mask, s, -jnp.inf)
        m_old = m_ref[...]
        m_new = jnp.maximum(m_old, jnp.max(s, axis=-1, keepdims=True))
        alpha = jnp.exp(m_old - m_new)
        pr = jnp.exp(s - m_new)
        l_ref[...] = alpha * l_ref[...] + jnp.sum(pr, axis=-1, keepdims=True)
        acc_ref[...] = alpha * acc_ref[...] + _dot(pr.astype(BF16), v16)
        m_ref[...] = m_new

    @pl.when(p == 0)
    def _():
        q = q_ref[0].astype(F32)
        q3 = jnp.broadcast_to(q[None], (nh, t, width))
        hd = lax.broadcasted_iota(jnp.int32, (nh, t, width), 0)
        ln = lax.broadcasted_iota(jnp.int32, (nh, t, width), 2)
        qbd = jnp.where((ln >= hd * dh) & (ln < (hd + 1) * dh), q3, 0.0)
        qbd_ref[...] = qbd.reshape(rows, width).astype(BF16)
        m_ref[...] = jnp.full_like(m_ref, -jnp.inf)
        l_ref[...] = jnp.zeros_like(l_ref)
        acc_ref[...] = jnp.zeros_like(acc_ref)
        zpad = jnp.zeros((page - t, width), F32)
        kt16 = jnp.concatenate([kn_ref[0], zpad], axis=0).T.astype(BF16)
        v16 = jnp.concatenate([vn_ref[0], zpad], axis=0).astype(BF16)
        r = lax.broadcasted_iota(jnp.int32, (rows, page), 0)
        c = lax.broadcasted_iota(jnp.int32, (rows, page), 1)
        accum(kt16, v16, c <= (r & (t - 1)))

    @pl.when(p > 0)
    def _():
        v = jnp.concatenate(
            [jnp.concatenate([vc[0, pl.ds(h, page, stride=nvh), :] for h in range(nvh)], axis=1).astype(BF16)
             for vc in vc_refs], axis=0)
        kt = jnp.concatenate([kc[0].astype(BF16) for kc in kc_refs], axis=1)
        accum(kt, v, None)

    @pl.when(p == pl.num_programs(1) - 1)
    def _():
        lam = _lambda(lv_ref, lam_init)
        r = lax.broadcasted_iota(jnp.int32, (rows, 1), 0)
        odd = ((r // t) & 1) == 1
        wgt = jnp.where(odd, -lam, 1.0) / l_ref[...]
        a3 = (acc_ref[...] * wgt).reshape(nh, t, width)
        hd = lax.broadcasted_iota(jnp.int32, (nh, t, width), 0)
        ln = lax.broadcasted_iota(jnp.int32, (nh, t, width), 2)
        pair = hd >> 1
        o = jnp.sum(jnp.where((ln >= pair * vd) & (ln < (pair + 1) * vd), a3, 0.0), axis=0)
        sub = sub_ref[...]
        for h in range(width // vd):
            oh = o[:, h * vd:(h + 1) * vd]
            o_ref[0, :, h * vd:(h + 1) * vd] = (_rms(oh, sub) * (1.0 - lam_init)).astype(o_ref.dtype)


def _paged_attention(q, k_new, v_new, cache_k, cache_v, page_table, lam_vecs, subln, *, dh, vd, lam_init, group=8):
    b, t, width = q.shape
    npg = page_table.shape[1]
    page = cache_k.shape[2]
    nh = width // dh
    nvh = width // vd
    assert t & (t - 1) == 0 and t <= page and nh == 2 * nvh and cache_v.shape[1] == page * nvh
    group = math.gcd(npg, group)

    def pidx(i):
        return lambda bb, p, pt: (pt[bb * npg + jnp.maximum(p - 1, 0) * group + i], 0, 0)

    grid_spec = pltpu.PrefetchScalarGridSpec(
        num_scalar_prefetch=1,
        grid=(b, npg // group + 1),
        in_specs=[pl.BlockSpec((1, t, width), lambda bb, p, pt: (bb, 0, 0)),
                  pl.BlockSpec((1, t, width), lambda bb, p, pt: (bb, 0, 0)),
                  pl.BlockSpec((1, t, width), lambda bb, p, pt: (bb, 0, 0))]
        + [pl.BlockSpec((1, width, page), pidx(i)) for i in range(group)]
        + [pl.BlockSpec((1, page * nvh, vd), pidx(i)) for i in range(group)]
        + [pl.BlockSpec(lam_vecs.shape, lambda bb, p, pt: (0, 0)),
           pl.BlockSpec((1, vd), lambda bb, p, pt: (0, 0))],
        out_specs=pl.BlockSpec((1, t, width), lambda bb, p, pt: (bb, 0, 0)),
        scratch_shapes=[pltpu.VMEM((nh * t, width), BF16),
                        pltpu.VMEM((nh * t, 1), F32),
                        pltpu.VMEM((nh * t, 1), F32),
                        pltpu.VMEM((nh * t, width), F32)])
    return pl.pallas_call(
        functools.partial(_paged_kernel, nh=nh, dh=dh, vd=vd, lam_init=lam_init, group=group),
        grid_spec=grid_spec,
        out_shape=jax.ShapeDtypeStruct((b, t, width), BF16),
        compiler_params=_cparams("parallel", "arbitrary"),
        name="diff_paged",
    )(page_table.reshape(-1), q, k_new, v_new, *([cache_k] * group), *([cache_v] * group),
      lam_vecs, subln.reshape(1, vd))


def _rot_weight(w, dh):
    k, n = w.shape
    w4 = w.reshape(k, n // dh, 2, dh // 2)
    return jnp.stack([-w4[:, :, 1], w4[:, :, 0]], axis=2).reshape(k, n)


def _rope_tables(pos, dh):
    half = dh // 2
    inv = 1.0 / (ROPE_THETA ** (jnp.arange(half, dtype=F32) / half))
    ang = pos.astype(F32)[:, None] * inv[None, :]
    reps = LANES // half
    return jnp.tile(jnp.cos(ang), (1, reps)), jnp.tile(jnp.sin(ang), (1, reps))


def _pad_rows(a, rows, front):
    pad = [(0, 0)] * a.ndim
    pad[-2] = (rows - a.shape[-2], 0) if front else (0, rows - a.shape[-2])
    return jnp.pad(a, pad)


def _prep_weights(p):
    n_a, d, a_in = p["a_w_in"].shape
    heads = p["a_A_log"].shape[1]
    dv = p["a_o_gain"].shape[1]
    a_vd = heads * dv
    a_qkv = a_in - a_vd - 2 * heads
    dh = p["b_lambda"].shape[-1]
    w = {}
    w_in = p["a_w_in"]
    tn = 512
    zpad = jnp.zeros((n_a, d, LANES - heads), F32)
    w["a_w_all"] = jnp.concatenate(
        [w_in[:, :, :a_qkv + a_vd], w_in[:, :, a_qkv + a_vd:a_qkv + a_vd + heads], zpad,
         w_in[:, :, a_qkv + a_vd + heads:], zpad, jnp.zeros((n_a, d, tn - 2 * LANES), F32)], axis=2).astype(BF16)
    w["a_cw"] = _pad_rows(p["a_conv_w"], SUBLANES, front=False)
    hp = jnp.zeros((n_a, LANES - heads), F32)
    w["a_neg_a"] = jnp.concatenate([-jnp.exp(p["a_A_log"].astype(F32)), hp], axis=1)[:, None, :]
    w["a_dt"] = jnp.concatenate([p["a_dt_bias"].astype(F32), hp], axis=1)[:, None, :]
    w["a_w_out"] = p["a_w_out"].astype(BF16)
    kq = p["w_kv"].shape[1] - (p["b_w_out"].shape[1])
    w["w_k"] = p["w_kv"][:, :kq].astype(BF16)
    w["w_k_rot"] = _rot_weight(p["w_kv"][:, :kq], dh).astype(BF16)
    w["w_v"] = p["w_kv"][:, kq:].astype(BF16)
    w["b_w_q"] = p["b_w_q"].astype(BF16)
    w["b_w_q_rot"] = jnp.stack([_rot_weight(p["b_w_q"][j], dh) for j in range(p["b_w_q"].shape[0])]).astype(BF16)
    w["b_w_out"] = p["b_w_out"].astype(BF16)
    w["b_w_q_t"] = jnp.swapaxes(w["b_w_q"], 1, 2)
    w["b_w_q_rot_t"] = jnp.swapaxes(w["b_w_q_rot"], 1, 2)
    f = p["f_w_down"].shape[1]
    w["f_wu"] = p["f_w_up"].astype(BF16)
    w["f_cw"] = _pad_rows(p["f_conv_w"], SUBLANES, front=False)
    w["f_wd"] = p["f_w_down"].astype(BF16)
    return w


def _trunk(x, pos, p, w, *, delta0, dconv0, fconv0, cache_k, cache_v, page_table):
    b, l, d = x.shape
    m = b * l
    sample = page_table is not None
    depth = p["f_norm"].shape[0]
    n_a = p["a_norm"].shape[0]
    heads = p["a_A_log"].shape[1]
    dv = p["a_o_gain"].shape[1]
    dk = (p["a_w_in"].shape[2] - 2 * heads - 2 * heads * dv) // (2 * heads)
    a_vd = heads * dv
    a_qk = heads * dk
    a_qkv = 2 * a_qk + a_vd
    dh = p["b_lambda"].shape[-1]
    vd = p["b_subln"].shape[-1]
    f = p["f_w_down"].shape[1]
    tm = m if sample else min(512, l)
    assert m % tm == 0 and (sample or l % tm == 0)

    h = x.reshape(m, d)
    cos, sin = _rope_tables(pos, dh)
    if sample:
        cos, sin = jnp.tile(cos, (b, 1)), jnp.tile(sin, (b, 1))

    def tails_to_state(tails, rows):
        if sample:
            return tails[:, SUBLANES - rows:, :]
        per = tails.shape[0] // b
        return tails.reshape(b, per, SUBLANES, -1)[:, per - 1, SUBLANES - rows:, :]

    deltas, dconvs, fconvs = [], [], []
    k_new = v_new = k16 = v16_t = None
    for layer in range(depth):
        if layer < n_a:
            st = _pad_rows(dconv0[layer], SUBLANES, front=True) if sample else None
            qkvz, ba, tails = _gdn_in(h, p["a_norm"][layer], w["a_w_all"][layer], w["a_cw"][layer], state=st,
                                      seq_len=l, a_qk=a_qk, a_qkv=a_qkv, a_vd=a_vd, qscale=dk ** -0.5,
                                      tm=tm if sample else min(tm, 256))
            dconvs.append(tails_to_state(tails, p["a_conv_w"].shape[1] - 1))
            o, s_fin = _gdn_chunk(qkvz, ba, w["a_neg_a"][layer], w["a_dt"][layer], p["a_o_gain"][layer],
                                  s0=delta0[layer] if sample else None, batch=b, seq_len=l,
                                  heads=heads, dk=dk, dv=dv)
            deltas.append(s_fin)
            w_o = w["a_w_out"][layer]
        else:
            if layer == n_a:
                if sample:
                    (k_new,) = _norm_proj(h, p["kv_norm"], w["w_k"], w_rot=w["w_k_rot"], cos=cos, sin=sin,
                                          out_dtypes=(F32,), tm=tm)
                    k_out = k_new.reshape(b, l, -1, dh)
                else:
                    k16, k_fm = _norm_proj(h, p["kv_norm"], w["w_k"], w_rot=w["w_k_rot"], cos=cos, sin=sin,
                                           out_dtypes=(BF16,), t_seq=l, tm=tm)
                    k_out = k_fm.reshape(b, -1, dh, l).transpose(0, 3, 1, 2)
                if sample:
                    (v_new,) = _norm_proj(h, p["kv_norm"], w["w_v"], out_dtypes=(F32,), tm=tm)
                else:
                    v_new, v16_t = _norm_proj(h, p["kv_norm"], w["w_v"], out_dtypes=(F32,), t_seq=l, t_dtype=BF16,
                                              tm=tm)
            j = layer - n_a
            lam_init = 0.8 - 0.6 * math.exp(-0.3 * layer)
            if sample:
                (q16,) = _norm_proj(h, p["b_norm"][j], w["b_w_q"][j], w_rot=w["b_w_q_rot"][j], cos=cos, sin=sin,
                                    scale=dh ** -0.5, out_dtypes=(BF16,), tm=tm)
                width = q16.shape[1]
                o = _paged_attention(q16.reshape(b, l, width), k_new.reshape(b, l, width), v_new.reshape(b, l, width),
                                     cache_k, cache_v, page_table, p["b_lambda"][j], p["b_subln"][j],
                                     dh=dh, vd=vd, lam_init=lam_init).reshape(m, width)
            else:
                q16_t = _norm_proj_t(h, p["b_norm"][j], w["b_w_q_t"][j], wt_rot=w["b_w_q_rot_t"][j],
                                     cos_t=cos.T, sin_t=sin.T, scale=dh ** -0.5 * math.log2(math.e), tm=tm)
                o = _flash_prompt(q16_t, k16, v16_t, p["b_lambda"][j], p["b_subln"][j], batch=b, seq_len=l,
                                  dh=dh, vd=vd, lam_init=lam_init)
            w_o = w["b_w_out"][j]
        st = _pad_rows(fconv0[layer], SUBLANES, front=True) if sample else None
        h, tails = _conv_ffn(h, o, w_o, p["f_norm"][layer], w["f_wu"][layer], w["f_cw"][layer],
                             w["f_wd"][layer], state=st, final_g=p["final_norm"] if layer == depth - 1 else None,
                             seq_len=l, tm=tm)
        fconvs.append(tails_to_state(tails, p["f_conv_w"].shape[1] - 1))
    return (h.reshape(b, l, d), jnp.stack(deltas), jnp.stack(dconvs), jnp.stack(fconvs),
            k_out, v_new.reshape(b, l, v_new.shape[1] // vd, vd))


def kernel(x_prompt, x_sample, state_delta, state_dconv, state_fconv, cache_k, cache_v, page_table, a_norm, a_w_in, a_conv_w, a_A_log, a_dt_bias, a_o_gain, a_w_out, kv_norm, w_kv, b_norm, b_w_q, b_lambda, b_subln, b_w_out, f_norm, f_w_up, f_conv_w, f_w_down, final_norm):
    p = dict(a_norm=a_norm, a_w_in=a_w_in, a_conv_w=a_conv_w, a_A_log=a_A_log, a_dt_bias=a_dt_bias,
             a_o_gain=a_o_gain, a_w_out=a_w_out, kv_norm=kv_norm, w_kv=w_kv, b_norm=b_norm, b_w_q=b_w_q,
             b_lambda=b_lambda, b_subln=b_subln, b_w_out=b_w_out, f_norm=f_norm, f_w_up=f_w_up,
             f_conv_w=f_conv_w, f_w_down=f_w_down, final_norm=final_norm)
    w = _prep_weights(p)
    lp = x_prompt.shape[1]
    prompt = _trunk(x_prompt, jnp.arange(lp, dtype=jnp.int32), p, w, delta0=None, dconv0=None, fconv0=None,
                    cache_k=None, cache_v=None, page_table=None)
    ls = x_sample.shape[1]
    past_len = page_table.shape[1] * cache_k.shape[1]
    pool, page = cache_k.shape[:2]
    sample = _trunk(x_sample, past_len + jnp.arange(ls, dtype=jnp.int32), p, w, delta0=state_delta,
                    dconv0=state_dconv, fconv0=state_fconv,
                    cache_k=cache_k.transpose(0, 2, 3, 1).reshape(pool, -1, page),
                    cache_v=cache_v.reshape(pool, page * cache_v.shape[2], cache_v.shape[3]),
                    page_table=page_table)
    return (prompt[0], sample[0]) + prompt[1:] + sample[1:]
```
